```python
import math
import jax, jax.numpy as jnp
from jax import lax
import numpy as np

D_MODEL = 1024
BATCH = 4
SEQ = 8192
DEPTH = 1
DEC_BATCH = 32
DEC_SEQ = 16
PAST_LEN = 1024

CHUNK = 64
N_BAND_CHUNKS = 8
HEAD_DIM = 64
A_HEADS = 4
A_WIDTH = A_HEADS * HEAD_DIM
REL_CLIP = 128
B_HEADS = 4
B_VDIM = 2 * HEAD_DIM
B_QK_WIDTH = B_HEADS * 2 * HEAD_DIM
B_WIDTH = B_HEADS * B_VDIM
M_HEADS = 4
M_WIDTH = M_HEADS * HEAD_DIM
N_MEM = 256
MIX_WIDTH = A_WIDTH + B_WIDTH + M_WIDTH
PROJ_WIDTHS = (A_WIDTH, A_WIDTH, A_WIDTH, B_QK_WIDTH, B_QK_WIDTH, B_WIDTH, M_WIDTH, MIX_WIDTH)
PROJ_TOTAL = sum(PROJ_WIDTHS)
PROJ_SPLITS = tuple(int(s) for s in np.cumsum(PROJ_WIDTHS)[:-1])
ROPE_THETA = 10000.0
Q_BLOCK = 128
DEEPNORM_ALPHA = (2.0 * DEPTH) ** 0.25
DEEPNORM_BETA = (8.0 * DEPTH) ** -0.25
LN_EPS = 1e-5
RMS_EPS = 1e-5
NEG_INF = -1e30

kernel_name = 'hybrid_stream_band_diff_mem_step'


def layer_norm(x, g, b):
    xf = x.astype(jnp.float32)
    mu = jnp.mean(xf, -1, keepdims=True)
    var = jnp.mean(jnp.square(xf - mu), -1, keepdims=True)
    return ((xf - mu) * lax.rsqrt(var + LN_EPS) * g.astype(jnp.float32) + b.astype(jnp.float32)).astype(x.dtype)


def rope(x, pos):
    half = x.shape[-1] // 2
    inv = ROPE_THETA ** (-jnp.arange(half, dtype=jnp.float32) / half)
    ang = pos.astype(jnp.float32)[:, None] * inv[None, :]
    shape = (pos.shape[0],) + (1,) * (x.ndim - 3) + (half,)
    cos = jnp.cos(ang).reshape(shape)
    sin = jnp.sin(ang).reshape(shape)
    x1 = x[..., :half].astype(jnp.float32)
    x2 = x[..., half:].astype(jnp.float32)
    return jnp.concatenate([x1 * cos - x2 * sin, x2 * cos + x1 * sin], -1).astype(x.dtype)


def split_projection(x, w_in, pos):
    Bt, S, _ = x.shape
    h = jnp.einsum('bsd,de->bse', x, w_in)
    a_q, a_k, a_v, b_q, b_k, b_v, m_q, gate = jnp.split(h, PROJ_SPLITS, axis=-1)
    a_q = a_q.reshape(Bt, S, A_HEADS, HEAD_DIM)
    a_k = a_k.reshape(Bt, S, A_HEADS, HEAD_DIM)
    a_v = a_v.reshape(Bt, S, A_HEADS, HEAD_DIM)
    b_q = rope(b_q.reshape(Bt, S, B_HEADS, 2, HEAD_DIM), pos)
    b_k = rope(b_k.reshape(Bt, S, B_HEADS, 2, HEAD_DIM), pos)
    b_v = b_v.reshape(Bt, S, B_HEADS, B_VDIM)
    m_q = m_q.reshape(Bt, S, M_HEADS, HEAD_DIM)
    return a_q, a_k, a_v, b_q, b_k, b_v, m_q, gate


def rel_position_bias(table, dist):
    return table[:, jnp.clip(dist, -REL_CLIP, REL_CLIP) + REL_CLIP].astype(jnp.float32)


def band_attention_prompt(q, k, v, rel_bias):
    Bt, S, H, d = q.shape
    nc = S // CHUNK
    nb = N_BAND_CHUNKS + 1
    pad = ((0, 0), (N_BAND_CHUNKS, 0), (0, 0), (0, 0), (0, 0))
    kc = jnp.pad(k.reshape(Bt, nc, CHUNK, H, d), pad)
    vc = jnp.pad(v.reshape(Bt, nc, CHUNK, H, d), pad)
    k_band = jnp.concatenate([kc[:, j:j + nc] for j in range(nb)], axis=2)
    v_band = jnp.concatenate([vc[:, j:j + nc] for j in range(nb)], axis=2)
    qc = q.reshape(Bt, nc, CHUNK, H, d)
    s = jnp.einsum('bcqhd,bckhd->bchqk', qc, k_band).astype(jnp.float32) * (d ** -0.5)
    qi = jnp.arange(CHUNK)
    ki = jnp.arange(nb * CHUNK)
    dist = qi[:, None] + N_BAND_CHUNKS * CHUNK - ki[None, :]
    s = s + rel_position_bias(rel_bias, dist)[None, None]
    valid = (jnp.arange(nc)[:, None] + ki[None, :] // CHUNK) >= N_BAND_CHUNKS
    s = jnp.where(valid[None, :, None, None, :], s, NEG_INF)
    p = jax.nn.softmax(s, axis=-1)
    o = jnp.einsum('bchqk,bckhd->bcqhd', p.astype(v.dtype), v_band)
    return o.reshape(Bt, S, H, d)


def band_attention_sample(q, k_new, v_new, k_cache, v_cache, rel_bias):
    T = q.shape[1]
    R = k_cache.shape[1]
    k = jnp.concatenate([k_cache, k_new], axis=1)
    v = jnp.concatenate([v_cache, v_new], axis=1)
    s = jnp.einsum('bqhd,bkhd->bhqk', q, k).astype(jnp.float32) * (q.shape[-1] ** -0.5)
    dist = jnp.arange(T)[:, None] + R - jnp.arange(R + T)[None, :]
    s = s + rel_position_bias(rel_bias, dist)[None]
    p = jax.nn.softmax(s, axis=-1)
    return jnp.einsum('bhqk,bkhd->bqhd', p.astype(v.dtype), v)


def diff_lambda_value(lp, lambda_init):
    lpf = lp.astype(jnp.float32)
    return jnp.exp(jnp.sum(lpf[0] * lpf[1])) - jnp.exp(jnp.sum(lpf[2] * lpf[3])) + lambda_init


def diff_core(q, k, v, lam, mask):
    s = jnp.einsum('bqhmd,bkhmd->bhmqk', q, k).astype(jnp.float32) * (q.shape[-1] ** -0.5)
    if mask is not None:
        s = jnp.where(mask, s, NEG_INF)
    p = jax.nn.softmax(s, axis=-1)
    w = p[:, :, 0] - lam * p[:, :, 1]
    return jnp.einsum('bhqk,bkhe->bqhe', w.astype(v.dtype), v)


def diff_attention_prompt(q, k, v, lam):
    Bt, S, H, _, d = q.shape
    nblk = S // Q_BLOCK
    qb = jnp.moveaxis(q.reshape(Bt, nblk, Q_BLOCK, H, 2, d), 1, 0)
    k_chunk = jnp.arange(S) // CHUNK

    def block(args):
        qi, i = args
        q_chunk = (i * Q_BLOCK + jnp.arange(Q_BLOCK)) // CHUNK
        mask = k_chunk[None, :] <= q_chunk[:, None]
        return diff_core(qi, k, v, lam, mask)

    out = lax.map(block, (qb, jnp.arange(nblk)))
    return jnp.moveaxis(out, 0, 1).reshape(Bt, S, H, v.shape[-1])


def diff_post(o, g, lambda_init):
    of = o.astype(jnp.float32)
    of = of * lax.rsqrt(jnp.mean(of * of, -1, keepdims=True) + RMS_EPS) * g.astype(jnp.float32) * (1.0 - lambda_init)
    return of.astype(o.dtype)


def memory_kv(mem, w_mem_kv):
    Bt, N, _ = mem.shape
    kv = jnp.einsum('bnd,de->bne', mem, w_mem_kv)
    mk, mv = jnp.split(kv, 2, axis=-1)
    return mk.reshape(Bt, N, M_HEADS, HEAD_DIM), mv.reshape(Bt, N, M_HEADS, HEAD_DIM)


def memory_attention(q, mk, mv):
    s = jnp.einsum('bqhd,bkhd->bhqk', q, mk).astype(jnp.float32) * (q.shape[-1] ** -0.5)
    p = jax.nn.softmax(s, axis=-1)
    return jnp.einsum('bhqk,bkhd->bqhd', p.astype(mv.dtype), mv)


def merge_output(x, o_a, o_b, o_m, gate, w_out, g, b):
    Bt, S, _ = x.shape
    mixed = jnp.concatenate([o_a.reshape(Bt, S, A_WIDTH), o_b.reshape(Bt, S, B_WIDTH), o_m.reshape(Bt, S, M_WIDTH)], -1)
    y = jnp.einsum('bse,ed->bsd', mixed * jax.nn.silu(gate), w_out)
    return layer_norm(DEEPNORM_ALPHA * x + y, g, b)


def setup_inputs(seed: int = 0) -> dict:
    key = jax.random.key(seed)
    ks = jax.random.split(key, 18)
    a_rows = min(N_BAND_CHUNKS * CHUNK, PAST_LEN)
    nrm = jax.random.normal
    return {
        'x_prompt': nrm(ks[0], (BATCH, SEQ, D_MODEL), jnp.float32),
        'x_sample': nrm(ks[1], (DEC_BATCH, DEC_SEQ, D_MODEL), jnp.float32),
        'cache_a_k': nrm(ks[2], (DEPTH, DEC_BATCH, a_rows, A_HEADS, HEAD_DIM), jnp.float32),
        'cache_a_v': nrm(ks[3], (DEPTH, DEC_BATCH, a_rows, A_HEADS, HEAD_DIM), jnp.float32),
        'cache_b_k': nrm(ks[4], (DEPTH, DEC_BATCH, PAST_LEN, B_HEADS, 2, HEAD_DIM), jnp.float32),
        'cache_b_v': nrm(ks[5], (DEPTH, DEC_BATCH, PAST_LEN, B_HEADS, B_VDIM), jnp.float32),
        'cache_mem_k': nrm(ks[6], (DEPTH, DEC_BATCH, N_MEM, M_HEADS, HEAD_DIM), jnp.float32),
        'cache_mem_v': nrm(ks[7], (DEPTH, DEC_BATCH, N_MEM, M_HEADS, HEAD_DIM), jnp.float32),
        'mem_prompt': nrm(ks[8], (BATCH, N_MEM, D_MODEL), jnp.float32),
        'w_in': nrm(ks[9], (DEPTH, D_MODEL, PROJ_TOTAL), jnp.float32) * D_MODEL ** -0.5,
        'w_mem_kv': nrm(ks[10], (DEPTH, D_MODEL, 2 * M_WIDTH), jnp.float32) * D_MODEL ** -0.5,
        'a_rel_bias': nrm(ks[11], (DEPTH, A_HEADS, 2 * REL_CLIP + 1), jnp.float32) * 0.5,
        'diff_lambda': nrm(ks[12], (DEPTH, 4, HEAD_DIM), jnp.float32) * 0.1,
        'diff_subln_g': 1.0 + 0.05 * nrm(ks[13], (DEPTH, B_VDIM), jnp.float32),
        'w_out': nrm(ks[14], (DEPTH, MIX_WIDTH, D_MODEL), jnp.float32) * (MIX_WIDTH ** -0.5) * DEEPNORM_BETA,
        'ln_g': 1.0 + 0.05 * nrm(ks[15], (DEPTH, D_MODEL), jnp.float32),
        'ln_b': 0.05 * nrm(ks[16], (DEPTH, D_MODEL), jnp.float32),
    }


def reference(x_prompt, x_sample, cache_a_k, cache_a_v, cache_b_k, cache_b_v, cache_mem_k, cache_mem_v, mem_prompt,
              w_in, w_mem_kv, a_rel_bias, diff_lambda, diff_subln_g, w_out, ln_g, ln_b):
    S = x_prompt.shape[1]
    T = x_sample.shape[1]
    past = cache_b_k.shape[2]
    pos_p = jnp.arange(S)
    pos_s = past + jnp.arange(T)
    a_rows_p = min(N_BAND_CHUNKS * CHUNK, S)
    xp, xs = x_prompt, x_sample
    ak_p, av_p, bk_p, bv_p, mk_p, mv_p = [], [], [], [], [], []
    ak_s, av_s, bk_s, bv_s = [], [], [], []
    for layer in range(DEPTH):
        lambda_init = 0.8 - 0.6 * math.exp(-0.3 * layer)
        lam = diff_lambda_value(diff_lambda[layer], lambda_init)
        aq, ak, av, bq, bk, bv, mq, gate = split_projection(xp, w_in[layer], pos_p)
        mk, mv = memory_kv(mem_prompt, w_mem_kv[layer])
        o_a = band_attention_prompt(aq, ak, av, a_rel_bias[layer])
        o_b = diff_post(diff_attention_prompt(bq, bk, bv, lam), diff_subln_g[layer], lambda_init)
        o_m = memory_attention(mq, mk, mv)
        xp_next = merge_output(xp, o_a, o_b, o_m, gate, w_out[layer], ln_g[layer], ln_b[layer])
        ak_p.append(ak[:, S - a_rows_p:])
        av_p.append(av[:, S - a_rows_p:])
        bk_p.append(bk)
        bv_p.append(bv)
        mk_p.append(mk)
        mv_p.append(mv)
        sq, sk, sv, tq, tk, tv, nq, sgate = split_projection(xs, w_in[layer], pos_s)
        s_a = band_attention_sample(sq, sk, sv, cache_a_k[layer], cache_a_v[layer], a_rel_bias[layer])
        tk_all = jnp.concatenate([cache_b_k[layer], tk], axis=1)
        tv_all = jnp.concatenate([cache_b_v[layer], tv], axis=1)
        s_b = diff_post(diff_core(tq, tk_all, tv_all, lam, None), diff_subln_g[layer], lambda_init)
        s_m = memory_attention(nq, cache_mem_k[layer], cache_mem_v[layer])
        xs = merge_output(xs, s_a, s_b, s_m, sgate, w_out[layer], ln_g[layer], ln_b[layer])
        ak_s.append(sk)
        av_s.append(sv)
        bk_s.append(tk)
        bv_s.append(tv)
        xp = xp_next
    return (xp, xs,
            jnp.stack(ak_p), jnp.stack(av_p), jnp.stack(bk_p), jnp.stack(bv_p), jnp.stack(mk_p), jnp.stack(mv_p),
            jnp.stack(ak_s), jnp.stack(av_s), jnp.stack(bk_s), jnp.stack(bv_s))
```

```python
import functools
import math

import jax
import jax.numpy as jnp
from jax import lax
from jax.experimental import pallas as pl
from jax.experimental.pallas import tpu as pltpu

F32 = jnp.float32
BF16 = jnp.bfloat16

D_MODEL = 1024
CHUNK = 64
N_BAND_CHUNKS = 8
BAND_ROWS = N_BAND_CHUNKS * CHUNK
HEAD_DIM = 64
A_WIDTH = 256
B_WIDTH = 512
B_VDIM = 128
M_WIDTH = 256
N_MEM = 256
N_HEADS = 4
MIX_WIDTH = A_WIDTH + B_WIDTH + M_WIDTH
REL_CLIP = 128
ROPE_THETA = 10000.0
LN_EPS = 1e-5
RMS_EPS = 1e-5
NEG_INF = -1e30
QK_SCALE = HEAD_DIM ** -0.5

_OFF = {}
_o = 0
for _name, _w in (("aq", A_WIDTH), ("ak", A_WIDTH), ("av", A_WIDTH), ("bq", B_WIDTH), ("bk", B_WIDTH),
                  ("bv", B_WIDTH), ("mq", M_WIDTH), ("gate", MIX_WIDTH)):
    _OFF[_name] = (_o, _o + _w)
    _o += _w
PROJ_TOTAL = _o

LANES = 128
ROW_TILE = 512
MERGE_TILE = 256
VMEM_LIMIT = 56 * 1024 * 1024


def _cparams(sem):
    return pltpu.CompilerParams(dimension_semantics=sem, vmem_limit_bytes=VMEM_LIMIT)


def _rope_slab(x, cos, sin_signed, lo_half):
    left = pltpu.roll(x, LANES - 32, 1)
    right = pltpu.roll(x, 32, 1)
    swapped = jnp.where(lo_half, left, right)
    return x * cos + swapped * sin_signed


def _proj_kernel(x_ref, w_ref, cos_ref, sin_ref, *out_refs, prompt):
    if prompt:
        (aq_ref, akb_ref, avb_ref, akf_ref, avf_ref, bqT_ref, bkb_ref, bvT_ref,
         bkf_ref, bvf_ref, mq_ref, g_ref) = out_refs
    else:
        aq_ref, akf_ref, avf_ref, bq_ref, bkf_ref, bvf_ref, mq_ref, g_ref = out_refs
    xb = x_ref[...].astype(BF16)

    def seg(name):
        lo, hi = _OFF[name]
        return jnp.dot(xb, w_ref[:, lo:hi], preferred_element_type=F32)

    aq_ref[...] = (seg("aq") * QK_SCALE).astype(BF16)
    ak = seg("ak")
    av = seg("av")
    akf_ref[...] = ak
    avf_ref[...] = av
    if prompt:
        akb_ref[...] = ak.astype(BF16)
        avb_ref[...] = av.astype(BF16)

    cos = cos_ref[...]
    sin = sin_ref[...]
    lane = lax.broadcasted_iota(jnp.int32, cos.shape, 1)
    lo_half = (lane % HEAD_DIM) < (HEAD_DIM // 2)
    bq = seg("bq")
    bk = seg("bk")
    bv = seg("bv")
    bvf_ref[...] = bv
    for c in range(B_WIDTH // LANES):
        sl = slice(c * LANES, (c + 1) * LANES)
        q_c = _rope_slab(bq[:, sl], cos, sin, lo_half) * QK_SCALE
        k_c = _rope_slab(bk[:, sl], cos, sin, lo_half)
        bkf_ref[:, sl] = k_c
        if prompt:
            bkb_ref[:, sl] = k_c.astype(BF16)
            bqT_ref[0, sl, :] = q_c.T.astype(BF16)
            bvT_ref[0, sl, :] = bv[:, sl].T.astype(BF16)
        else:
            bq_ref[:, sl] = q_c.astype(BF16)

    mq_ref[...] = (seg("mq") * QK_SCALE).astype(BF16)
    gate = seg("gate")
    g_ref[...] = (gate / (1.0 + jnp.exp(-gate))).astype(BF16)


def _project(x2, w_in_b, cos, sin, *, prompt, rows_per_seq):
    rows = x2.shape[0]
    tm = ROW_TILE
    n_tiles = rows // tm
    pos_tiles = cos.shape[0] // tm
    row_spec = lambda w: pl.BlockSpec((tm, w), lambda i: (i, 0))
    in_specs = [
        row_spec(D_MODEL),
        pl.BlockSpec((D_MODEL, PROJ_TOTAL), lambda i: (0, 0)),
        pl.BlockSpec((tm, LANES), lambda i: (i % pos_tiles, 0)),
        pl.BlockSpec((tm, LANES), lambda i: (i % pos_tiles, 0)),
    ]
    sds = jax.ShapeDtypeStruct
    if prompt:
        tiles_per_seq = rows_per_seq // tm
        n_seq = rows // rows_per_seq
        tail_spec = pl.BlockSpec((tm, A_WIDTH), lambda i: (i // tiles_per_seq, 0))
        t_spec = pl.BlockSpec((1, B_WIDTH, tm), lambda i: (i, 0, 0))
        out_shape = (
            sds((rows, A_WIDTH), BF16), sds((rows, A_WIDTH), BF16), sds((rows, A_WIDTH), BF16),
            sds((n_seq * tm, A_WIDTH), F32), sds((n_seq * tm, A_WIDTH), F32),
            sds((n_tiles, B_WIDTH, tm), BF16), sds((rows, B_WIDTH), BF16), sds((n_tiles, B_WIDTH, tm), BF16),
            sds((rows, B_WIDTH), F32), sds((rows, B_WIDTH), F32),
            sds((rows, M_WIDTH), BF16), sds((rows, MIX_WIDTH), BF16),
        )
        out_specs = (
            row_spec(A_WIDTH), row_spec(A_WIDTH), row_spec(A_WIDTH), tail_spec, tail_spec,
            t_spec, row_spec(B_WIDTH), t_spec, row_spec(B_WIDTH), row_spec(B_WIDTH),
            row_spec(M_WIDTH), row_spec(MIX_WIDTH),
        )
    else:
        out_shape = (
            sds((rows, A_WIDTH), BF16), sds((rows, A_WIDTH), F32), sds((rows, A_WIDTH), F32),
            sds((rows, B_WIDTH), BF16), sds((rows, B_WIDTH), F32), sds((rows, B_WIDTH), F32),
            sds((rows, M_WIDTH), BF16), sds((rows, MIX_WIDTH), BF16),
        )
        out_specs = (
            row_spec(A_WIDTH), row_spec(A_WIDTH), row_spec(A_WIDTH),
            row_spec(B_WIDTH), row_spec(B_WIDTH), row_spec(B_WIDTH),
            row_spec(M_WIDTH), row_spec(MIX_WIDTH),
        )
    return pl.pallas_call(
        functools.partial(_proj_kernel, prompt=prompt),
        out_shape=out_shape,
        grid=(n_tiles,),
        in_specs=in_specs,
        out_specs=out_specs,
        compiler_params=_cparams(("arbitrary",)),
        name="proj_prompt" if prompt else "proj_sample",
    )(x2, w_in_b, cos, sin)


def _mem_kv_kernel(mem_ref, w_ref, mk_ref, mv_ref):
    kv = jnp.dot(mem_ref[...].astype(BF16), w_ref[...], preferred_element_type=F32)
    mk_ref[...] = kv[:, :M_WIDTH]
    mv_ref[...] = kv[:, M_WIDTH:]


def _mem_kv(mem2, w_b):
    rows = mem2.shape[0]
    tm = N_MEM
    return pl.pallas_call(
        _mem_kv_kernel,
        out_shape=(jax.ShapeDtypeStruct((rows, M_WIDTH), F32), jax.ShapeDtypeStruct((rows, M_WIDTH), F32)),
        grid=(rows // tm,),
        in_specs=[pl.BlockSpec((tm, D_MODEL), lambda i: (i, 0)),
                  pl.BlockSpec((D_MODEL, 2 * M_WIDTH), lambda i: (0, 0))],
        out_specs=(pl.BlockSpec((tm, M_WIDTH), lambda i: (i, 0)), pl.BlockSpec((tm, M_WIDTH), lambda i: (i, 0))),
        compiler_params=_cparams(("arbitrary",)),
        name="mem_kv",
    )(mem2, w_b)


def _lambda_value(lp, lambda_init):
    a = jnp.sum(lp[0:1, :] * lp[1:2, :], axis=1, keepdims=True)
    b = jnp.sum(lp[2:3, :] * lp[3:4, :], axis=1, keepdims=True)
    return jnp.exp(a) - jnp.exp(b) + lambda_init


def _diff_post(o, g, lambda_init):
    ms = jnp.mean(o * o, axis=-1, keepdims=True)
    return o * lax.rsqrt(ms + RMS_EPS) * g * (1.0 - lambda_init)


def _diff_attn_kernel(qT_ref, k_ref, vT_ref, lam_ref, g_ref, o_ref, acc0_ref, acc1_ref, *, tile, lambda_init):
    t = pl.program_id(2)
    qT = qT_ref[0].astype(F32)
    row = lax.broadcasted_iota(jnp.int32, qT.shape, 0)
    q_maps = (jnp.where(row < HEAD_DIM, qT, 0.0).astype(BF16),
              jnp.where(row >= HEAD_DIM, qT, 0.0).astype(BF16))
    acc_refs = (acc0_ref, acc1_ref)
    acc0_ref[...] = jnp.zeros_like(acc0_ref)
    acc1_ref[...] = jnp.zeros_like(acc1_ref)

    def step(j, carry, masked):
        k = k_ref[pl.ds(pl.multiple_of(j * tile, tile), tile), :]
        vT = vT_ref[j]
        if masked:
            kc = lax.broadcasted_iota(jnp.int32, (tile, tile), 0) // CHUNK
            qc = lax.broadcasted_iota(jnp.int32, (tile, tile), 1) // CHUNK
            visible = kc <= qc
        new = []
        for mp in range(2):
            m, l = carry[2 * mp], carry[2 * mp + 1]
            s = jnp.dot(k, q_maps[mp], preferred_element_type=F32)
            if masked:
                s = jnp.where(visible, s, NEG_INF)
            m_new = jnp.maximum(m, jnp.max(s, axis=0, keepdims=True))
            alpha = jnp.exp(m - m_new)
            p = jnp.exp(s - m_new)
            l_new = alpha * l + jnp.sum(p, axis=0, keepdims=True)
            pv = jnp.dot(vT, p.astype(BF16), preferred_element_type=F32)
            acc_refs[mp][...] = alpha * acc_refs[mp][...] + pv
            new += [m_new, l_new]
        return tuple(new)

    m_init = jnp.full((1, tile), NEG_INF, F32)
    l_init = jnp.zeros((1, tile), F32)
    carry = lax.fori_loop(0, t, lambda j, c: step(j, c, False), (m_init, l_init, m_init, l_init))
    _, l0, _, l1 = step(t, carry, True)

    lam = _lambda_value(lam_ref[...], lambda_init)
    oT = acc0_ref[...] * (1.0 / l0) - lam * (acc1_ref[...] * (1.0 / l1))
    o_ref[...] = _diff_post(oT.T, g_ref[...], lambda_init)


def _diff_attention(bqT, bkb, bvT, lam_p, subln_g, *, n_seq, seq, lambda_init):
    tile = ROW_TILE
    nq = seq // tile
    rows = n_seq * seq
    bvT4 = bvT.reshape(n_seq, nq, B_WIDTH, tile)
    return pl.pallas_call(
        functools.partial(_diff_attn_kernel, tile=tile, lambda_init=lambda_init),
        out_shape=jax.ShapeDtypeStruct((rows, B_WIDTH), F32),
        grid=(n_seq, N_HEADS, nq),
        in_specs=[
            pl.BlockSpec((1, B_VDIM, tile), lambda b, h, t: (b * nq + t, h, 0)),
            pl.BlockSpec((seq, B_VDIM), lambda b, h, t: (b, h)),
            pl.BlockSpec((None, nq, B_VDIM, tile), lambda b, h, t: (b, 0, h, 0)),
            pl.BlockSpec((4, HEAD_DIM), lambda b, h, t: (0, 0)),
            pl.BlockSpec((1, B_VDIM), lambda b, h, t: (0, 0)),
        ],
        out_specs=pl.BlockSpec((tile, B_VDIM), lambda b, h, t: (b * nq + t, h)),
        scratch_shapes=[pltpu.VMEM((B_VDIM, tile), F32), pltpu.VMEM((B_VDIM, tile), F32)],
        compiler_params=_cparams(("arbitrary", "arbitrary", "arbitrary")),
        name="diff_attn",
    )(bqT, bkb, bvT4, lam_p, subln_g)


def _head_masks(width):
    lane = lax.broadcasted_iota(jnp.int32, (1, width), 1)
    return [(lane >= h * HEAD_DIM) & (lane < (h + 1) * HEAD_DIM) for h in range(N_HEADS)]


def _dot_nt(a, b):
    return lax.dot_general(a, b, (((1,), (1,)), ((), ())), preferred_element_type=F32)


def _softmax_parts(blocks):
    m = functools.reduce(jnp.maximum, [jnp.max(s, axis=-1, keepdims=True) for s in blocks])
    ps = [jnp.exp(s - m) for s in blocks]
    l = functools.reduce(jnp.add, [jnp.sum(p, axis=-1, keepdims=True) for p in ps])
    return ps, l


def _attend(q_masked, keys, vals, biases):
    blocks = []
    for kk, bb in zip(keys, biases):
        s = _dot_nt(q_masked, kk)
        blocks.append(s if bb is None else s + bb)
    ps, l = _softmax_parts(blocks)
    o = functools.reduce(jnp.add, [jnp.dot(p.astype(BF16), vv, preferred_element_type=F32)
                                   for p, vv in zip(ps, vals)])
    return o * (1.0 / l)


def _merge_out(x, mixed, g, w_out_ref, lng, lnb, alpha):
    mg = (mixed * g.astype(F32)).astype(BF16)
    y = jnp.dot(mg, w_out_ref[...], preferred_element_type=F32)
    z = alpha * x + y
    mu = jnp.mean(z, axis=-1, keepdims=True)
    zc = z - mu
    var = jnp.mean(zc * zc, axis=-1, keepdims=True)
    return zc * lax.rsqrt(var + LN_EPS) * lng + lnb


def _merge_kernel(x_ref, aq_ref, k0_ref, k1_ref, k2_ref, v0_ref, v1_ref, v2_ref, mq_ref, mk_ref, mv_ref,
                  ob_ref, g_ref, bias_ref, wout_ref, lng_ref, lnb_ref, y_ref, *, tiles_per_seq, alpha):
    t = pl.program_id(0) % tiles_per_seq
    masks = _head_masks(A_WIDTH)
    aq = aq_ref[...].astype(F32)
    mq = mq_ref[...].astype(F32)
    keys = [k0_ref[...], k1_ref[...], k2_ref[...]]
    vals = [v0_ref[...], v1_ref[...], v2_ref[...]]
    mk = mk_ref[...].astype(BF16)
    mv = mv_ref[...].astype(BF16)
    pen0 = jnp.where(t >= 2, 0.0, NEG_INF).astype(F32)
    pen1 = jnp.where(t >= 1, 0.0, NEG_INF).astype(F32)
    tile = MERGE_TILE
    o_a = jnp.zeros((tile, A_WIDTH), F32)
    o_m = jnp.zeros((tile, M_WIDTH), F32)
    for h in range(N_HEADS):
        qh = jnp.where(masks[h], aq, 0.0).astype(BF16)
        bias = [bias_ref[h, :, 0:tile] + pen0, bias_ref[h, :, tile:2 * tile] + pen1, bias_ref[h, :, 2 * tile:3 * tile]]
        o_a = jnp.where(masks[h], _attend(qh, keys, vals, bias), o_a)
        qmh = jnp.where(masks[h], mq, 0.0).astype(BF16)
        o_m = jnp.where(masks[h], _attend(qmh, [mk], [mv], [None]), o_m)
    mixed = jnp.concatenate([o_a, ob_ref[...], o_m], axis=1)
    y_ref[...] = _merge_out(x_ref[...], mixed, g_ref[...], wout_ref, lng_ref[...], lnb_ref[...], alpha)


def _merge_prompt(x2, aq, akb, avb, mq, mk, mv, ob, g, bias, w_out_b, lng, lnb, *, seq, alpha):
    rows = x2.shape[0]
    tile = MERGE_TILE
    tps = seq // tile
    n_tiles = rows // tile

    def prev(d):
        return lambda i: ((i // tps) * tps + jnp.maximum(i % tps - d, 0), 0)

    row_spec = lambda w: pl.BlockSpec((tile, w), lambda i: (i, 0))
    kv_specs = [pl.BlockSpec((tile, A_WIDTH), prev(d)) for d in (2, 1, 0)]
    const = lambda shape: pl.BlockSpec(shape, lambda i: (0,) * len(shape))
    return pl.pallas_call(
        functools.partial(_merge_kernel, tiles_per_seq=tps, alpha=alpha),
        out_shape=jax.ShapeDtypeStruct((rows, D_MODEL), F32),
        grid=(n_tiles,),
        in_specs=[row_spec(D_MODEL), row_spec(A_WIDTH)] + kv_specs + kv_specs + [
            row_spec(M_WIDTH),
            pl.BlockSpec((N_MEM, M_WIDTH), lambda i: (i // tps, 0)),
            pl.BlockSpec((N_MEM, M_WIDTH), lambda i: (i // tps, 0)),
            row_spec(B_WIDTH), row_spec(MIX_WIDTH),
            const((N_HEADS, tile, 3 * tile)), const((MIX_WIDTH, D_MODEL)), const((1, D_MODEL)), const((1, D_MODEL)),
        ],
        out_specs=row_spec(D_MODEL),
        compiler_params=_cparams(("arbitrary",)),
        name="merge_prompt",
    )(x2, aq, akb, akb, akb, avb, avb, avb, mq, mk, mv, ob, g, bias, w_out_b, lng, lnb)


def _sample_kernel(x_ref, sq_ref, sk_ref, sv_ref, tq_ref, tk_ref, tv_ref, nq_ref, g_ref,
                   cak_ref, cav_ref, cbk_ref, cbv_ref, cmk_ref, cmv_ref,
                   biasc_ref, biasn_ref, lam_ref, subg_ref, wout_ref, lng_ref, lnb_ref, y_ref,
                   *, lambda_init, alpha):
    masks = _head_masks(A_WIDTH)
    sq = sq_ref[...].astype(F32)
    nq = nq_ref[...].astype(F32)
    a_keys = [cak_ref[0].astype(BF16), sk_ref[...].astype(BF16)]
    a_vals = [cav_ref[0].astype(BF16), sv_ref[...].astype(BF16)]
    mk = cmk_ref[0].astype(BF16)
    mv = cmv_ref[0].astype(BF16)
    rows = sq.shape[0]
    o_a = jnp.zeros((rows, A_WIDTH), F32)
    o_m = jnp.zeros((rows, M_WIDTH), F32)
    for h in range(N_HEADS):
        qh = jnp.where(masks[h], sq, 0.0).astype(BF16)
        o_a = jnp.where(masks[h], _attend(qh, a_keys, a_vals, [biasc_ref[h], biasn_ref[h]]), o_a)
        qmh = jnp.where(masks[h], nq, 0.0).astype(BF16)
        o_m = jnp.where(masks[h], _attend(qmh, [mk], [mv], [None]), o_m)

    lam = _lambda_value(lam_ref[...], lambda_init)
    subg = subg_ref[...]
    lane = lax.broadcasted_iota(jnp.int32, (1, B_VDIM), 1)
    o_b = []
    for h in range(N_HEADS):
        sl = slice(h * B_VDIM, (h + 1) * B_VDIM)
        q = tq_ref[:, sl].astype(F32)
        keys = [cbk_ref[0, :, sl].astype(BF16), tk_ref[:, sl].astype(BF16)]
        vals = [cbv_ref[0, :, sl].astype(BF16), tv_ref[:, sl].astype(BF16)]
        o0 = _attend(jnp.where(lane < HEAD_DIM, q, 0.0).astype(BF16), keys, vals, [None, None])
        o1 = _attend(jnp.where(lane >= HEAD_DIM, q, 0.0).astype(BF16), keys, vals, [None, None])
        o_b.append(_diff_post(o0 - lam * o1, subg, lambda_init))
    mixed = jnp.concatenate([o_a] + o_b + [o_m], axis=1)
    y_ref[...] = _merge_out(x_ref[...], mixed, g_ref[...], wout_ref, lng_ref[...], lnb_ref[...], alpha)


def _sample_step(xs2, sq, sk, sv, tq, tk, tv, nq, g, cak, cav, cbk, cbv, cmk, cmv, biasc, biasn,
                 lam_p, subln_g, w_out_b, lng, lnb, *, n_streams, t_new, lambda_init, alpha):
    past_a = cak.shape[1]
    past_b = cbk.shape[1]
    row_spec = lambda w: pl.BlockSpec((t_new, w), lambda n: (n, 0))
    cache_spec = lambda r, w: pl.BlockSpec((1, r, w), lambda n: (n, 0, 0))
    const = lambda shape: pl.BlockSpec(shape, lambda n: (0,) * len(shape))
    return pl.pallas_call(
        functools.partial(_sample_kernel, lambda_init=lambda_init, alpha=alpha),
        out_shape=jax.ShapeDtypeStruct((n_streams * t_new, D_MODEL), F32),
        grid=(n_streams,),
        in_specs=[
            row_spec(D_MODEL), row_spec(A_WIDTH), row_spec(A_WIDTH), row_spec(A_WIDTH),
            row_spec(B_WIDTH), row_spec(B_WIDTH), row_spec(B_WIDTH), row_spec(M_WIDTH), row_spec(MIX_WIDTH),
            cache_spec(past_a, A_WIDTH), cache_spec(past_a, A_WIDTH),
            cache_spec(past_b, B_WIDTH), cache_spec(past_b, B_WIDTH),
            cache_spec(N_MEM, M_WIDTH), cache_spec(N_MEM, M_WIDTH),
            const((N_HEADS, t_new, past_a)), const((N_HEADS, t_new, t_new)),
            const((4, HEAD_DIM)), const((1, B_VDIM)),
            const((MIX_WIDTH, D_MODEL)), const((1, D_MODEL)), const((1, D_MODEL)),
        ],
        out_specs=row_spec(D_MODEL),
        compiler_params=_cparams(("arbitrary",)),
        name="sample_step",
    )(xs2, sq, sk, sv, tq, tk, tv, nq, g, cak, cav, cbk, cbv, cmk, cmv, biasc, biasn,
      lam_p, subln_g, w_out_b, lng, lnb)


def _rope_tables(pos):
    half = HEAD_DIM // 2
    inv = ROPE_THETA ** (-jnp.arange(half, dtype=F32) / half)
    ang = pos.astype(F32)[:, None] * inv[None, :]
    cos = jnp.cos(ang)
    sin = jnp.sin(ang)
    return jnp.tile(cos, (1, LANES // half)), jnp.concatenate([-sin, sin, -sin, sin], axis=-1)


def _rel_bias(table, dist):
    return table[:, jnp.clip(dist, -REL_CLIP, REL_CLIP) + REL_CLIP].astype(F32)


def _band_bias_prompt(table):
    i = jnp.arange(MERGE_TILE)[:, None]
    j = jnp.arange(3 * MERGE_TILE)[None, :]
    bias = _rel_bias(table, i + BAND_ROWS - j)
    qc = i // CHUNK
    kc = j // CHUNK
    visible = (kc >= qc) & (kc <= qc + N_BAND_CHUNKS)
    return jnp.where(visible[None], bias, NEG_INF)


def kernel(x_prompt, x_sample, cache_a_k, cache_a_v, cache_b_k, cache_b_v, cache_mem_k, cache_mem_v, mem_prompt,
           w_in, w_mem_kv, a_rel_bias, diff_lambda, diff_subln_g, w_out, ln_g, ln_b):
    depth = w_in.shape[0]
    assert depth == 1, "single-layer step only"
    n_seq, seq, _ = x_prompt.shape
    n_streams, t_new, _ = x_sample.shape
    past_a = cache_a_k.shape[2]
    past_b = cache_b_k.shape[2]
    assert seq % ROW_TILE == 0 and BAND_ROWS == ROW_TILE == 2 * MERGE_TILE and past_a == BAND_ROWS
    assert n_streams * t_new == ROW_TILE
    layer = 0
    lambda_init = 0.8 - 0.6 * math.exp(-0.3 * layer)
    alpha = (2.0 * depth) ** 0.25

    w_in_b = w_in[layer].astype(BF16)
    w_mem_b = w_mem_kv[layer].astype(BF16)
    w_out_b = w_out[layer].astype(BF16)
    table = a_rel_bias[layer]
    lam_p = diff_lambda[layer]
    subln_g = diff_subln_g[layer].reshape(1, B_VDIM)
    lng = ln_g[layer].reshape(1, D_MODEL)
    lnb = ln_b[layer].reshape(1, D_MODEL)

    rows = n_seq * seq
    x2 = x_prompt.reshape(rows, D_MODEL)
    cos_p, sin_p = _rope_tables(jnp.arange(seq))
    (aq, akb, avb, akf, avf, bqT, bkb, bvT, bkf, bvf, mq, g) = _project(
        x2, w_in_b, cos_p, sin_p, prompt=True, rows_per_seq=seq)
    mk, mv = _mem_kv(mem_prompt.reshape(n_seq * N_MEM, D_MODEL), w_mem_b)
    ob = _diff_attention(bqT, bkb, bvT, lam_p, subln_g, n_seq=n_seq, seq=seq, lambda_init=lambda_init)
    y_p = _merge_prompt(x2, aq, akb, avb, mq, mk, mv, ob, g, _band_bias_prompt(table), w_out_b, lng, lnb,
                        seq=seq, alpha=alpha)

    xs2 = x_sample.reshape(n_streams * t_new, D_MODEL)
    pos_s = past_b + jnp.arange(t_new)
    cos_s, sin_s = _rope_tables(jnp.tile(pos_s, n_streams))
    sq, sk, sv, tq, tk, tv, nq, sg = _project(xs2, w_in_b, cos_s, sin_s, prompt=False, rows_per_seq=t_new)
    ti = jnp.arange(t_new)[:, None]
    biasc = _rel_bias(table, ti + past_a - jnp.arange(past_a)[None, :])
    biasn = _rel_bias(table, ti - jnp.arange(t_new)[None, :])
    y_s = _sample_step(
        xs2, sq, sk, sv, tq, tk, tv, nq, sg,
        cache_a_k[layer].reshape(n_streams, past_a, A_WIDTH), cache_a_v[layer].reshape(n_streams, past_a, A_WIDTH),
        cache_b_k[layer].reshape(n_streams, past_b, B_WIDTH), cache_b_v[layer].reshape(n_streams, past_b, B_WIDTH),
        cache_mem_k[layer].reshape(n_streams, N_MEM, M_WIDTH), cache_mem_v[layer].reshape(n_streams, N_MEM, M_WIDTH),
        biasc, biasn, lam_p, subln_g, w_out_b, lng, lnb,
        n_streams=n_streams, t_new=t_new, lambda_init=lambda_init, alpha=alpha)

    hd = (N_HEADS, HEAD_DIM)
    return (
        y_p.reshape(n_seq, seq, D_MODEL),
        y_s.reshape(n_streams, t_new, D_MODEL),
        akf.reshape(1, n_seq, BAND_ROWS, *hd),
        avf.reshape(1, n_seq, BAND_ROWS, *hd),
        bkf.reshape(1, n_seq, seq, N_HEADS, 2, HEAD_DIM),
        bvf.reshape(1, n_seq, seq, N_HEADS, B_VDIM),
        mk.reshape(1, n_seq, N_MEM, *hd),
        mv.reshape(1, n_seq, N_MEM, *hd),
        sk.reshape(1, n_streams, t_new, *hd),
        sv.reshape(1, n_streams, t_new, *hd),
        tk.reshape(1, n_streams, t_new, N_HEADS, 2, HEAD_DIM),
        tv.reshape(1, n_streams, t_new, N_HEADS, B_VDIM),
    )
```

```python
import functools
import math

import jax
import jax.numpy as jnp
from jax import lax
from jax.experimental import pallas as pl
from jax.experimental.pallas import tpu as pltpu

F32 = jnp.float32
BF16 = jnp.bfloat16

D_MODEL = 1024
CHUNK = 64
N_BAND_CHUNKS = 8
BAND_ROWS = N_BAND_CHUNKS * CHUNK
HEAD_DIM = 64
A_WIDTH = 256
B_WIDTH = 512
B_VDIM = 128
M_WIDTH = 256
N_MEM = 256
N_HEADS = 4
MIX_WIDTH = A_WIDTH + B_WIDTH + M_WIDTH
REL_CLIP = 128
ROPE_THETA = 10000.0
LN_EPS = 1e-5
RMS_EPS = 1e-5
NEG_INF = -1e30
QK_SCALE = HEAD_DIM ** -0.5
LOG2E = math.log2(math.e)

_OFF = {}
_o = 0
for _name, _w in (("aq", A_WIDTH), ("ak", A_WIDTH), ("av", A_WIDTH), ("bq", B_WIDTH), ("bk", B_WIDTH),
                  ("bv", B_WIDTH), ("mq", M_WIDTH), ("gate", MIX_WIDTH)):
    _OFF[_name] = (_o, _o + _w)
    _o += _w
PROJ_TOTAL = _o

LANES = 128
ROW_TILE = 512
KV_TILE = ROW_TILE // 2
ONES_ROWS = 16
MERGE_TILE = 256
VMEM_LIMIT = 56 * 1024 * 1024


def _cparams(sem):
    return pltpu.CompilerParams(dimension_semantics=sem, vmem_limit_bytes=VMEM_LIMIT)


def _rope_slab(x, cos, sin_signed, lo_half):
    left = pltpu.roll(x, LANES - 32, 1)
    right = pltpu.roll(x, 32, 1)
    swapped = jnp.where(lo_half, left, right)
    return x * cos + swapped * sin_signed


def _proj_kernel(x_ref, w_ref, cos_ref, sin_ref, *out_refs, prompt):
    if prompt:
        (aq_ref, akb_ref, avb_ref, akf_ref, avf_ref, bqT_ref, bkb_ref, bvT_ref,
         bkf_ref, bvf_ref, mq_ref, g_ref) = out_refs
    else:
        aq_ref, akf_ref, avf_ref, bq_ref, bkf_ref, bvf_ref, mq_ref, g_ref = out_refs
    xb = x_ref[...].astype(BF16)

    def seg(name):
        lo, hi = _OFF[name]
        return jnp.dot(xb, w_ref[:, lo:hi], preferred_element_type=F32)

    aq_ref[...] = (seg("aq") * QK_SCALE).astype(BF16)
    ak = seg("ak")
    av = seg("av")
    akf_ref[...] = ak
    avf_ref[...] = av
    if prompt:
        akb_ref[...] = ak.astype(BF16)
        avb_ref[...] = av.astype(BF16)

    cos = cos_ref[...]
    sin = sin_ref[...]
    lane = lax.broadcasted_iota(jnp.int32, cos.shape, 1)
    lo_half = (lane % HEAD_DIM) < (HEAD_DIM // 2)
    bq = seg("bq")
    bk = seg("bk")
    bv = seg("bv")
    bvf_ref[...] = bv
    for c in range(B_WIDTH // LANES):
        sl = slice(c * LANES, (c + 1) * LANES)
        q_c = _rope_slab(bq[:, sl], cos, sin, lo_half)
        k_c = _rope_slab(bk[:, sl], cos, sin, lo_half)
        bkf_ref[:, sl] = k_c
        if prompt:
            bkb_ref[:, sl] = k_c.astype(BF16)
            bqT_ref[0, sl, :] = (q_c * (QK_SCALE * LOG2E)).T.astype(BF16)
            vT_c = bv[:, sl].T.astype(BF16)
            for half in range(ROW_TILE // KV_TILE):
                bvT_ref[half, sl, :] = vT_c[:, half * KV_TILE:(half + 1) * KV_TILE]
        else:
            bq_ref[:, sl] = (q_c * QK_SCALE).astype(BF16)

    mq_ref[...] = (seg("mq") * QK_SCALE).astype(BF16)
    gate = seg("gate")
    g_ref[...] = (gate / (1.0 + jnp.exp(-gate))).astype(BF16)


def _project(x2, w_in_b, cos, sin, *, prompt, rows_per_seq):
    rows = x2.shape[0]
    tm = ROW_TILE
    n_tiles = rows // tm
    pos_tiles = cos.shape[0] // tm
    row_spec = lambda w: pl.BlockSpec((tm, w), lambda i: (i, 0))
    in_specs = [
        row_spec(D_MODEL),
        pl.BlockSpec((D_MODEL, PROJ_TOTAL), lambda i: (0, 0)),
        pl.BlockSpec((tm, LANES), lambda i: (i % pos_tiles, 0)),
        pl.BlockSpec((tm, LANES), lambda i: (i % pos_tiles, 0)),
    ]
    sds = jax.ShapeDtypeStruct
    if prompt:
        tiles_per_seq = rows_per_seq // tm
        n_seq = rows // rows_per_seq
        tail_spec = pl.BlockSpec((tm, A_WIDTH), lambda i: (i // tiles_per_seq, 0))
        t_spec = pl.BlockSpec((1, B_WIDTH, tm), lambda i: (i, 0, 0))
        kv_per_row_tile = tm // KV_TILE
        vt_spec = pl.BlockSpec((kv_per_row_tile, B_WIDTH, KV_TILE), lambda i: (i, 0, 0))
        out_shape = (
            sds((rows, A_WIDTH), BF16), sds((rows, A_WIDTH), BF16), sds((rows, A_WIDTH), BF16),
            sds((n_seq * tm, A_WIDTH), F32), sds((n_seq * tm, A_WIDTH), F32),
            sds((n_tiles, B_WIDTH, tm), BF16), sds((rows, B_WIDTH), BF16),
            sds((n_tiles * kv_per_row_tile, B_WIDTH, KV_TILE), BF16),
            sds((rows, B_WIDTH), F32), sds((rows, B_WIDTH), F32),
            sds((rows, M_WIDTH), BF16), sds((rows, MIX_WIDTH), BF16),
        )
        out_specs = (
            row_spec(A_WIDTH), row_spec(A_WIDTH), row_spec(A_WIDTH), tail_spec, tail_spec,
            t_spec, row_spec(B_WIDTH), vt_spec, row_spec(B_WIDTH), row_spec(B_WIDTH),
            row_spec(M_WIDTH), row_spec(MIX_WIDTH),
        )
    else:
        out_shape = (
            sds((rows, A_WIDTH), BF16), sds((rows, A_WIDTH), F32), sds((rows, A_WIDTH), F32),
            sds((rows, B_WIDTH), BF16), sds((rows, B_WIDTH), F32), sds((rows, B_WIDTH), F32),
            sds((rows, M_WIDTH), BF16), sds((rows, MIX_WIDTH), BF16),
        )
        out_specs = (
            row_spec(A_WIDTH), row_spec(A_WIDTH), row_spec(A_WIDTH),
            row_spec(B_WIDTH), row_spec(B_WIDTH), row_spec(B_WIDTH),
            row_spec(M_WIDTH), row_spec(MIX_WIDTH),
        )
    return pl.pallas_call(
        functools.partial(_proj_kernel, prompt=prompt),
        out_shape=out_shape,
        grid=(n_tiles,),
        in_specs=in_specs,
        out_specs=out_specs,
        compiler_params=_cparams(("arbitrary",)),
        name="proj_prompt" if prompt else "proj_sample",
    )(x2, w_in_b, cos, sin)


def _mem_kv_kernel(mem_ref, w_ref, mk_ref, mv_ref):
    kv = jnp.dot(mem_ref[...].astype(BF16), w_ref[...], preferred_element_type=F32)
    mk_ref[...] = kv[:, :M_WIDTH]
    mv_ref[...] = kv[:, M_WIDTH:]


def _mem_kv(mem2, w_b):
    rows = mem2.shape[0]
    tm = N_MEM
    return pl.pallas_call(
        _mem_kv_kernel,
        out_shape=(jax.ShapeDtypeStruct((rows, M_WIDTH), F32), jax.ShapeDtypeStruct((rows, M_WIDTH), F32)),
        grid=(rows // tm,),
        in_specs=[pl.BlockSpec((tm, D_MODEL), lambda i: (i, 0)),
                  pl.BlockSpec((D_MODEL, 2 * M_WIDTH), lambda i: (0, 0))],
        out_specs=(pl.BlockSpec((tm, M_WIDTH), lambda i: (i, 0)), pl.BlockSpec((tm, M_WIDTH), lambda i: (i, 0))),
        compiler_params=_cparams(("arbitrary",)),
        name="mem_kv",
    )(mem2, w_b)


def _lambda_value(lp, lambda_init):
    a = jnp.sum(lp[0:1, :] * lp[1:2, :], axis=1, keepdims=True)
    b = jnp.sum(lp[2:3, :] * lp[3:4, :], axis=1, keepdims=True)
    return jnp.exp(a) - jnp.exp(b) + lambda_init


def _diff_post(o, g, lambda_init):
    ms = jnp.mean(o * o, axis=-1, keepdims=True)
    return o * lax.rsqrt(ms + RMS_EPS) * g * (1.0 - lambda_init)


def _diff_attn_kernel(qT_ref, k_ref, vT_ref, lam_ref, g_ref, o_ref,
                      s0_ref, s1_ref, p0_ref, p1_ref, a0_ref, a1_ref, m_ref, acc_ref, *, tq, tk, lambda_init):
    t = pl.program_id(2)
    qT = qT_ref[0].astype(F32)
    row = lax.broadcasted_iota(jnp.int32, qT.shape, 0)
    q_maps = (jnp.where(row < HEAD_DIM, qT, 0.0).astype(BF16),
              jnp.where(row >= HEAD_DIM, qT, 0.0).astype(BF16))
    acc_ref[...] = jnp.zeros_like(acc_ref)
    m_ref[...] = jnp.full(m_ref.shape, NEG_INF, F32)
    s_refs = (s0_ref, s1_ref)
    p_refs = (p0_ref, p1_ref)
    a_refs = (a0_ref, a1_ref)
    ones_rows = jnp.ones((ONES_ROWS, tk), BF16)

    def qk(j, buf):
        k = k_ref[pl.ds(pl.multiple_of(j * tk, tk), tk), :]
        for mp in range(2):
            s_refs[buf][mp] = jnp.dot(k, q_maps[mp], preferred_element_type=F32)

    def softmax(buf, visible):
        for mp in range(2):
            s = s_refs[buf][mp]
            if visible is not None:
                s = jnp.where(visible, s, NEG_INF)
            m_new = jnp.maximum(m_ref[mp], jnp.max(s, axis=0, keepdims=True))
            a_refs[buf][mp] = jnp.exp2(m_ref[mp] - m_new)
            m_ref[mp] = m_new
            p_refs[buf][mp] = jnp.exp2(s - m_new).astype(BF16)

    def pv(j, buf):
        vT = jnp.concatenate([vT_ref[j], ones_rows], axis=0)
        for mp in range(2):
            upd = jnp.dot(vT, p_refs[buf][mp], preferred_element_type=F32)
            acc_ref[mp] = a_refs[buf][mp] * acc_ref[mp] + upd

    def diag_visible(half):
        kc = lax.broadcasted_iota(jnp.int32, (tk, tq), 0) // CHUNK + half * (tk // CHUNK)
        qc = lax.broadcasted_iota(jnp.int32, (tk, tq), 1) // CHUNK
        return kc <= qc

    def tick_pair(j, masked):
        pv(j - 2, 0)
        qk(j, 0)
        softmax(1, None)
        pv(j - 1, 1)
        qk(j + 1, 1)
        softmax(0, diag_visible(0) if masked else None)

    qk(0, 0)
    qk(1, 1)
    softmax(0, diag_visible(0) | (t > 0))

    def body(i, c):
        tick_pair(2 + 2 * i, False)
        return c

    lax.fori_loop(0, t - 1, body, 0)

    @pl.when(t > 0)
    def _():
        tick_pair(2 * t, True)

    pv(2 * t, 0)
    softmax(1, diag_visible(1))
    pv(2 * t + 1, 1)

    lam = _lambda_value(lam_ref[...], lambda_init)
    inv0 = 1.0 / acc_ref[0, B_VDIM:B_VDIM + 1]
    inv1 = 1.0 / acc_ref[1, B_VDIM:B_VDIM + 1]
    oT = acc_ref[0, :B_VDIM] * inv0 - lam * (acc_ref[1, :B_VDIM] * inv1)
    o_ref[...] = _diff_post(oT.T, g_ref[...], lambda_init)


def _diff_attention(bqT, bkb, bvT, lam_p, subln_g, *, n_seq, seq, lambda_init):
    tq, tk = ROW_TILE, KV_TILE
    nq = seq // tq
    nk = seq // tk
    rows = n_seq * seq
    bvT4 = bvT.reshape(n_seq, nk, B_WIDTH, tk)
    return pl.pallas_call(
        functools.partial(_diff_attn_kernel, tq=tq, tk=tk, lambda_init=lambda_init),
        out_shape=jax.ShapeDtypeStruct((rows, B_WIDTH), F32),
        grid=(n_seq, N_HEADS, nq),
        in_specs=[
            pl.BlockSpec((1, B_VDIM, tq), lambda b, h, t: (b * nq + t, h, 0)),
            pl.BlockSpec((seq, B_VDIM), lambda b, h, t: (b, h)),
            pl.BlockSpec((None, nk, B_VDIM, tk), lambda b, h, t: (b, 0, h, 0)),
            pl.BlockSpec((4, HEAD_DIM), lambda b, h, t: (0, 0)),
            pl.BlockSpec((1, B_VDIM), lambda b, h, t: (0, 0)),
        ],
        out_specs=pl.BlockSpec((tq, B_VDIM), lambda b, h, t: (b * nq + t, h)),
        scratch_shapes=[
            pltpu.VMEM((2, tk, tq), F32), pltpu.VMEM((2, tk, tq), F32),
            pltpu.VMEM((2, tk, tq), BF16), pltpu.VMEM((2, tk, tq), BF16),
            pltpu.VMEM((2, 1, tq), F32), pltpu.VMEM((2, 1, tq), F32),
            pltpu.VMEM((2, 1, tq), F32),
            pltpu.VMEM((2, B_VDIM + ONES_ROWS, tq), F32),
        ],
        compiler_params=_cparams(("arbitrary", "arbitrary", "arbitrary")),
        name="diff_attn",
    )(bqT, bkb, bvT4, lam_p, subln_g)


def _head_masks(width):
    lane = lax.broadcasted_iota(jnp.int32, (1, width), 1)
    return [(lane >= h * HEAD_DIM) & (lane < (h + 1) * HEAD_DIM) for h in range(N_HEADS)]


def _dot_nt(a, b):
    return lax.dot_general(a, b, (((1,), (1,)), ((), ())), preferred_element_type=F32)


def _softmax_parts(blocks):
    m = functools.reduce(jnp.maximum, [jnp.max(s, axis=-1, keepdims=True) for s in blocks])
    ps = [jnp.exp(s - m) for s in blocks]
    l = functools.reduce(jnp.add, [jnp.sum(p, axis=-1, keepdims=True) for p in ps])
    return ps, l


def _attend(q_masked, keys, vals, biases):
    blocks = []
    for kk, bb in zip(keys, biases):
        s = _dot_nt(q_masked, kk)
        blocks.append(s if bb is None else s + bb)
    ps, l = _softmax_parts(blocks)
    o = functools.reduce(jnp.add, [jnp.dot(p.astype(BF16), vv, preferred_element_type=F32)
                                   for p, vv in zip(ps, vals)])
    return o * (1.0 / l)


def _merge_out(x, mixed, g, w_out_ref, lng, lnb, alpha):
    mg = (mixed * g.astype(F32)).astype(BF16)
    y = jnp.dot(mg, w_out_ref[...], preferred_element_type=F32)
    z = alpha * x + y
    mu = jnp.mean(z, axis=-1, keepdims=True)
    zc = z - mu
    var = jnp.mean(zc * zc, axis=-1, keepdims=True)
    return zc * lax.rsqrt(var + LN_EPS) * lng + lnb


def _merge_kernel(x_ref, aq_ref, k0_ref, k1_ref, k2_ref, v0_ref, v1_ref, v2_ref, mq_ref, mk_ref, mv_ref,
                  ob_ref, g_ref, bias_ref, wout_ref, lng_ref, lnb_ref, y_ref, *, tiles_per_seq, alpha):
    t = pl.program_id(0) % tiles_per_seq
    masks = _head_masks(A_WIDTH)
    aq = aq_ref[...].astype(F32)
    mq = mq_ref[...].astype(F32)
    keys = [k0_ref[...], k1_ref[...], k2_ref[...]]
    vals = [v0_ref[...], v1_ref[...], v2_ref[...]]
    mk = mk_ref[...].astype(BF16)
    mv = mv_ref[...].astype(BF16)
    pen0 = jnp.where(t >= 2, 0.0, NEG_INF).astype(F32)
    pen1 = jnp.where(t >= 1, 0.0, NEG_INF).astype(F32)
    tile = MERGE_TILE
    o_a = jnp.zeros((tile, A_WIDTH), F32)
    o_m = jnp.zeros((tile, M_WIDTH), F32)
    for h in range(N_HEADS):
        qh = jnp.where(masks[h], aq, 0.0).astype(BF16)
        bias = [bias_ref[h, :, 0:tile] + pen0, bias_ref[h, :, tile:2 * tile] + pen1, bias_ref[h, :, 2 * tile:3 * tile]]
        o_a = jnp.where(masks[h], _attend(qh, keys, vals, bias), o_a)
        qmh = jnp.where(masks[h], mq, 0.0).astype(BF16)
        o_m = jnp.where(masks[h], _attend(qmh, [mk], [mv], [None]), o_m)
    mixed = jnp.concatenate([o_a, ob_ref[...], o_m], axis=1)
    y_ref[...] = _merge_out(x_ref[...], mixed, g_ref[...], wout_ref, lng_ref[...], lnb_ref[...], alpha)


def _merge_prompt(x2, aq, akb, avb, mq, mk, mv, ob, g, bias, w_out_b, lng, lnb, *, seq, alpha):
    rows = x2.shape[0]
    tile = MERGE_TILE
    tps = seq // tile
    n_tiles = rows // tile

    def prev(d):
        return lambda i: ((i // tps) * tps + jnp.maximum(i % tps - d, 0), 0)

    row_spec = lambda w: pl.BlockSpec((tile, w), lambda i: (i, 0))
    kv_specs = [pl.BlockSpec((tile, A_WIDTH), prev(d)) for d in (2, 1, 0)]
    const = lambda shape: pl.BlockSpec(shape, lambda i: (0,) * len(shape))
    return pl.pallas_call(
        functools.partial(_merge_kernel, tiles_per_seq=tps, alpha=alpha),
        out_shape=jax.ShapeDtypeStruct((rows, D_MODEL), F32),
        grid=(n_tiles,),
        in_specs=[row_spec(D_MODEL), row_spec(A_WIDTH)] + kv_specs + kv_specs + [
            row_spec(M_WIDTH),
            pl.BlockSpec((N_MEM, M_WIDTH), lambda i: (i // tps, 0)),
            pl.BlockSpec((N_MEM, M_WIDTH), lambda i: (i // tps, 0)),
            row_spec(B_WIDTH), row_spec(MIX_WIDTH),
            const((N_HEADS, tile, 3 * tile)), const((MIX_WIDTH, D_MODEL)), const((1, D_MODEL)), const((1, D_MODEL)),
        ],
        out_specs=row_spec(D_MODEL),
        compiler_params=_cparams(("arbitrary",)),
        name="merge_prompt",
    )(x2, aq, akb, akb, akb, avb, avb, avb, mq, mk, mv, ob, g, bias, w_out_b, lng, lnb)


def _sample_kernel(x_ref, sq_ref, sk_ref, sv_ref, tq_ref, tk_ref, tv_ref, nq_ref, g_ref,
                   cak_ref, cav_ref, cbk_ref, cbv_ref, cmk_ref, cmv_ref,
                   biasc_ref, biasn_ref, lam_ref, subg_ref, wout_ref, lng_ref, lnb_ref, y_ref,
                   *, lambda_init, alpha):
    masks = _head_masks(A_WIDTH)
    sq = sq_ref[...].astype(F32)
    nq = nq_ref[...].astype(F32)
    a_keys = [cak_ref[0].astype(BF16), sk_ref[...].astype(BF16)]
    a_vals = [cav_ref[0].astype(BF16), sv_ref[...].astype(BF16)]
    mk = cmk_ref[0].astype(BF16)
    mv = cmv_ref[0].astype(BF16)
    rows = sq.shape[0]
    o_a = jnp.zeros((rows, A_WIDTH), F32)
    o_m = jnp.zeros((rows, M_WIDTH), F32)
    for h in range(N_HEADS):
        qh = jnp.where(masks[h], sq, 0.0).astype(BF16)
        o_a = jnp.where(masks[h], _attend(qh, a_keys, a_vals, [biasc_ref[h], biasn_ref[h]]), o_a)
        qmh = jnp.where(masks[h], nq, 0.0).astype(BF16)
        o_m = jnp.where(masks[h], _attend(qmh, [mk], [mv], [None]), o_m)

    lam = _lambda_value(lam_ref[...], lambda_init)
    subg = subg_ref[...]
    lane = lax.broadcasted_iota(jnp.int32, (1, B_VDIM), 1)
    o_b = []
    for h in range(N_HEADS):
        sl = slice(h * B_VDIM, (h + 1) * B_VDIM)
        q = tq_ref[:, sl].astype(F32)
        keys = [cbk_ref[0, :, sl].astype(BF16), tk_ref[:, sl].astype(BF16)]
        vals = [cbv_ref[0, :, sl].astype(BF16), tv_ref[:, sl].astype(BF16)]
        o0 = _attend(jnp.where(lane < HEAD_DIM, q, 0.0).astype(BF16), keys, vals, [None, None])
        o1 = _attend(jnp.where(lane >= HEAD_DIM, q, 0.0).astype(BF16), keys, vals, [None, None])
        o_b.append(_diff_post(o0 - lam * o1, subg, lambda_init))
    mixed = jnp.concatenate([o_a] + o_b + [o_m], axis=1)
    y_ref[...] = _merge_out(x_ref[...], mixed, g_ref[...], wout_ref, lng_ref[...], lnb_ref[...], alpha)


def _sample_step(xs2, sq, sk, sv, tq, tk, tv, nq, g, cak, cav, cbk, cbv, cmk, cmv, biasc, biasn,
                 lam_p, subln_g, w_out_b, lng, lnb, *, n_streams, t_new, lambda_init, alpha):
    past_a = cak.shape[1]
    past_b = cbk.shape[1]
    row_spec = lambda w: pl.BlockSpec((t_new, w), lambda n: (n, 0))
    cache_spec = lambda r, w: pl.BlockSpec((1, r, w), lambda n: (n, 0, 0))
    const = lambda shape: pl.BlockSpec(shape, lambda n: (0,) * len(shape))
    return pl.pallas_call(
        functools.partial(_sample_kernel, lambda_init=lambda_init, alpha=alpha),
        out_shape=jax.ShapeDtypeStruct((n_streams * t_new, D_MODEL), F32),
        grid=(n_streams,),
        in_specs=[
            row_spec(D_MODEL), row_spec(A_WIDTH), row_spec(A_WIDTH), row_spec(A_WIDTH),
            row_spec(B_WIDTH), row_spec(B_WIDTH), row_spec(B_WIDTH), row_spec(M_WIDTH), row_spec(MIX_WIDTH),
            cache_spec(past_a, A_WIDTH), cache_spec(past_a, A_WIDTH),
            cache_spec(past_b, B_WIDTH), cache_spec(past_b, B_WIDTH),
            cache_spec(N_MEM, M_WIDTH), cache_spec(N_MEM, M_WIDTH),
            const((N_HEADS, t_new, past_a)), const((N_HEADS, t_new, t_new)),
            const((4, HEAD_DIM)), const((1, B_VDIM)),
            const((MIX_WIDTH, D_MODEL)), const((1, D_MODEL)), const((1, D_MODEL)),
        ],
        out_specs=row_spec(D_MODEL),
        compiler_params=_cparams(("arbitrary",)),
        name="sample_step",
    )(xs2, sq, sk, sv, tq, tk, tv, nq, g, cak, cav, cbk, cbv, cmk, cmv, biasc, biasn,
      lam_p, subln_g, w_out_b, lng, lnb)


def _rope_tables(pos):
    half = HEAD_DIM // 2
    inv = ROPE_THETA ** (-jnp.arange(half, dtype=F32) / half)
    ang = pos.astype(F32)[:, None] * inv[None, :]
    cos = jnp.cos(ang)
    sin = jnp.sin(ang)
    return jnp.tile(cos, (1, LANES // half)), jnp.concatenate([-sin, sin, -sin, sin], axis=-1)


def _rel_bias(table, dist):
    return table[:, jnp.clip(dist, -REL_CLIP, REL_CLIP) + REL_CLIP].astype(F32)


def _band_bias_prompt(table):
    n, width, period = MERGE_TILE, 3 * MERGE_TILE, 4 * MERGE_TILE
    u = jnp.arange(period)
    dist = jnp.where(u < width, BAND_ROWS - u, BAND_ROWS + period - u)
    diagonals = _rel_bias(table, dist)
    flat = jnp.tile(diagonals, (1, n))[:, :n * (period - 1)]
    bias = flat.reshape(N_HEADS, n, period - 1)[:, :, :width]
    i = jnp.arange(n)[:, None]
    j = jnp.arange(width)[None, :]
    qc = i // CHUNK
    kc = j // CHUNK
    visible = (kc >= qc) & (kc <= qc + N_BAND_CHUNKS)
    return jnp.where(visible[None], bias, NEG_INF)


def kernel(x_prompt, x_sample, cache_a_k, cache_a_v, cache_b_k, cache_b_v, cache_mem_k, cache_mem_v, mem_prompt,
           w_in, w_mem_kv, a_rel_bias, diff_lambda, diff_subln_g, w_out, ln_g, ln_b):
    depth = w_in.shape[0]
    assert depth == 1, "single-layer step only"
    n_seq, seq, _ = x_prompt.shape
    n_streams, t_new, _ = x_sample.shape
    past_a = cache_a_k.shape[2]
    past_b = cache_b_k.shape[2]
    assert seq % ROW_TILE == 0 and BAND_ROWS == ROW_TILE == 2 * MERGE_TILE and past_a == BAND_ROWS
    assert n_streams * t_new == ROW_TILE
    layer = 0
    lambda_init = 0.8 - 0.6 * math.exp(-0.3 * layer)
    alpha = (2.0 * depth) ** 0.25

    w_in_b = w_in[layer].astype(BF16)
    w_mem_b = w_mem_kv[layer].astype(BF16)
    w_out_b = w_out[layer].astype(BF16)
    table = a_rel_bias[layer]
    lam_p = diff_lambda[layer]
    subln_g = diff_subln_g[layer].reshape(1, B_VDIM)
    lng = ln_g[layer].reshape(1, D_MODEL)
    lnb = ln_b[layer].reshape(1, D_MODEL)

    rows = n_seq * seq
    x2 = x_prompt.reshape(rows, D_MODEL)
    cos_p, sin_p = _rope_tables(jnp.arange(seq))
    (aq, akb, avb, akf, avf, bqT, bkb, bvT, bkf, bvf, mq, g) = _project(
        x2, w_in_b, cos_p, sin_p, prompt=True, rows_per_seq=seq)
    mk, mv = _mem_kv(mem_prompt.reshape(n_seq * N_MEM, D_MODEL), w_mem_b)
    ob = _diff_attention(bqT, bkb, bvT, lam_p, subln_g, n_seq=n_seq, seq=seq, lambda_init=lambda_init)
    y_p = _merge_prompt(x2, aq, akb, avb, mq, mk, mv, ob, g, _band_bias_prompt(table), w_out_b, lng, lnb,
                        seq=seq, alpha=alpha)

    xs2 = x_sample.reshape(n_streams * t_new, D_MODEL)
    pos_s = past_b + jnp.arange(t_new)
    cos_s, sin_s = _rope_tables(jnp.tile(pos_s, n_streams))
    sq, sk, sv, tq, tk, tv, nq, sg = _project(xs2, w_in_b, cos_s, sin_s, prompt=False, rows_per_seq=t_new)
    ti = jnp.arange(t_new)[:, None]
    biasc = _rel_bias(table, ti + past_a - jnp.arange(past_a)[None, :])
    biasn = _rel_bias(table, ti - jnp.arange(t_new)[None, :])
    y_s = _sample_step(
        xs2, sq, sk, sv, tq, tk, tv, nq, sg,
        cache_a_k[layer].reshape(n_streams, past_a, A_WIDTH), cache_a_v[layer].reshape(n_streams, past_a, A_WIDTH),
        cache_b_k[layer].reshape(n_streams, past_b, B_WIDTH), cache_b_v[layer].reshape(n_streams, past_b, B_WIDTH),
        cache_mem_k[layer].reshape(n_streams, N_MEM, M_WIDTH), cache_mem_v[layer].reshape(n_streams, N_MEM, M_WIDTH),
        biasc, biasn, lam_p, subln_g, w_out_b, lng, lnb,
        n_streams=n_streams, t_new=t_new, lambda_init=lambda_init, alpha=alpha)

    hd = (N_HEADS, HEAD_DIM)
    return (
        y_p.reshape(n_seq, seq, D_MODEL),
        y_s.reshape(n_streams, t_new, D_MODEL),
        akf.reshape(1, n_seq, BAND_ROWS, *hd),
        avf.reshape(1, n_seq, BAND_ROWS, *hd),
        bkf.reshape(1, n_seq, seq, N_HEADS, 2, HEAD_DIM),
        bvf.reshape(1, n_seq, seq, N_HEADS, B_VDIM),
        mk.reshape(1, n_seq, N_MEM, *hd),
        mv.reshape(1, n_seq, N_MEM, *hd),
        sk.reshape(1, n_streams, t_new, *hd),
        sv.reshape(1, n_streams, t_new, *hd),
        tk.reshape(1, n_streams, t_new, N_HEADS, 2, HEAD_DIM),
        tv.reshape(1, n_streams, t_new, N_HEADS, B_VDIM),
    )
```

```python
import functools
import math

import jax
import jax.numpy as jnp
from jax import lax
from jax.experimental import pallas as pl
from jax.experimental.pallas import tpu as pltpu

F32 = jnp.float32
BF16 = jnp.bfloat16

D_MODEL = 1024
CHUNK = 64
N_BAND_CHUNKS = 8
BAND_ROWS = N_BAND_CHUNKS * CHUNK
HEAD_DIM = 64
A_WIDTH = 256
B_WIDTH = 512
B_VDIM = 128
M_WIDTH = 256
N_MEM = 256
N_HEADS = 4
MIX_WIDTH = A_WIDTH + B_WIDTH + M_WIDTH
REL_CLIP = 128
ROPE_THETA = 10000.0
LN_EPS = 1e-5
RMS_EPS = 1e-5
NEG_INF = -1e30
QK_SCALE = HEAD_DIM ** -0.5
LOG2E = math.log2(math.e)

_OFF = {}
_o = 0
for _name, _w in (("aq", A_WIDTH), ("ak", A_WIDTH), ("av", A_WIDTH), ("bq", B_WIDTH), ("bk", B_WIDTH),
                  ("bv", B_WIDTH), ("mq", M_WIDTH), ("gate", MIX_WIDTH)):
    _OFF[_name] = (_o, _o + _w)
    _o += _w
PROJ_TOTAL = _o

LANES = 128
ROW_TILE = 512
KV_TILE = ROW_TILE // 2
ONES_ROWS = 16
MERGE_TILE = 256
VMEM_LIMIT = 56 * 1024 * 1024


def _cparams(sem):
    return pltpu.CompilerParams(dimension_semantics=sem, vmem_limit_bytes=VMEM_LIMIT)


def _rope_slab(x, cos, sin_signed, lo_half):
    left = pltpu.roll(x, LANES - 32, 1)
    right = pltpu.roll(x, 32, 1)
    swapped = jnp.where(lo_half, left, right)
    return x * cos + swapped * sin_signed


def _proj_kernel(x_ref, w_ref, cos_ref, sin_ref, *out_refs, prompt):
    if prompt:
        (aq_ref, akb_ref, avb_ref, akf_ref, avf_ref, bqT_ref, bkb_ref, bvT_ref,
         bkf_ref, bvf_ref, mq_ref, g_ref) = out_refs
    else:
        aq_ref, akf_ref, avf_ref, bq_ref, bkf_ref, bvf_ref, mq_ref, g_ref = out_refs
    xb = x_ref[...].astype(BF16)

    def seg(name):
        lo, hi = _OFF[name]
        return jnp.dot(xb, w_ref[:, lo:hi], preferred_element_type=F32)

    aq_ref[...] = (seg("aq") * QK_SCALE).astype(BF16)
    ak = seg("ak")
    av = seg("av")
    akf_ref[...] = ak
    avf_ref[...] = av
    if prompt:
        akb_ref[...] = ak.astype(BF16)
        avb_ref[...] = av.astype(BF16)

    cos = cos_ref[...]
    sin = sin_ref[...]
    lane = lax.broadcasted_iota(jnp.int32, cos.shape, 1)
    lo_half = (lane % HEAD_DIM) < (HEAD_DIM // 2)
    bq = seg("bq")
    bk = seg("bk")
    bv = seg("bv")
    n_rows = bv.shape[0]
    for c in range(B_WIDTH // LANES):
        sl = slice(c * LANES, (c + 1) * LANES)
        q_c = _rope_slab(bq[:, sl], cos, sin, lo_half)
        k_c = _rope_slab(bk[:, sl], cos, sin, lo_half)
        bvf_ref[pl.ds(c, n_rows, stride=N_HEADS), :] = bv[:, sl]
        bkf_ref[pl.ds(2 * c, n_rows, stride=2 * N_HEADS), :] = k_c[:, :HEAD_DIM]
        bkf_ref[pl.ds(2 * c + 1, n_rows, stride=2 * N_HEADS), :] = k_c[:, HEAD_DIM:]
        if prompt:
            bkb_ref[:, sl] = k_c.astype(BF16)
            bqT_ref[0, sl, :] = (q_c * (QK_SCALE * LOG2E)).T.astype(BF16)
            vT_c = bv[:, sl].T.astype(BF16)
            for half in range(ROW_TILE // KV_TILE):
                bvT_ref[half, sl, :] = vT_c[:, half * KV_TILE:(half + 1) * KV_TILE]
        else:
            bq_ref[:, sl] = (q_c * QK_SCALE).astype(BF16)

    mq_ref[...] = (seg("mq") * QK_SCALE).astype(BF16)
    gate = seg("gate")
    g_ref[...] = (gate / (1.0 + jnp.exp(-gate))).astype(BF16)


def _project(x2, w_in_b, cos, sin, *, prompt, rows_per_seq):
    rows = x2.shape[0]
    tm = ROW_TILE
    n_tiles = rows // tm
    pos_tiles = cos.shape[0] // tm
    row_spec = lambda w: pl.BlockSpec((tm, w), lambda i: (i, 0))
    bk_rows, bv_rows = 2 * N_HEADS, N_HEADS
    bkf_shape = jax.ShapeDtypeStruct((rows * bk_rows, HEAD_DIM), F32)
    bvf_shape = jax.ShapeDtypeStruct((rows * bv_rows, B_VDIM), F32)
    bkf_spec = pl.BlockSpec((tm * bk_rows, HEAD_DIM), lambda i: (i, 0))
    bvf_spec = pl.BlockSpec((tm * bv_rows, B_VDIM), lambda i: (i, 0))
    in_specs = [
        row_spec(D_MODEL),
        pl.BlockSpec((D_MODEL, PROJ_TOTAL), lambda i: (0, 0)),
        pl.BlockSpec((tm, LANES), lambda i: (i % pos_tiles, 0)),
        pl.BlockSpec((tm, LANES), lambda i: (i % pos_tiles, 0)),
    ]
    sds = jax.ShapeDtypeStruct
    if prompt:
        tiles_per_seq = rows_per_seq // tm
        n_seq = rows // rows_per_seq
        tail_spec = pl.BlockSpec((tm, A_WIDTH), lambda i: (i // tiles_per_seq, 0))
        t_spec = pl.BlockSpec((1, B_WIDTH, tm), lambda i: (i, 0, 0))
        kv_per_row_tile = tm // KV_TILE
        vt_spec = pl.BlockSpec((kv_per_row_tile, B_WIDTH, KV_TILE), lambda i: (i, 0, 0))
        out_shape = (
            sds((rows, A_WIDTH), BF16), sds((rows, A_WIDTH), BF16), sds((rows, A_WIDTH), BF16),
            sds((n_seq * tm, A_WIDTH), F32), sds((n_seq * tm, A_WIDTH), F32),
            sds((n_tiles, B_WIDTH, tm), BF16), sds((rows, B_WIDTH), BF16),
            sds((n_tiles * kv_per_row_tile, B_WIDTH, KV_TILE), BF16),
            bkf_shape, bvf_shape,
            sds((rows, M_WIDTH), BF16), sds((rows, MIX_WIDTH), BF16),
        )
        out_specs = (
            row_spec(A_WIDTH), row_spec(A_WIDTH), row_spec(A_WIDTH), tail_spec, tail_spec,
            t_spec, row_spec(B_WIDTH), vt_spec, bkf_spec, bvf_spec,
            row_spec(M_WIDTH), row_spec(MIX_WIDTH),
        )
    else:
        out_shape = (
            sds((rows, A_WIDTH), BF16), sds((rows, A_WIDTH), F32), sds((rows, A_WIDTH), F32),
            sds((rows, B_WIDTH), BF16), bkf_shape, bvf_shape,
            sds((rows, M_WIDTH), BF16), sds((rows, MIX_WIDTH), BF16),
        )
        out_specs = (
            row_spec(A_WIDTH), row_spec(A_WIDTH), row_spec(A_WIDTH),
            row_spec(B_WIDTH), bkf_spec, bvf_spec,
            row_spec(M_WIDTH), row_spec(MIX_WIDTH),
        )
    return pl.pallas_call(
        functools.partial(_proj_kernel, prompt=prompt),
        out_shape=out_shape,
        grid=(n_tiles,),
        in_specs=in_specs,
        out_specs=out_specs,
        compiler_params=_cparams(("arbitrary",)),
        name="proj_prompt" if prompt else "proj_sample",
    )(x2, w_in_b, cos, sin)


def _mem_kv_kernel(mem_ref, w_ref, mk_ref, mv_ref):
    kv = jnp.dot(mem_ref[...].astype(BF16), w_ref[...], preferred_element_type=F32)
    mk_ref[...] = kv[:, :M_WIDTH]
    mv_ref[...] = kv[:, M_WIDTH:]


def _mem_kv(mem2, w_b):
    rows = mem2.shape[0]
    tm = N_MEM
    return pl.pallas_call(
        _mem_kv_kernel,
        out_shape=(jax.ShapeDtypeStruct((rows, M_WIDTH), F32), jax.ShapeDtypeStruct((rows, M_WIDTH), F32)),
        grid=(rows // tm,),
        in_specs=[pl.BlockSpec((tm, D_MODEL), lambda i: (i, 0)),
                  pl.BlockSpec((D_MODEL, 2 * M_WIDTH), lambda i: (0, 0))],
        out_specs=(pl.BlockSpec((tm, M_WIDTH), lambda i: (i, 0)), pl.BlockSpec((tm, M_WIDTH), lambda i: (i, 0))),
        compiler_params=_cparams(("arbitrary",)),
        name="mem_kv",
    )(mem2, w_b)


def _lambda_value(lp, lambda_init):
    a = jnp.sum(lp[0:1, :] * lp[1:2, :], axis=1, keepdims=True)
    b = jnp.sum(lp[2:3, :] * lp[3:4, :], axis=1, keepdims=True)
    return jnp.exp(a) - jnp.exp(b) + lambda_init


def _diff_post(o, g, lambda_init):
    ms = jnp.mean(o * o, axis=-1, keepdims=True)
    return o * lax.rsqrt(ms + RMS_EPS) * g * (1.0 - lambda_init)


def _diff_attn_kernel(qT_ref, k_ref, vT_ref, lam_ref, g_ref, o_ref,
                      s0_ref, s1_ref, p0_ref, p1_ref, a0_ref, a1_ref, m_ref, acc_ref, *, tq, tk, lambda_init):
    t = pl.program_id(2)
    qT = qT_ref[0].astype(F32)
    row = lax.broadcasted_iota(jnp.int32, qT.shape, 0)
    q_maps = (jnp.where(row < HEAD_DIM, qT, 0.0).astype(BF16),
              jnp.where(row >= HEAD_DIM, qT, 0.0).astype(BF16))
    acc_ref[...] = jnp.zeros_like(acc_ref)
    m_ref[...] = jnp.full(m_ref.shape, NEG_INF, F32)
    s_refs = (s0_ref, s1_ref)
    p_refs = (p0_ref, p1_ref)
    a_refs = (a0_ref, a1_ref)
    ones_rows = jnp.ones((ONES_ROWS, tk), BF16)

    def qk(j, buf):
        k = k_ref[pl.ds(pl.multiple_of(j * tk, tk), tk), :]
        for mp in range(2):
            s_refs[buf][mp] = jnp.dot(k, q_maps[mp], preferred_element_type=F32)

    def softmax(buf, visible):
        for mp in range(2):
            s = s_refs[buf][mp]
            if visible is not None:
                s = jnp.where(visible, s, NEG_INF)
            m_new = jnp.maximum(m_ref[mp], jnp.max(s, axis=0, keepdims=True))
            a_refs[buf][mp] = jnp.exp2(m_ref[mp] - m_new)
            m_ref[mp] = m_new
            p_refs[buf][mp] = jnp.exp2(s - m_new).astype(BF16)

    def pv(j, buf):
        vT = jnp.concatenate([vT_ref[j], ones_rows], axis=0)
        for mp in range(2):
            upd = jnp.dot(vT, p_refs[buf][mp], preferred_element_type=F32)
            acc_ref[mp] = a_refs[buf][mp] * acc_ref[mp] + upd

    def diag_visible(half):
        kc = lax.broadcasted_iota(jnp.int32, (tk, tq), 0) // CHUNK + half * (tk // CHUNK)
        qc = lax.broadcasted_iota(jnp.int32, (tk, tq), 1) // CHUNK
        return kc <= qc

    def tick_pair(j, masked):
        pv(j - 2, 0)
        qk(j, 0)
        softmax(1, None)
        pv(j - 1, 1)
        qk(j + 1, 1)
        softmax(0, diag_visible(0) if masked else None)

    qk(0, 0)
    qk(1, 1)
    softmax(0, diag_visible(0) | (t > 0))

    def body(i, c):
        tick_pair(2 + 2 * i, False)
        return c

    lax.fori_loop(0, t - 1, body, 0)

    @pl.when(t > 0)
    def _():
        tick_pair(2 * t, True)

    pv(2 * t, 0)
    softmax(1, diag_visible(1))
    pv(2 * t + 1, 1)

    lam = _lambda_value(lam_ref[...], lambda_init)
    inv0 = 1.0 / acc_ref[0, B_VDIM:B_VDIM + 1]
    inv1 = 1.0 / acc_ref[1, B_VDIM:B_VDIM + 1]
    oT = acc_ref[0, :B_VDIM] * inv0 - lam * (acc_ref[1, :B_VDIM] * inv1)
    o_ref[...] = _diff_post(oT.T, g_ref[...], lambda_init)


def _diff_attention(bqT, bkb, bvT, lam_p, subln_g, *, n_seq, seq, lambda_init):
    tq, tk = ROW_TILE, KV_TILE
    nq = seq // tq
    nk = seq // tk
    rows = n_seq * seq
    bvT4 = bvT.reshape(n_seq, nk, B_WIDTH, tk)
    return pl.pallas_call(
        functools.partial(_diff_attn_kernel, tq=tq, tk=tk, lambda_init=lambda_init),
        out_shape=jax.ShapeDtypeStruct((rows, B_WIDTH), F32),
        grid=(n_seq, N_HEADS, nq),
        in_specs=[
            pl.BlockSpec((1, B_VDIM, tq), lambda b, h, t: (b * nq + t, h, 0)),
            pl.BlockSpec((seq, B_VDIM), lambda b, h, t: (b, h)),
            pl.BlockSpec((None, nk, B_VDIM, tk), lambda b, h, t: (b, 0, h, 0)),
            pl.BlockSpec((4, HEAD_DIM), lambda b, h, t: (0, 0)),
            pl.BlockSpec((1, B_VDIM), lambda b, h, t: (0, 0)),
        ],
        out_specs=pl.BlockSpec((tq, B_VDIM), lambda b, h, t: (b * nq + t, h)),
        scratch_shapes=[
            pltpu.VMEM((2, tk, tq), F32), pltpu.VMEM((2, tk, tq), F32),
            pltpu.VMEM((2, tk, tq), BF16), pltpu.VMEM((2, tk, tq), BF16),
            pltpu.VMEM((2, 1, tq), F32), pltpu.VMEM((2, 1, tq), F32),
            pltpu.VMEM((2, 1, tq), F32),
            pltpu.VMEM((2, B_VDIM + ONES_ROWS, tq), F32),
        ],
        compiler_params=_cparams(("arbitrary", "arbitrary", "arbitrary")),
        name="diff_attn",
    )(bqT, bkb, bvT4, lam_p, subln_g)


def _head_masks(width):
    lane = lax.broadcasted_iota(jnp.int32, (1, width), 1)
    return [(lane >= h * HEAD_DIM) & (lane < (h + 1) * HEAD_DIM) for h in range(N_HEADS)]


def _dot_nt(a, b):
    return lax.dot_general(a, b, (((1,), (1,)), ((), ())), preferred_element_type=F32)


def _softmax_parts(blocks):
    m = functools.reduce(jnp.maximum, [jnp.max(s, axis=-1, keepdims=True) for s in blocks])
    ps = [jnp.exp(s - m) for s in blocks]
    l = functools.reduce(jnp.add, [jnp.sum(p, axis=-1, keepdims=True) for p in ps])
    return ps, l


def _attend(q_masked, keys, vals, biases):
    blocks = []
    for kk, bb in zip(keys, biases):
        s = _dot_nt(q_masked, kk)
        blocks.append(s if bb is None else s + bb)
    return _softmax_pv(blocks, vals)


def _softmax_pv(blocks, vals):
    ps, l = _softmax_parts(blocks)
    o = functools.reduce(jnp.add, [jnp.dot(p.astype(BF16), vv, preferred_element_type=F32)
                                   for p, vv in zip(ps, vals)])
    return o * (1.0 / l)


def _merge_out(x, mixed, g, w_out_ref, lng, lnb, alpha):
    mg = (mixed * g.astype(F32)).astype(BF16)
    y = jnp.dot(mg, w_out_ref[...], preferred_element_type=F32)
    z = alpha * x + y
    mu = jnp.mean(z, axis=-1, keepdims=True)
    zc = z - mu
    var = jnp.mean(zc * zc, axis=-1, keepdims=True)
    return zc * lax.rsqrt(var + LN_EPS) * lng + lnb


def _merge_kernel(x_ref, aq_ref, k0_ref, k1_ref, k2_ref, v0_ref, v1_ref, v2_ref, mq_ref, mk_ref, mv_ref,
                  ob_ref, g_ref, bias_ref, wout_ref, lng_ref, lnb_ref, y_ref, *, tiles_per_seq, alpha):
    t = pl.program_id(0) % tiles_per_seq
    masks = _head_masks(A_WIDTH)
    aq = aq_ref[...].astype(F32)
    mq = mq_ref[...].astype(F32)
    keys = [k0_ref[...], k1_ref[...], k2_ref[...]]
    vals = [v0_ref[...], v1_ref[...], v2_ref[...]]
    mk = mk_ref[...].astype(BF16)
    mv = mv_ref[...].astype(BF16)
    pen0 = jnp.where(t >= 2, 0.0, NEG_INF).astype(F32)
    pen1 = jnp.where(t >= 1, 0.0, NEG_INF).astype(F32)
    tile = MERGE_TILE
    o_a = jnp.zeros((tile, A_WIDTH), F32)
    o_m = jnp.zeros((tile, M_WIDTH), F32)
    pens = (pen0, pen1, None)
    a_scores, m_scores = [], []
    for h in range(N_HEADS):
        qh = jnp.where(masks[h], aq, 0.0).astype(BF16)
        blocks = []
        for b in range(3):
            s = _dot_nt(qh, keys[b]) + bias_ref[h, :, b * tile:(b + 1) * tile]
            blocks.append(s if pens[b] is None else s + pens[b])
        a_scores.append(blocks)
        qmh = jnp.where(masks[h], mq, 0.0).astype(BF16)
        m_scores.append([_dot_nt(qmh, mk)])
    for h in range(N_HEADS):
        o_a = jnp.where(masks[h], _softmax_pv(a_scores[h], vals), o_a)
        o_m = jnp.where(masks[h], _softmax_pv(m_scores[h], [mv]), o_m)
    mixed = jnp.concatenate([o_a, ob_ref[...], o_m], axis=1)
    y_ref[...] = _merge_out(x_ref[...], mixed, g_ref[...], wout_ref, lng_ref[...], lnb_ref[...], alpha)


def _merge_prompt(x2, aq, akb, avb, mq, mk, mv, ob, g, bias, w_out_b, lng, lnb, *, seq, alpha):
    rows = x2.shape[0]
    tile = MERGE_TILE
    tps = seq // tile
    n_tiles = rows // tile

    def prev(d):
        return lambda i: ((i // tps) * tps + jnp.maximum(i % tps - d, 0), 0)

    row_spec = lambda w: pl.BlockSpec((tile, w), lambda i: (i, 0))
    kv_specs = [pl.BlockSpec((tile, A_WIDTH), prev(d)) for d in (2, 1, 0)]
    const = lambda shape: pl.BlockSpec(shape, lambda i: (0,) * len(shape))
    return pl.pallas_call(
        functools.partial(_merge_kernel, tiles_per_seq=tps, alpha=alpha),
        out_shape=jax.ShapeDtypeStruct((rows, D_MODEL), F32),
        grid=(n_tiles,),
        in_specs=[row_spec(D_MODEL), row_spec(A_WIDTH)] + kv_specs + kv_specs + [
            row_spec(M_WIDTH),
            pl.BlockSpec((N_MEM, M_WIDTH), lambda i: (i // tps, 0)),
            pl.BlockSpec((N_MEM, M_WIDTH), lambda i: (i // tps, 0)),
            row_spec(B_WIDTH), row_spec(MIX_WIDTH),
            const((N_HEADS, tile, 3 * tile)), const((MIX_WIDTH, D_MODEL)), const((1, D_MODEL)), const((1, D_MODEL)),
        ],
        out_specs=row_spec(D_MODEL),
        compiler_params=_cparams(("arbitrary",)),
        name="merge_prompt",
    )(x2, aq, akb, akb, akb, avb, avb, avb, mq, mk, mv, ob, g, bias, w_out_b, lng, lnb)


def _sample_kernel(x_ref, sq_ref, sk_ref, sv_ref, tq_ref, tk_ref, tv_ref, nq_ref, g_ref,
                   cak_ref, cav_ref, cbk_ref, cbv_ref, cmk_ref, cmv_ref,
                   biasc_ref, biasn_ref, lam_ref, subg_ref, wout_ref, lng_ref, lnb_ref, y_ref,
                   *, lambda_init, alpha):
    masks = _head_masks(A_WIDTH)
    sq = sq_ref[...].astype(F32)
    nq = nq_ref[...].astype(F32)
    a_keys = [cak_ref[0].astype(BF16), sk_ref[...].astype(BF16)]
    a_vals = [cav_ref[0].astype(BF16), sv_ref[...].astype(BF16)]
    mk = cmk_ref[0].astype(BF16)
    mv = cmv_ref[0].astype(BF16)
    rows = sq.shape[0]
    o_a = jnp.zeros((rows, A_WIDTH), F32)
    o_m = jnp.zeros((rows, M_WIDTH), F32)
    for h in range(N_HEADS):
        qh = jnp.where(masks[h], sq, 0.0).astype(BF16)
        o_a = jnp.where(masks[h], _attend(qh, a_keys, a_vals, [biasc_ref[h], biasn_ref[h]]), o_a)
        qmh = jnp.where(masks[h], nq, 0.0).astype(BF16)
        o_m = jnp.where(masks[h], _attend(qmh, [mk], [mv], [None]), o_m)

    lam = _lambda_value(lam_ref[...], lambda_init)
    subg = subg_ref[...]
    past_b = cbv_ref.shape[1] // N_HEADS
    o_b = []
    for h in range(N_HEADS):
        q = tq_ref[:, h * B_VDIM:(h + 1) * B_VDIM].astype(F32)
        q_maps = [q[:, :HEAD_DIM], pltpu.roll(q, HEAD_DIM, 1)[:, :HEAD_DIM]]
        vals = [cbv_ref[0, pl.ds(h, past_b, stride=N_HEADS), :].astype(BF16),
                tv_ref[pl.ds(h, rows, stride=N_HEADS), :].astype(BF16)]
        o_maps = []
        for mp in range(2):
            hm = 2 * h + mp
            keys = [cbk_ref[0, pl.ds(hm, past_b, stride=2 * N_HEADS), :].astype(BF16),
                    tk_ref[pl.ds(hm, rows, stride=2 * N_HEADS), :].astype(BF16)]
            o_maps.append(_attend(q_maps[mp].astype(BF16), keys, vals, [None, None]))
        o_b.append(_diff_post(o_maps[0] - lam * o_maps[1], subg, lambda_init))
    mixed = jnp.concatenate([o_a] + o_b + [o_m], axis=1)
    y_ref[...] = _merge_out(x_ref[...], mixed, g_ref[...], wout_ref, lng_ref[...], lnb_ref[...], alpha)


def _sample_step(xs2, sq, sk, sv, tq, tk, tv, nq, g, cak, cav, cbk, cbv, cmk, cmv, biasc, biasn,
                 lam_p, subln_g, w_out_b, lng, lnb, *, n_streams, t_new, lambda_init, alpha):
    past_a = cak.shape[1]
    bk_rows, bv_rows = 2 * N_HEADS, N_HEADS
    past_b = cbv.shape[1] // bv_rows
    row_spec = lambda w: pl.BlockSpec((t_new, w), lambda n: (n, 0))
    nrow_spec = lambda r, w: pl.BlockSpec((t_new * r, w), lambda n: (n, 0))
    cache_spec = lambda r, w: pl.BlockSpec((1, r, w), lambda n: (n, 0, 0))
    const = lambda shape: pl.BlockSpec(shape, lambda n: (0,) * len(shape))
    return pl.pallas_call(
        functools.partial(_sample_kernel, lambda_init=lambda_init, alpha=alpha),
        out_shape=jax.ShapeDtypeStruct((n_streams * t_new, D_MODEL), F32),
        grid=(n_streams,),
        in_specs=[
            row_spec(D_MODEL), row_spec(A_WIDTH), row_spec(A_WIDTH), row_spec(A_WIDTH),
            row_spec(B_WIDTH), nrow_spec(bk_rows, HEAD_DIM), nrow_spec(bv_rows, B_VDIM),
            row_spec(M_WIDTH), row_spec(MIX_WIDTH),
            cache_spec(past_a, A_WIDTH), cache_spec(past_a, A_WIDTH),
            cache_spec(past_b * bk_rows, HEAD_DIM), cache_spec(past_b * bv_rows, B_VDIM),
            cache_spec(N_MEM, M_WIDTH), cache_spec(N_MEM, M_WIDTH),
            const((N_HEADS, t_new, past_a)), const((N_HEADS, t_new, t_new)),
            const((4, HEAD_DIM)), const((1, B_VDIM)),
            const((MIX_WIDTH, D_MODEL)), const((1, D_MODEL)), const((1, D_MODEL)),
        ],
        out_specs=row_spec(D_MODEL),
        compiler_params=_cparams(("arbitrary",)),
        name="sample_step",
    )(xs2, sq, sk, sv, tq, tk, tv, nq, g, cak, cav, cbk, cbv, cmk, cmv, biasc, biasn,
      lam_p, subln_g, w_out_b, lng, lnb)


def _rope_tables(pos):
    half = HEAD_DIM // 2
    inv = ROPE_THETA ** (-jnp.arange(half, dtype=F32) / half)
    ang = pos.astype(F32)[:, None] * inv[None, :]
    cos = jnp.cos(ang)
    sin = jnp.sin(ang)
    return jnp.tile(cos, (1, LANES // half)), jnp.concatenate([-sin, sin, -sin, sin], axis=-1)


def _rel_bias(table, dist):
    return table[:, jnp.clip(dist, -REL_CLIP, REL_CLIP) + REL_CLIP].astype(F32)


def _band_bias_prompt(table):
    n, width, period = MERGE_TILE, 3 * MERGE_TILE, 4 * MERGE_TILE
    u = jnp.arange(period)
    dist = jnp.where(u < width, BAND_ROWS - u, BAND_ROWS + period - u)
    diagonals = _rel_bias(table, dist)
    flat = jnp.tile(diagonals, (1, n))[:, :n * (period - 1)]
    bias = flat.reshape(N_HEADS, n, period - 1)[:, :, :width]
    i = jnp.arange(n)[:, None]
    j = jnp.arange(width)[None, :]
    qc = i // CHUNK
    kc = j // CHUNK
    visible = (kc >= qc) & (kc <= qc + N_BAND_CHUNKS)
    return jnp.where(visible[None], bias, NEG_INF)


def kernel(x_prompt, x_sample, cache_a_k, cache_a_v, cache_b_k, cache_b_v, cache_mem_k, cache_mem_v, mem_prompt,
           w_in, w_mem_kv, a_rel_bias, diff_lambda, diff_subln_g, w_out, ln_g, ln_b):
    depth = w_in.shape[0]
    assert depth == 1, "single-layer step only"
    n_seq, seq, _ = x_prompt.shape
    n_streams, t_new, _ = x_sample.shape
    past_a = cache_a_k.shape[2]
    past_b = cache_b_k.shape[2]
    assert seq % ROW_TILE == 0 and BAND_ROWS == ROW_TILE == 2 * MERGE_TILE and past_a == BAND_ROWS
    assert n_streams * t_new == ROW_TILE
    layer = 0
    lambda_init = 0.8 - 0.6 * math.exp(-0.3 * layer)
    alpha = (2.0 * depth) ** 0.25

    w_in_b = w_in[layer].astype(BF16)
    w_mem_b = w_mem_kv[layer].astype(BF16)
    w_out_b = w_out[layer].astype(BF16)
    table = a_rel_bias[layer]
    lam_p = diff_lambda[layer]
    subln_g = diff_subln_g[layer].reshape(1, B_VDIM)
    lng = ln_g[layer].reshape(1, D_MODEL)
    lnb = ln_b[layer].reshape(1, D_MODEL)

    rows = n_seq * seq
    x2 = x_prompt.reshape(rows, D_MODEL)
    cos_p, sin_p = _rope_tables(jnp.arange(seq))
    (aq, akb, avb, akf, avf, bqT, bkb, bvT, bkf, bvf, mq, g) = _project(
        x2, w_in_b, cos_p, sin_p, prompt=True, rows_per_seq=seq)
    mk, mv = _mem_kv(mem_prompt.reshape(n_seq * N_MEM, D_MODEL), w_mem_b)
    ob = _diff_attention(bqT, bkb, bvT, lam_p, subln_g, n_seq=n_seq, seq=seq, lambda_init=lambda_init)
    y_p = _merge_prompt(x2, aq, akb, avb, mq, mk, mv, ob, g, _band_bias_prompt(table), w_out_b, lng, lnb,
                        seq=seq, alpha=alpha)

    xs2 = x_sample.reshape(n_streams * t_new, D_MODEL)
    pos_s = past_b + jnp.arange(t_new)
    cos_s, sin_s = _rope_tables(jnp.tile(pos_s, n_streams))
    sq, sk, sv, tq, tk, tv, nq, sg = _project(xs2, w_in_b, cos_s, sin_s, prompt=False, rows_per_seq=t_new)
    ti = jnp.arange(t_new)[:, None]
    biasc = _rel_bias(table, ti + past_a - jnp.arange(past_a)[None, :])
    biasn = _rel_bias(table, ti - jnp.arange(t_new)[None, :])
    y_s = _sample_step(
        xs2, sq, sk, sv, tq, tk, tv, nq, sg,
        cache_a_k[layer].reshape(n_streams, past_a, A_WIDTH), cache_a_v[layer].reshape(n_streams, past_a, A_WIDTH),
        cache_b_k[layer].reshape(n_streams, past_b * 2 * N_HEADS, HEAD_DIM),
        cache_b_v[layer].reshape(n_streams, past_b * N_HEADS, B_VDIM),
        cache_mem_k[layer].reshape(n_streams, N_MEM, M_WIDTH), cache_mem_v[layer].reshape(n_streams, N_MEM, M_WIDTH),
        biasc, biasn, lam_p, subln_g, w_out_b, lng, lnb,
        n_streams=n_streams, t_new=t_new, lambda_init=lambda_init, alpha=alpha)

    hd = (N_HEADS, HEAD_DIM)
    return (
        y_p.reshape(n_seq, seq, D_MODEL),
        y_s.reshape(n_streams, t_new, D_MODEL),
        akf.reshape(1, n_seq, BAND_ROWS, *hd),
        avf.reshape(1, n_seq, BAND_ROWS, *hd),
        bkf.reshape(1, n_seq, seq, N_HEADS, 2, HEAD_DIM),
        bvf.reshape(1, n_seq, seq, N_HEADS, B_VDIM),
        mk.reshape(1, n_seq, N_MEM, *hd),
        mv.reshape(1, n_seq, N_MEM, *hd),
        sk.reshape(1, n_streams, t_new, *hd),
        sv.reshape(1, n_streams, t_new, *hd),
        tk.reshape(1, n_streams, t_new, N_HEADS, 2, HEAD_DIM),
        tv.reshape(1, n_streams, t_new, N_HEADS, B_VDIM),
    )
```

```python
import functools
import math

import jax
import jax.numpy as jnp
from jax import lax
from jax.experimental import pallas as pl
from jax.experimental.pallas import tpu as pltpu

F32 = jnp.float32
BF16 = jnp.bfloat16

D_MODEL = 1024
CHUNK = 64
N_BAND_CHUNKS = 8
BAND_ROWS = N_BAND_CHUNKS * CHUNK
HEAD_DIM = 64
A_WIDTH = 256
B_WIDTH = 512
B_VDIM = 128
M_WIDTH = 256
N_MEM = 256
N_HEADS = 4
MIX_WIDTH = A_WIDTH + B_WIDTH + M_WIDTH
REL_CLIP = 128
ROPE_THETA = 10000.0
LN_EPS = 1e-5
RMS_EPS = 1e-5
NEG_INF = -1e30
QK_SCALE = HEAD_DIM ** -0.5
LOG2E = math.log2(math.e)

_OFF = {}
_o = 0
for _name, _w in (("aq", A_WIDTH), ("ak", A_WIDTH), ("av", A_WIDTH), ("bq", B_WIDTH), ("bk", B_WIDTH),
                  ("bv", B_WIDTH), ("mq", M_WIDTH), ("gate", MIX_WIDTH)):
    _OFF[_name] = (_o, _o + _w)
    _o += _w
PROJ_TOTAL = _o

LANES = 128
ROW_TILE = 512
KV_TILE = ROW_TILE // 2
ONES_ROWS = 16
MERGE_TILE = 256
VMEM_LIMIT = 56 * 1024 * 1024


def _cparams(sem):
    return pltpu.CompilerParams(dimension_semantics=sem, vmem_limit_bytes=VMEM_LIMIT)


def _rope_slab(x, cos, sin_signed, lo_half):
    left = pltpu.roll(x, LANES - 32, 1)
    right = pltpu.roll(x, 32, 1)
    swapped = jnp.where(lo_half, left, right)
    return x * cos + swapped * sin_signed


def _proj_kernel(x_ref, w_ref, cos_ref, sin_ref, *out_refs, prompt):
    if prompt:
        (aq_ref, akb_ref, avb_ref, akf_ref, avf_ref, bqT_ref, bkb_ref, bvT_ref,
         bkf_ref, bvf_ref, mq_ref, g_ref) = out_refs
    else:
        aq_ref, akf_ref, avf_ref, bq_ref, bkf_ref, bvf_ref, mq_ref, g_ref = out_refs
    xb = x_ref[...].astype(BF16)

    def seg(name):
        lo, hi = _OFF[name]
        return jnp.dot(xb, w_ref[:, lo:hi], preferred_element_type=F32)

    aq_ref[...] = (seg("aq") * QK_SCALE).astype(BF16)
    ak = seg("ak")
    av = seg("av")
    akf_ref[...] = ak
    avf_ref[...] = av
    if prompt:
        akb_ref[...] = ak.astype(BF16)
        avb_ref[...] = av.astype(BF16)

    cos = cos_ref[...]
    sin = sin_ref[...]
    lane = lax.broadcasted_iota(jnp.int32, cos.shape, 1)
    lo_half = (lane % HEAD_DIM) < (HEAD_DIM // 2)
    bq = seg("bq")
    bk = seg("bk")
    bv = seg("bv")
    n_rows = bv.shape[0]
    for c in range(B_WIDTH // LANES):
        sl = slice(c * LANES, (c + 1) * LANES)
        q_c = _rope_slab(bq[:, sl], cos, sin, lo_half)
        k_c = _rope_slab(bk[:, sl], cos, sin, lo_half)
        bvf_ref[pl.ds(c, n_rows, stride=N_HEADS), :] = bv[:, sl]
        if prompt:
            kT_c = k_c.T
            bkf_ref[0, c, 0] = kT_c[:HEAD_DIM]
            bkf_ref[0, c, 1] = kT_c[HEAD_DIM:]
            bkb_ref[:, sl] = k_c.astype(BF16)
            bqT_ref[0, sl, :] = (q_c * (QK_SCALE * LOG2E)).T.astype(BF16)
            vT_c = bv[:, sl].T.astype(BF16)
            for half in range(ROW_TILE // KV_TILE):
                bvT_ref[half, sl, :] = vT_c[:, half * KV_TILE:(half + 1) * KV_TILE]
        else:
            bkf_ref[pl.ds(2 * c, n_rows, stride=2 * N_HEADS), :] = k_c[:, :HEAD_DIM]
            bkf_ref[pl.ds(2 * c + 1, n_rows, stride=2 * N_HEADS), :] = k_c[:, HEAD_DIM:]
            bq_ref[:, sl] = (q_c * QK_SCALE).astype(BF16)

    mq_ref[...] = (seg("mq") * QK_SCALE).astype(BF16)
    gate = seg("gate")
    g_ref[...] = (gate / (1.0 + jnp.exp(-gate))).astype(BF16)


def _project(x2, w_in_b, cos, sin, *, prompt, rows_per_seq):
    rows = x2.shape[0]
    tm = ROW_TILE
    n_tiles = rows // tm
    pos_tiles = cos.shape[0] // tm
    row_spec = lambda w: pl.BlockSpec((tm, w), lambda i: (i, 0))
    bk_rows, bv_rows = 2 * N_HEADS, N_HEADS
    bkf_shape = jax.ShapeDtypeStruct((rows * bk_rows, HEAD_DIM), F32)
    bvf_shape = jax.ShapeDtypeStruct((rows * bv_rows, B_VDIM), F32)
    bkf_spec = pl.BlockSpec((tm * bk_rows, HEAD_DIM), lambda i: (i, 0))
    bvf_spec = pl.BlockSpec((tm * bv_rows, B_VDIM), lambda i: (i, 0))
    in_specs = [
        row_spec(D_MODEL),
        pl.BlockSpec((D_MODEL, PROJ_TOTAL), lambda i: (0, 0)),
        pl.BlockSpec((tm, LANES), lambda i: (i % pos_tiles, 0)),
        pl.BlockSpec((tm, LANES), lambda i: (i % pos_tiles, 0)),
    ]
    sds = jax.ShapeDtypeStruct
    if prompt:
        tiles_per_seq = rows_per_seq // tm
        n_seq = rows // rows_per_seq
        tail_spec = pl.BlockSpec((tm, A_WIDTH), lambda i: (i // tiles_per_seq, 0))
        t_spec = pl.BlockSpec((1, B_WIDTH, tm), lambda i: (i, 0, 0))
        kv_per_row_tile = tm // KV_TILE
        vt_spec = pl.BlockSpec((kv_per_row_tile, B_WIDTH, KV_TILE), lambda i: (i, 0, 0))
        bkf_shape = jax.ShapeDtypeStruct((n_seq, N_HEADS, 2, HEAD_DIM, rows_per_seq), F32)
        bkf_spec = pl.BlockSpec((1, N_HEADS, 2, HEAD_DIM, tm),
                                lambda i: (i // tiles_per_seq, 0, 0, 0, i % tiles_per_seq))
        out_shape = (
            sds((rows, A_WIDTH), BF16), sds((rows, A_WIDTH), BF16), sds((rows, A_WIDTH), BF16),
            sds((n_seq * tm, A_WIDTH), F32), sds((n_seq * tm, A_WIDTH), F32),
            sds((n_tiles, B_WIDTH, tm), BF16), sds((rows, B_WIDTH), BF16),
            sds((n_tiles * kv_per_row_tile, B_WIDTH, KV_TILE), BF16),
            bkf_shape, bvf_shape,
            sds((rows, M_WIDTH), BF16), sds((rows, MIX_WIDTH), BF16),
        )
        out_specs = (
            row_spec(A_WIDTH), row_spec(A_WIDTH), row_spec(A_WIDTH), tail_spec, tail_spec,
            t_spec, row_spec(B_WIDTH), vt_spec, bkf_spec, bvf_spec,
            row_spec(M_WIDTH), row_spec(MIX_WIDTH),
        )
    else:
        out_shape = (
            sds((rows, A_WIDTH), BF16), sds((rows, A_WIDTH), F32), sds((rows, A_WIDTH), F32),
            sds((rows, B_WIDTH), BF16), bkf_shape, bvf_shape,
            sds((rows, M_WIDTH), BF16), sds((rows, MIX_WIDTH), BF16),
        )
        out_specs = (
            row_spec(A_WIDTH), row_spec(A_WIDTH), row_spec(A_WIDTH),
            row_spec(B_WIDTH), bkf_spec, bvf_spec,
            row_spec(M_WIDTH), row_spec(MIX_WIDTH),
        )
    return pl.pallas_call(
        functools.partial(_proj_kernel, prompt=prompt),
        out_shape=out_shape,
        grid=(n_tiles,),
        in_specs=in_specs,
        out_specs=out_specs,
        compiler_params=_cparams(("arbitrary",)),
        name="proj_prompt" if prompt else "proj_sample",
    )(x2, w_in_b, cos, sin)


def _mem_kv_kernel(mem_ref, w_ref, mk_ref, mv_ref):
    kv = jnp.dot(mem_ref[...].astype(BF16), w_ref[...], preferred_element_type=F32)
    mk_ref[...] = kv[:, :M_WIDTH]
    mv_ref[...] = kv[:, M_WIDTH:]


def _mem_kv(mem2, w_b):
    rows = mem2.shape[0]
    tm = N_MEM
    return pl.pallas_call(
        _mem_kv_kernel,
        out_shape=(jax.ShapeDtypeStruct((rows, M_WIDTH), F32), jax.ShapeDtypeStruct((rows, M_WIDTH), F32)),
        grid=(rows // tm,),
        in_specs=[pl.BlockSpec((tm, D_MODEL), lambda i: (i, 0)),
                  pl.BlockSpec((D_MODEL, 2 * M_WIDTH), lambda i: (0, 0))],
        out_specs=(pl.BlockSpec((tm, M_WIDTH), lambda i: (i, 0)), pl.BlockSpec((tm, M_WIDTH), lambda i: (i, 0))),
        compiler_params=_cparams(("arbitrary",)),
        name="mem_kv",
    )(mem2, w_b)


def _lambda_value(lp, lambda_init):
    a = jnp.sum(lp[0:1, :] * lp[1:2, :], axis=1, keepdims=True)
    b = jnp.sum(lp[2:3, :] * lp[3:4, :], axis=1, keepdims=True)
    return jnp.exp(a) - jnp.exp(b) + lambda_init


def _diff_post(o, g, lambda_init):
    ms = jnp.mean(o * o, axis=-1, keepdims=True)
    return o * lax.rsqrt(ms + RMS_EPS) * g * (1.0 - lambda_init)


def _diff_attn_kernel(qT_ref, k_ref, vT_ref, lam_ref, g_ref, o_ref,
                      s0_ref, s1_ref, p0_ref, p1_ref, a0_ref, a1_ref, t0_ref, t1_ref, m_ref, acc_ref,
                      *, tq, tk, lambda_init):
    t = pl.program_id(2)
    qT = qT_ref[0].astype(F32)
    row = lax.broadcasted_iota(jnp.int32, qT.shape, 0)
    q_maps = (jnp.where(row < HEAD_DIM, qT, 0.0).astype(BF16),
              jnp.where(row >= HEAD_DIM, qT, 0.0).astype(BF16))
    acc_ref[...] = jnp.zeros_like(acc_ref)
    m_ref[...] = jnp.full(m_ref.shape, NEG_INF, F32)
    s_refs = (s0_ref, s1_ref)
    p_refs = (p0_ref, p1_ref)
    a_refs = (a0_ref, a1_ref)
    t_refs = (t0_ref, t1_ref)
    ones_rows = jnp.ones((ONES_ROWS, tk), BF16)

    def qk(j, buf, visible=None):
        k = k_ref[pl.ds(pl.multiple_of(j * tk, tk), tk), :]
        for mp in range(2):
            s = jnp.dot(k, q_maps[mp], preferred_element_type=F32)
            if visible is not None:
                s = jnp.where(visible, s, NEG_INF)
            s_refs[buf][mp] = s
            t_refs[buf][mp] = jnp.max(s, axis=0, keepdims=True)

    def softmax(buf):
        for mp in range(2):
            m_new = jnp.maximum(m_ref[mp], t_refs[buf][mp])
            a_refs[buf][mp] = jnp.exp2(m_ref[mp] - m_new)
            m_ref[mp] = m_new
            p_refs[buf][mp] = jnp.exp2(s_refs[buf][mp] - m_new).astype(BF16)

    def pv(j, buf):
        vT = jnp.concatenate([vT_ref[j], ones_rows], axis=0)
        for mp in range(2):
            upd = jnp.dot(vT, p_refs[buf][mp], preferred_element_type=F32)
            acc_ref[mp] = a_refs[buf][mp] * acc_ref[mp] + upd

    def diag_visible(half):
        kc = lax.broadcasted_iota(jnp.int32, (tk, tq), 0) // CHUNK + half * (tk // CHUNK)
        qc = lax.broadcasted_iota(jnp.int32, (tk, tq), 1) // CHUNK
        return kc <= qc

    def tick_pair(j, masked):
        pv(j - 2, 0)
        qk(j, 0, diag_visible(0) if masked else None)
        softmax(1)
        pv(j - 1, 1)
        qk(j + 1, 1, diag_visible(1) if masked else None)
        softmax(0)

    qk(0, 0, diag_visible(0) | (t > 0))
    qk(1, 1, diag_visible(1) | (t > 0))
    softmax(0)

    def body(i, c):
        tick_pair(2 + 2 * i, False)
        return c

    lax.fori_loop(0, t - 1, body, 0)

    @pl.when(t > 0)
    def _():
        tick_pair(2 * t, True)

    pv(2 * t, 0)
    softmax(1)
    pv(2 * t + 1, 1)

    lam = _lambda_value(lam_ref[...], lambda_init)
    inv0 = 1.0 / acc_ref[0, B_VDIM:B_VDIM + 1]
    inv1 = 1.0 / acc_ref[1, B_VDIM:B_VDIM + 1]
    oT = acc_ref[0, :B_VDIM] * inv0 - lam * (acc_ref[1, :B_VDIM] * inv1)
    o_ref[...] = _diff_post(oT.T, g_ref[...], lambda_init)


def _diff_attention(bqT, bkb, bvT, lam_p, subln_g, *, n_seq, seq, lambda_init):
    tq, tk = ROW_TILE, KV_TILE
    nq = seq // tq
    nk = seq // tk
    rows = n_seq * seq
    bvT4 = bvT.reshape(n_seq, nk, B_WIDTH, tk)
    return pl.pallas_call(
        functools.partial(_diff_attn_kernel, tq=tq, tk=tk, lambda_init=lambda_init),
        out_shape=jax.ShapeDtypeStruct((rows, B_WIDTH), F32),
        grid=(n_seq, N_HEADS, nq),
        in_specs=[
            pl.BlockSpec((1, B_VDIM, tq), lambda b, h, t: (b * nq + t, h, 0)),
            pl.BlockSpec((seq, B_VDIM), lambda b, h, t: (b, h)),
            pl.BlockSpec((None, nk, B_VDIM, tk), lambda b, h, t: (b, 0, h, 0)),
            pl.BlockSpec((4, HEAD_DIM), lambda b, h, t: (0, 0)),
            pl.BlockSpec((1, B_VDIM), lambda b, h, t: (0, 0)),
        ],
        out_specs=pl.BlockSpec((tq, B_VDIM), lambda b, h, t: (b * nq + t, h)),
        scratch_shapes=[
            pltpu.VMEM((2, tk, tq), F32), pltpu.VMEM((2, tk, tq), F32),
            pltpu.VMEM((2, tk, tq), BF16), pltpu.VMEM((2, tk, tq), BF16),
            pltpu.VMEM((2, 1, tq), F32), pltpu.VMEM((2, 1, tq), F32),
            pltpu.VMEM((2, 1, tq), F32), pltpu.VMEM((2, 1, tq), F32),
            pltpu.VMEM((2, 1, tq), F32),
            pltpu.VMEM((2, B_VDIM + ONES_ROWS, tq), F32),
        ],
        compiler_params=_cparams(("arbitrary", "arbitrary", "arbitrary")),
        name="diff_attn",
    )(bqT, bkb, bvT4, lam_p, subln_g)


def _head_masks(width):
    lane = lax.broadcasted_iota(jnp.int32, (1, width), 1)
    return [(lane >= h * HEAD_DIM) & (lane < (h + 1) * HEAD_DIM) for h in range(N_HEADS)]


def _dot_nt(a, b):
    return lax.dot_general(a, b, (((1,), (1,)), ((), ())), preferred_element_type=F32)


def _softmax_parts(blocks):
    m = functools.reduce(jnp.maximum, [jnp.max(s, axis=-1, keepdims=True) for s in blocks])
    ps = [jnp.exp(s - m) for s in blocks]
    l = functools.reduce(jnp.add, [jnp.sum(p, axis=-1, keepdims=True) for p in ps])
    return ps, l


def _attend(q_masked, keys, vals, biases):
    blocks = []
    for kk, bb in zip(keys, biases):
        s = _dot_nt(q_masked, kk)
        blocks.append(s if bb is None else s + bb)
    return _softmax_pv(blocks, vals)


def _softmax_pv(blocks, vals):
    ps, l = _softmax_parts(blocks)
    o = functools.reduce(jnp.add, [jnp.dot(p.astype(BF16), vv, preferred_element_type=F32)
                                   for p, vv in zip(ps, vals)])
    return o * (1.0 / l)


def _merge_out(x, mixed, g, w_out_ref, lng, lnb, alpha):
    mg = (mixed * g.astype(F32)).astype(BF16)
    y = jnp.dot(mg, w_out_ref[...], preferred_element_type=F32)
    z = alpha * x + y
    mu = jnp.mean(z, axis=-1, keepdims=True)
    zc = z - mu
    var = jnp.mean(zc * zc, axis=-1, keepdims=True)
    return zc * lax.rsqrt(var + LN_EPS) * lng + lnb


def _merge_kernel(x_ref, aq_ref, k0_ref, k1_ref, k2_ref, v0_ref, v1_ref, v2_ref, mq_ref, mk_ref, mv_ref,
                  ob_ref, g_ref, bias_ref, wout_ref, lng_ref, lnb_ref, y_ref, *, tiles_per_seq, alpha):
    t = pl.program_id(0) % tiles_per_seq
    masks = _head_masks(A_WIDTH)
    aq = aq_ref[...].astype(F32)
    mq = mq_ref[...].astype(F32)
    keys = [k0_ref[...], k1_ref[...], k2_ref[...]]
    vals = [v0_ref[...], v1_ref[...], v2_ref[...]]
    mk = mk_ref[...].astype(BF16)
    mv = mv_ref[...].astype(BF16)
    pen0 = jnp.where(t >= 2, 0.0, NEG_INF).astype(F32)
    pen1 = jnp.where(t >= 1, 0.0, NEG_INF).astype(F32)
    tile = MERGE_TILE
    o_a = jnp.zeros((tile, A_WIDTH), F32)
    o_m = jnp.zeros((tile, M_WIDTH), F32)
    pens = (pen0, pen1, None)
    a_scores, m_scores = [], []
    for h in range(N_HEADS):
        qh = jnp.where(masks[h], aq, 0.0).astype(BF16)
        blocks = []
        for b in range(3):
            s = _dot_nt(qh, keys[b]) + bias_ref[h, :, b * tile:(b + 1) * tile]
            blocks.append(s if pens[b] is None else s + pens[b])
        a_scores.append(blocks)
        qmh = jnp.where(masks[h], mq, 0.0).astype(BF16)
        m_scores.append([_dot_nt(qmh, mk)])
    for h in range(N_HEADS):
        o_a = jnp.where(masks[h], _softmax_pv(a_scores[h], vals), o_a)
        o_m = jnp.where(masks[h], _softmax_pv(m_scores[h], [mv]), o_m)
    mixed = jnp.concatenate([o_a, ob_ref[...], o_m], axis=1)
    y_ref[...] = _merge_out(x_ref[...], mixed, g_ref[...], wout_ref, lng_ref[...], lnb_ref[...], alpha)


def _merge_prompt(x2, aq, akb, avb, mq, mk, mv, ob, g, bias, w_out_b, lng, lnb, *, seq, alpha):
    rows = x2.shape[0]
    tile = MERGE_TILE
    tps = seq // tile
    n_tiles = rows // tile

    def prev(d):
        return lambda i: ((i // tps) * tps + jnp.maximum(i % tps - d, 0), 0)

    row_spec = lambda w: pl.BlockSpec((tile, w), lambda i: (i, 0))
    kv_specs = [pl.BlockSpec((tile, A_WIDTH), prev(d)) for d in (2, 1, 0)]
    const = lambda shape: pl.BlockSpec(shape, lambda i: (0,) * len(shape))
    return pl.pallas_call(
        functools.partial(_merge_kernel, tiles_per_seq=tps, alpha=alpha),
        out_shape=jax.ShapeDtypeStruct((rows, D_MODEL), F32),
        grid=(n_tiles,),
        in_specs=[row_spec(D_MODEL), row_spec(A_WIDTH)] + kv_specs + kv_specs + [
            row_spec(M_WIDTH),
            pl.BlockSpec((N_MEM, M_WIDTH), lambda i: (i // tps, 0)),
            pl.BlockSpec((N_MEM, M_WIDTH), lambda i: (i // tps, 0)),
            row_spec(B_WIDTH), row_spec(MIX_WIDTH),
            const((N_HEADS, tile, 3 * tile)), const((MIX_WIDTH, D_MODEL)), const((1, D_MODEL)), const((1, D_MODEL)),
        ],
        out_specs=row_spec(D_MODEL),
        compiler_params=_cparams(("arbitrary",)),
        name="merge_prompt",
    )(x2, aq, akb, akb, akb, avb, avb, avb, mq, mk, mv, ob, g, bias, w_out_b, lng, lnb)


def _sample_kernel(x_ref, sq_ref, sk_ref, sv_ref, tq_ref, tk_ref, tv_ref, nq_ref, g_ref,
                   cak_ref, cav_ref, cbk_ref, cbv_ref, cmk_ref, cmv_ref,
                   biasc_ref, biasn_ref, lam_ref, subg_ref, wout_ref, lng_ref, lnb_ref, y_ref,
                   *, lambda_init, alpha):
    masks = _head_masks(A_WIDTH)
    sq = sq_ref[...].astype(F32)
    nq = nq_ref[...].astype(F32)
    a_keys = [cak_ref[0].astype(BF16), sk_ref[...].astype(BF16)]
    a_vals = [cav_ref[0].astype(BF16), sv_ref[...].astype(BF16)]
    mk = cmk_ref[0].astype(BF16)
    mv = cmv_ref[0].astype(BF16)
    rows = sq.shape[0]
    o_a = jnp.zeros((rows, A_WIDTH), F32)
    o_m = jnp.zeros((rows, M_WIDTH), F32)
    for h in range(N_HEADS):
        qh = jnp.where(masks[h], sq, 0.0).astype(BF16)
        o_a = jnp.where(masks[h], _attend(qh, a_keys, a_vals, [biasc_ref[h], biasn_ref[h]]), o_a)
        qmh = jnp.where(masks[h], nq, 0.0).astype(BF16)
        o_m = jnp.where(masks[h], _attend(qmh, [mk], [mv], [None]), o_m)

    lam = _lambda_value(lam_ref[...], lambda_init)
    subg = subg_ref[...]
    past_b = cbv_ref.shape[1] // N_HEADS
    o_b = []
    for h in range(N_HEADS):
        q = tq_ref[:, h * B_VDIM:(h + 1) * B_VDIM].astype(F32)
        q_maps = [q[:, :HEAD_DIM], pltpu.roll(q, HEAD_DIM, 1)[:, :HEAD_DIM]]
        vals = [cbv_ref[0, pl.ds(h, past_b, stride=N_HEADS), :].astype(BF16),
                tv_ref[pl.ds(h, rows, stride=N_HEADS), :].astype(BF16)]
        o_maps = []
        for mp in range(2):
            q_m = q_maps[mp].astype(BF16)
            k_new = tk_ref[pl.ds(2 * h + mp, rows, stride=2 * N_HEADS), :].astype(BF16)
            blocks = [jnp.dot(q_m, cbk_ref[0, h, mp].astype(BF16), preferred_element_type=F32),
                      _dot_nt(q_m, k_new)]
            o_maps.append(_softmax_pv(blocks, vals))
        o_b.append(_diff_post(o_maps[0] - lam * o_maps[1], subg, lambda_init))
    mixed = jnp.concatenate([o_a] + o_b + [o_m], axis=1)
    y_ref[...] = _merge_out(x_ref[...], mixed, g_ref[...], wout_ref, lng_ref[...], lnb_ref[...], alpha)


def _sample_step(xs2, sq, sk, sv, tq, tk, tv, nq, g, cak, cav, cbk, cbv, cmk, cmv, biasc, biasn,
                 lam_p, subln_g, w_out_b, lng, lnb, *, n_streams, t_new, lambda_init, alpha):
    past_a = cak.shape[1]
    bk_rows, bv_rows = 2 * N_HEADS, N_HEADS
    past_b = cbv.shape[1] // bv_rows
    row_spec = lambda w: pl.BlockSpec((t_new, w), lambda n: (n, 0))
    nrow_spec = lambda r, w: pl.BlockSpec((t_new * r, w), lambda n: (n, 0))
    cache_spec = lambda r, w: pl.BlockSpec((1, r, w), lambda n: (n, 0, 0))
    const = lambda shape: pl.BlockSpec(shape, lambda n: (0,) * len(shape))
    return pl.pallas_call(
        functools.partial(_sample_kernel, lambda_init=lambda_init, alpha=alpha),
        out_shape=jax.ShapeDtypeStruct((n_streams * t_new, D_MODEL), F32),
        grid=(n_streams,),
        in_specs=[
            row_spec(D_MODEL), row_spec(A_WIDTH), row_spec(A_WIDTH), row_spec(A_WIDTH),
            row_spec(B_WIDTH), nrow_spec(bk_rows, HEAD_DIM), nrow_spec(bv_rows, B_VDIM),
            row_spec(M_WIDTH), row_spec(MIX_WIDTH),
            cache_spec(past_a, A_WIDTH), cache_spec(past_a, A_WIDTH),
            pl.BlockSpec((1, N_HEADS, 2, HEAD_DIM, past_b), lambda n: (n, 0, 0, 0, 0)),
            cache_spec(past_b * bv_rows, B_VDIM),
            cache_spec(N_MEM, M_WIDTH), cache_spec(N_MEM, M_WIDTH),
            const((N_HEADS, t_new, past_a)), const((N_HEADS, t_new, t_new)),
            const((4, HEAD_DIM)), const((1, B_VDIM)),
            const((MIX_WIDTH, D_MODEL)), const((1, D_MODEL)), const((1, D_MODEL)),
        ],
        out_specs=row_spec(D_MODEL),
        compiler_params=_cparams(("arbitrary",)),
        name="sample_step",
    )(xs2, sq, sk, sv, tq, tk, tv, nq, g, cak, cav, cbk, cbv, cmk, cmv, biasc, biasn,
      lam_p, subln_g, w_out_b, lng, lnb)


def _rope_tables(pos):
    half = HEAD_DIM // 2
    inv = ROPE_THETA ** (-jnp.arange(half, dtype=F32) / half)
    ang = pos.astype(F32)[:, None] * inv[None, :]
    cos = jnp.cos(ang)
    sin = jnp.sin(ang)
    return jnp.tile(cos, (1, LANES // half)), jnp.concatenate([-sin, sin, -sin, sin], axis=-1)


def _rel_bias(table, dist):
    return table[:, jnp.clip(dist, -REL_CLIP, REL_CLIP) + REL_CLIP].astype(F32)


def _band_bias_prompt(table):
    n, width, period = MERGE_TILE, 3 * MERGE_TILE, 4 * MERGE_TILE
    u = jnp.arange(period)
    dist = jnp.where(u < width, BAND_ROWS - u, BAND_ROWS + period - u)
    diagonals = _rel_bias(table, dist)
    flat = jnp.tile(diagonals, (1, n))[:, :n * (period - 1)]
    bias = flat.reshape(N_HEADS, n, period - 1)[:, :, :width]
    i = jnp.arange(n)[:, None]
    j = jnp.arange(width)[None, :]
    qc = i // CHUNK
    kc = j // CHUNK
    visible = (kc >= qc) & (kc <= qc + N_BAND_CHUNKS)
    return jnp.where(visible[None], bias, NEG_INF)


def kernel(x_prompt, x_sample, cache_a_k, cache_a_v, cache_b_k, cache_b_v, cache_mem_k, cache_mem_v, mem_prompt,
           w_in, w_mem_kv, a_rel_bias, diff_lambda, diff_subln_g, w_out, ln_g, ln_b):
    depth = w_in.shape[0]
    assert depth == 1, "single-layer step only"
    n_seq, seq, _ = x_prompt.shape
    n_streams, t_new, _ = x_sample.shape
    past_a = cache_a_k.shape[2]
    past_b = cache_b_k.shape[2]
    assert seq % ROW_TILE == 0 and BAND_ROWS == ROW_TILE == 2 * MERGE_TILE and past_a == BAND_ROWS
    assert n_streams * t_new == ROW_TILE
    layer = 0
    lambda_init = 0.8 - 0.6 * math.exp(-0.3 * layer)
    alpha = (2.0 * depth) ** 0.25

    w_in_b = w_in[layer].astype(BF16)
    w_mem_b = w_mem_kv[layer].astype(BF16)
    w_out_b = w_out[layer].astype(BF16)
    table = a_rel_bias[layer]
    lam_p = diff_lambda[layer]
    subln_g = diff_subln_g[layer].reshape(1, B_VDIM)
    lng = ln_g[layer].reshape(1, D_MODEL)
    lnb = ln_b[layer].reshape(1, D_MODEL)

    rows = n_seq * seq
    x2 = x_prompt.reshape(rows, D_MODEL)
    cos_p, sin_p = _rope_tables(jnp.arange(seq))
    (aq, akb, avb, akf, avf, bqT, bkb, bvT, bkf, bvf, mq, g) = _project(
        x2, w_in_b, cos_p, sin_p, prompt=True, rows_per_seq=seq)
    mk, mv = _mem_kv(mem_prompt.reshape(n_seq * N_MEM, D_MODEL), w_mem_b)
    ob = _diff_attention(bqT, bkb, bvT, lam_p, subln_g, n_seq=n_seq, seq=seq, lambda_init=lambda_init)
    y_p = _merge_prompt(x2, aq, akb, avb, mq, mk, mv, ob, g, _band_bias_prompt(table), w_out_b, lng, lnb,
                        seq=seq, alpha=alpha)

    xs2 = x_sample.reshape(n_streams * t_new, D_MODEL)
    pos_s = past_b + jnp.arange(t_new)
    cos_s, sin_s = _rope_tables(jnp.tile(pos_s, n_streams))
    sq, sk, sv, tq, tk, tv, nq, sg = _project(xs2, w_in_b, cos_s, sin_s, prompt=False, rows_per_seq=t_new)
    ti = jnp.arange(t_new)[:, None]
    biasc = _rel_bias(table, ti + past_a - jnp.arange(past_a)[None, :])
    biasn = _rel_bias(table, ti - jnp.arange(t_new)[None, :])
    y_s = _sample_step(
        xs2, sq, sk, sv, tq, tk, tv, nq, sg,
        cache_a_k[layer].reshape(n_streams, past_a, A_WIDTH), cache_a_v[layer].reshape(n_streams, past_a, A_WIDTH),
        jnp.transpose(cache_b_k[layer], (0, 2, 3, 4, 1)),
        cache_b_v[layer].reshape(n_streams, past_b * N_HEADS, B_VDIM),
        cache_mem_k[layer].reshape(n_streams, N_MEM, M_WIDTH), cache_mem_v[layer].reshape(n_streams, N_MEM, M_WIDTH),
        biasc, biasn, lam_p, subln_g, w_out_b, lng, lnb,
        n_streams=n_streams, t_new=t_new, lambda_init=lambda_init, alpha=alpha)

    hd = (N_HEADS, HEAD_DIM)
    return (
        y_p.reshape(n_seq, seq, D_MODEL),
        y_s.reshape(n_streams, t_new, D_MODEL),
        akf.reshape(1, n_seq, BAND_ROWS, *hd),
        avf.reshape(1, n_seq, BAND_ROWS, *hd),
        jnp.transpose(bkf, (0, 4, 1, 2, 3))[None],
        bvf.reshape(1, n_seq, seq, N_HEADS, B_VDIM),
        mk.reshape(1, n_seq, N_MEM, *hd),
        mv.reshape(1, n_seq, N_MEM, *hd),
        sk.reshape(1, n_streams, t_new, *hd),
        sv.reshape(1, n_streams, t_new, *hd),
        tk.reshape(1, n_streams, t_new, N_HEADS, 2, HEAD_DIM),
        tv.reshape(1, n_streams, t_new, N_HEADS, B_VDIM),
    )
```

```python
import functools
import math

import jax
import jax.numpy as jnp
from jax import lax
from jax.experimental import pallas as pl
from jax.experimental.pallas import tpu as pltpu

F32 = jnp.float32
BF16 = jnp.bfloat16

D_MODEL = 1024
CHUNK = 64
N_BAND_CHUNKS = 8
BAND_ROWS = N_BAND_CHUNKS * CHUNK
HEAD_DIM = 64
A_WIDTH = 256
B_WIDTH = 512
B_VDIM = 128
M_WIDTH = 256
N_MEM = 256
N_HEADS = 4
MIX_WIDTH = A_WIDTH + B_WIDTH + M_WIDTH
REL_CLIP = 128
ROPE_THETA = 10000.0
LN_EPS = 1e-5
RMS_EPS = 1e-5
NEG_INF = -1e30
QK_SCALE = HEAD_DIM ** -0.5
LOG2E = math.log2(math.e)

_OFF = {}
_o = 0
for _name, _w in (("aq", A_WIDTH), ("ak", A_WIDTH), ("av", A_WIDTH), ("bq", B_WIDTH), ("bk", B_WIDTH),
                  ("bv", B_WIDTH), ("mq", M_WIDTH), ("gate", MIX_WIDTH)):
    _OFF[_name] = (_o, _o + _w)
    _o += _w
PROJ_TOTAL = _o

LANES = 128
ROW_TILE = 512
KV_TILE = ROW_TILE // 2
ONES_ROWS = 16
MERGE_TILE = 256
VMEM_LIMIT = 56 * 1024 * 1024


def _cparams(sem):
    return pltpu.CompilerParams(dimension_semantics=sem, vmem_limit_bytes=VMEM_LIMIT)


def _rope_slab(x, cos, sin_signed, lo_half):
    left = pltpu.roll(x, LANES - 32, 1)
    right = pltpu.roll(x, 32, 1)
    swapped = jnp.where(lo_half, left, right)
    return x * cos + swapped * sin_signed


def _proj_kernel(x_ref, w_ref, cos_ref, sin_ref, *out_refs, prompt):
    if prompt:
        (aq_ref, akb_ref, avb_ref, akf_ref, avf_ref, bqT_ref, bkb_ref, bvT_ref,
         bkf_ref, bvf_ref, mq_ref, g_ref) = out_refs
    else:
        aq_ref, akf_ref, avf_ref, bq_ref, bkf_ref, bvf_ref, mq_ref, g_ref = out_refs
    xb = x_ref[...].astype(BF16)

    def seg(name):
        lo, hi = _OFF[name]
        return jnp.dot(xb, w_ref[:, lo:hi], preferred_element_type=F32)

    aq_ref[...] = (seg("aq") * QK_SCALE).astype(BF16)
    ak = seg("ak")
    av = seg("av")
    akf_ref[...] = ak
    avf_ref[...] = av
    if prompt:
        akb_ref[...] = ak.astype(BF16)
        avb_ref[...] = av.astype(BF16)

    cos = cos_ref[...]
    sin = sin_ref[...]
    lane = lax.broadcasted_iota(jnp.int32, cos.shape, 1)
    lo_half = (lane % HEAD_DIM) < (HEAD_DIM // 2)
    bq = seg("bq")
    bk = seg("bk")
    bv = seg("bv")
    n_rows = bv.shape[0]
    for c in range(B_WIDTH // LANES):
        sl = slice(c * LANES, (c + 1) * LANES)
        q_c = _rope_slab(bq[:, sl], cos, sin, lo_half)
        k_c = _rope_slab(bk[:, sl], cos, sin, lo_half)
        bvf_ref[pl.ds(c, n_rows, stride=N_HEADS), :] = bv[:, sl]
        if prompt:
            kT_c = k_c.T
            bkf_ref[0, c, 0] = kT_c[:HEAD_DIM]
            bkf_ref[0, c, 1] = kT_c[HEAD_DIM:]
            bkb_ref[:, sl] = k_c.astype(BF16)
            bqT_ref[0, sl, :] = (q_c * (QK_SCALE * LOG2E)).T.astype(BF16)
            vT_c = bv[:, sl].T.astype(BF16)
            for half in range(ROW_TILE // KV_TILE):
                bvT_ref[half, sl, :] = vT_c[:, half * KV_TILE:(half + 1) * KV_TILE]
        else:
            bkf_ref[pl.ds(2 * c, n_rows, stride=2 * N_HEADS), :] = k_c[:, :HEAD_DIM]
            bkf_ref[pl.ds(2 * c + 1, n_rows, stride=2 * N_HEADS), :] = k_c[:, HEAD_DIM:]
            bq_ref[:, sl] = (q_c * QK_SCALE).astype(BF16)

    mq_ref[...] = (seg("mq") * QK_SCALE).astype(BF16)
    gate = seg("gate")
    g_ref[...] = (gate / (1.0 + jnp.exp(-gate))).astype(BF16)


def _project(x2, w_in_b, cos, sin, *, prompt, rows_per_seq):
    rows = x2.shape[0]
    tm = ROW_TILE
    n_tiles = rows // tm
    pos_tiles = cos.shape[0] // tm
    row_spec = lambda w: pl.BlockSpec((tm, w), lambda i: (i, 0))
    bk_rows, bv_rows = 2 * N_HEADS, N_HEADS
    bkf_shape = jax.ShapeDtypeStruct((rows * bk_rows, HEAD_DIM), F32)
    bvf_shape = jax.ShapeDtypeStruct((rows * bv_rows, B_VDIM), F32)
    bkf_spec = pl.BlockSpec((tm * bk_rows, HEAD_DIM), lambda i: (i, 0))
    bvf_spec = pl.BlockSpec((tm * bv_rows, B_VDIM), lambda i: (i, 0))
    in_specs = [
        row_spec(D_MODEL),
        pl.BlockSpec((D_MODEL, PROJ_TOTAL), lambda i: (0, 0)),
        pl.BlockSpec((tm, LANES), lambda i: (i % pos_tiles, 0)),
        pl.BlockSpec((tm, LANES), lambda i: (i % pos_tiles, 0)),
    ]
    sds = jax.ShapeDtypeStruct
    if prompt:
        tiles_per_seq = rows_per_seq // tm
        n_seq = rows // rows_per_seq
        tail_spec = pl.BlockSpec((tm, A_WIDTH), lambda i: (i // tiles_per_seq, 0))
        t_spec = pl.BlockSpec((1, B_WIDTH, tm), lambda i: (i, 0, 0))
        kv_per_row_tile = tm // KV_TILE
        vt_spec = pl.BlockSpec((kv_per_row_tile, B_WIDTH, KV_TILE), lambda i: (i, 0, 0))
        bkf_shape = jax.ShapeDtypeStruct((n_seq, N_HEADS, 2, HEAD_DIM, rows_per_seq), F32)
        bkf_spec = pl.BlockSpec((1, N_HEADS, 2, HEAD_DIM, tm),
                                lambda i: (i // tiles_per_seq, 0, 0, 0, i % tiles_per_seq))
        out_shape = (
            sds((rows, A_WIDTH), BF16), sds((rows, A_WIDTH), BF16), sds((rows, A_WIDTH), BF16),
            sds((n_seq * tm, A_WIDTH), F32), sds((n_seq * tm, A_WIDTH), F32),
            sds((n_tiles, B_WIDTH, tm), BF16), sds((rows, B_WIDTH), BF16),
            sds((n_tiles * kv_per_row_tile, B_WIDTH, KV_TILE), BF16),
            bkf_shape, bvf_shape,
            sds((rows, M_WIDTH), BF16), sds((rows, MIX_WIDTH), BF16),
        )
        out_specs = (
            row_spec(A_WIDTH), row_spec(A_WIDTH), row_spec(A_WIDTH), tail_spec, tail_spec,
            t_spec, row_spec(B_WIDTH), vt_spec, bkf_spec, bvf_spec,
            row_spec(M_WIDTH), row_spec(MIX_WIDTH),
        )
    else:
        out_shape = (
            sds((rows, A_WIDTH), BF16), sds((rows, A_WIDTH), F32), sds((rows, A_WIDTH), F32),
            sds((rows, B_WIDTH), BF16), bkf_shape, bvf_shape,
            sds((rows, M_WIDTH), BF16), sds((rows, MIX_WIDTH), BF16),
        )
        out_specs = (
            row_spec(A_WIDTH), row_spec(A_WIDTH), row_spec(A_WIDTH),
            row_spec(B_WIDTH), bkf_spec, bvf_spec,
            row_spec(M_WIDTH), row_spec(MIX_WIDTH),
        )
    return pl.pallas_call(
        functools.partial(_proj_kernel, prompt=prompt),
        out_shape=out_shape,
        grid=(n_tiles,),
        in_specs=in_specs,
        out_specs=out_specs,
        compiler_params=_cparams(("arbitrary",)),
        name="proj_prompt" if prompt else "proj_sample",
    )(x2, w_in_b, cos, sin)


def _mem_kv_kernel(mem_ref, w_ref, mk_ref, mv_ref):
    kv = jnp.dot(mem_ref[...].astype(BF16), w_ref[...], preferred_element_type=F32)
    mk_ref[...] = kv[:, :M_WIDTH]
    mv_ref[...] = kv[:, M_WIDTH:]


def _mem_kv(mem2, w_b):
    rows = mem2.shape[0]
    tm = N_MEM
    return pl.pallas_call(
        _mem_kv_kernel,
        out_shape=(jax.ShapeDtypeStruct((rows, M_WIDTH), F32), jax.ShapeDtypeStruct((rows, M_WIDTH), F32)),
        grid=(rows // tm,),
        in_specs=[pl.BlockSpec((tm, D_MODEL), lambda i: (i, 0)),
                  pl.BlockSpec((D_MODEL, 2 * M_WIDTH), lambda i: (0, 0))],
        out_specs=(pl.BlockSpec((tm, M_WIDTH), lambda i: (i, 0)), pl.BlockSpec((tm, M_WIDTH), lambda i: (i, 0))),
        compiler_params=_cparams(("arbitrary",)),
        name="mem_kv",
    )(mem2, w_b)


def _lambda_value(lp, lambda_init):
    a = jnp.sum(lp[0:1, :] * lp[1:2, :], axis=1, keepdims=True)
    b = jnp.sum(lp[2:3, :] * lp[3:4, :], axis=1, keepdims=True)
    return jnp.exp(a) - jnp.exp(b) + lambda_init


def _diff_post(o, g, lambda_init):
    ms = jnp.mean(o * o, axis=-1, keepdims=True)
    return o * lax.rsqrt(ms + RMS_EPS) * g * (1.0 - lambda_init)


def _diff_attn_kernel(qT_ref, k_ref, vT_ref, lam_ref, g_ref, o_ref,
                      s0_ref, s1_ref, p0_ref, p1_ref, a0_ref, a1_ref, t0_ref, t1_ref, m_ref, acc_ref,
                      *, tq, tk, lambda_init):
    t = pl.program_id(2)
    qT = qT_ref[0].astype(F32)
    row = lax.broadcasted_iota(jnp.int32, qT.shape, 0)
    q_maps = (jnp.where(row < HEAD_DIM, qT, 0.0).astype(BF16),
              jnp.where(row >= HEAD_DIM, qT, 0.0).astype(BF16))
    acc_ref[...] = jnp.zeros_like(acc_ref)
    m_ref[...] = jnp.full(m_ref.shape, NEG_INF, F32)
    s_refs = (s0_ref, s1_ref)
    p_refs = (p0_ref, p1_ref)
    a_refs = (a0_ref, a1_ref)
    t_refs = (t0_ref, t1_ref)
    ones_rows = jnp.ones((ONES_ROWS, tk), BF16)

    def qk(j, buf, visible=None):
        k = k_ref[pl.ds(pl.multiple_of(j * tk, tk), tk), :]
        for mp in range(2):
            s = jnp.dot(k, q_maps[mp], preferred_element_type=F32)
            if visible is not None:
                s = jnp.where(visible, s, NEG_INF)
            s_refs[buf][mp] = s
            t_refs[buf][mp] = jnp.max(s, axis=0, keepdims=True)

    def softmax(buf):
        for mp in range(2):
            m_new = jnp.maximum(m_ref[mp], t_refs[buf][mp])
            a_refs[buf][mp] = jnp.exp2(m_ref[mp] - m_new)
            m_ref[mp] = m_new
            p_refs[buf][mp] = jnp.exp2(s_refs[buf][mp] - m_new).astype(BF16)

    def pv(j, buf):
        vT = jnp.concatenate([vT_ref[j], ones_rows], axis=0)
        for mp in range(2):
            upd = jnp.dot(vT, p_refs[buf][mp], preferred_element_type=F32)
            acc_ref[mp] = a_refs[buf][mp] * acc_ref[mp] + upd

    def diag_visible(half):
        kc = lax.broadcasted_iota(jnp.int32, (tk, tq), 0) // CHUNK + half * (tk // CHUNK)
        qc = lax.broadcasted_iota(jnp.int32, (tk, tq), 1) // CHUNK
        return kc <= qc

    def tick_pair(j, masked):
        pv(j - 2, 0)
        qk(j, 0, diag_visible(0) if masked else None)
        softmax(1)
        pv(j - 1, 1)
        qk(j + 1, 1, diag_visible(1) if masked else None)
        softmax(0)

    qk(0, 0, diag_visible(0) | (t > 0))
    qk(1, 1, diag_visible(1) | (t > 0))
    softmax(0)

    def body(i, c):
        tick_pair(2 + 4 * i, False)
        tick_pair(4 + 4 * i, False)
        return c

    n_plain = jnp.maximum(t - 1, 0)
    lax.fori_loop(0, n_plain // 2, body, 0)

    @pl.when(n_plain % 2 == 1)
    def _():
        tick_pair(2 * t - 2, False)

    @pl.when(t > 0)
    def _():
        tick_pair(2 * t, True)

    pv(2 * t, 0)
    softmax(1)
    pv(2 * t + 1, 1)

    lam = _lambda_value(lam_ref[...], lambda_init)
    inv0 = 1.0 / acc_ref[0, B_VDIM:B_VDIM + 1]
    inv1 = 1.0 / acc_ref[1, B_VDIM:B_VDIM + 1]
    oT = acc_ref[0, :B_VDIM] * inv0 - lam * (acc_ref[1, :B_VDIM] * inv1)
    o_ref[...] = _diff_post(oT.T, g_ref[...], lambda_init)


def _diff_attention(bqT, bkb, bvT, lam_p, subln_g, *, n_seq, seq, lambda_init):
    tq, tk = ROW_TILE, KV_TILE
    nq = seq // tq
    nk = seq // tk
    rows = n_seq * seq
    bvT4 = bvT.reshape(n_seq, nk, B_WIDTH, tk)
    return pl.pallas_call(
        functools.partial(_diff_attn_kernel, tq=tq, tk=tk, lambda_init=lambda_init),
        out_shape=jax.ShapeDtypeStruct((rows, B_WIDTH), F32),
        grid=(n_seq, N_HEADS, nq),
        in_specs=[
            pl.BlockSpec((1, B_VDIM, tq), lambda b, h, t: (b * nq + t, h, 0)),
            pl.BlockSpec((seq, B_VDIM), lambda b, h, t: (b, h)),
            pl.BlockSpec((None, nk, B_VDIM, tk), lambda b, h, t: (b, 0, h, 0)),
            pl.BlockSpec((4, HEAD_DIM), lambda b, h, t: (0, 0)),
            pl.BlockSpec((1, B_VDIM), lambda b, h, t: (0, 0)),
        ],
        out_specs=pl.BlockSpec((tq, B_VDIM), lambda b, h, t: (b * nq + t, h)),
        scratch_shapes=[
            pltpu.VMEM((2, tk, tq), F32), pltpu.VMEM((2, tk, tq), F32),
            pltpu.VMEM((2, tk, tq), BF16), pltpu.VMEM((2, tk, tq), BF16),
            pltpu.VMEM((2, 1, tq), F32), pltpu.VMEM((2, 1, tq), F32),
            pltpu.VMEM((2, 1, tq), F32), pltpu.VMEM((2, 1, tq), F32),
            pltpu.VMEM((2, 1, tq), F32),
            pltpu.VMEM((2, B_VDIM + ONES_ROWS, tq), F32),
        ],
        compiler_params=_cparams(("arbitrary", "arbitrary", "arbitrary")),
        name="diff_attn",
    )(bqT, bkb, bvT4, lam_p, subln_g)


def _head_masks(width):
    lane = lax.broadcasted_iota(jnp.int32, (1, width), 1)
    return [(lane >= h * HEAD_DIM) & (lane < (h + 1) * HEAD_DIM) for h in range(N_HEADS)]


def _dot_nt(a, b):
    return lax.dot_general(a, b, (((1,), (1,)), ((), ())), preferred_element_type=F32)


def _softmax_parts(blocks):
    m = functools.reduce(jnp.maximum, [jnp.max(s, axis=-1, keepdims=True) for s in blocks])
    ps = [jnp.exp(s - m) for s in blocks]
    l = functools.reduce(jnp.add, [jnp.sum(p, axis=-1, keepdims=True) for p in ps])
    return ps, l


def _attend(q_masked, keys, vals, biases):
    blocks = []
    for kk, bb in zip(keys, biases):
        s = _dot_nt(q_masked, kk)
        blocks.append(s if bb is None else s + bb)
    return _softmax_pv(blocks, vals)


def _softmax_pv(blocks, vals):
    ps, l = _softmax_parts(blocks)
    o = functools.reduce(jnp.add, [jnp.dot(p.astype(BF16), vv, preferred_element_type=F32)
                                   for p, vv in zip(ps, vals)])
    return o * (1.0 / l)


def _merge_out(x, mixed, g, w_out_ref, lng, lnb, alpha):
    mg = (mixed * g.astype(F32)).astype(BF16)
    y = jnp.dot(mg, w_out_ref[...], preferred_element_type=F32)
    z = alpha * x + y
    mu = jnp.mean(z, axis=-1, keepdims=True)
    zc = z - mu
    var = jnp.mean(zc * zc, axis=-1, keepdims=True)
    return zc * lax.rsqrt(var + LN_EPS) * lng + lnb


def _merge_kernel(x_ref, aq_ref, k0_ref, k1_ref, k2_ref, v0_ref, v1_ref, v2_ref, mq_ref, mk_ref, mv_ref,
                  ob_ref, g_ref, bias_ref, wout_ref, lng_ref, lnb_ref, y_ref, *, tiles_per_seq, alpha):
    t = pl.program_id(0) % tiles_per_seq
    masks = _head_masks(A_WIDTH)
    aq = aq_ref[...].astype(F32)
    mq = mq_ref[...].astype(F32)
    keys = [k0_ref[...], k1_ref[...], k2_ref[...]]
    vals = [v0_ref[...], v1_ref[...], v2_ref[...]]
    mk = mk_ref[...].astype(BF16)
    mv = mv_ref[...].astype(BF16)
    pen0 = jnp.where(t >= 2, 0.0, NEG_INF).astype(F32)
    pen1 = jnp.where(t >= 1, 0.0, NEG_INF).astype(F32)
    tile = MERGE_TILE
    o_a = jnp.zeros((tile, A_WIDTH), F32)
    o_m = jnp.zeros((tile, M_WIDTH), F32)
    pens = (pen0, pen1, None)
    a_scores, m_scores = [], []
    for h in range(N_HEADS):
        qh = jnp.where(masks[h], aq, 0.0).astype(BF16)
        blocks = []
        for b in range(3):
            s = _dot_nt(qh, keys[b]) + bias_ref[h, :, b * tile:(b + 1) * tile]
            blocks.append(s if pens[b] is None else s + pens[b])
        a_scores.append(blocks)
        qmh = jnp.where(masks[h], mq, 0.0).astype(BF16)
        m_scores.append([_dot_nt(qmh, mk)])
    for h in range(N_HEADS):
        o_a = jnp.where(masks[h], _softmax_pv(a_scores[h], vals), o_a)
        o_m = jnp.where(masks[h], _softmax_pv(m_scores[h], [mv]), o_m)
    mixed = jnp.concatenate([o_a, ob_ref[...], o_m], axis=1)
    y_ref[...] = _merge_out(x_ref[...], mixed, g_ref[...], wout_ref, lng_ref[...], lnb_ref[...], alpha)


def _merge_prompt(x2, aq, akb, avb, mq, mk, mv, ob, g, bias, w_out_b, lng, lnb, *, seq, alpha):
    rows = x2.shape[0]
    tile = MERGE_TILE
    tps = seq // tile
    n_tiles = rows // tile

    def prev(d):
        return lambda i: ((i // tps) * tps + jnp.maximum(i % tps - d, 0), 0)

    row_spec = lambda w: pl.BlockSpec((tile, w), lambda i: (i, 0))
    kv_specs = [pl.BlockSpec((tile, A_WIDTH), prev(d)) for d in (2, 1, 0)]
    const = lambda shape: pl.BlockSpec(shape, lambda i: (0,) * len(shape))
    return pl.pallas_call(
        functools.partial(_merge_kernel, tiles_per_seq=tps, alpha=alpha),
        out_shape=jax.ShapeDtypeStruct((rows, D_MODEL), F32),
        grid=(n_tiles,),
        in_specs=[row_spec(D_MODEL), row_spec(A_WIDTH)] + kv_specs + kv_specs + [
            row_spec(M_WIDTH),
            pl.BlockSpec((N_MEM, M_WIDTH), lambda i: (i // tps, 0)),
            pl.BlockSpec((N_MEM, M_WIDTH), lambda i: (i // tps, 0)),
            row_spec(B_WIDTH), row_spec(MIX_WIDTH),
            const((N_HEADS, tile, 3 * tile)), const((MIX_WIDTH, D_MODEL)), const((1, D_MODEL)), const((1, D_MODEL)),
        ],
        out_specs=row_spec(D_MODEL),
        compiler_params=_cparams(("arbitrary",)),
        name="merge_prompt",
    )(x2, aq, akb, akb, akb, avb, avb, avb, mq, mk, mv, ob, g, bias, w_out_b, lng, lnb)


def _sample_kernel(x_ref, sq_ref, sk_ref, sv_ref, tq_ref, tk_ref, tv_ref, nq_ref, g_ref,
                   cak_ref, cav_ref, cbk_ref, cbv_ref, cmk_ref, cmv_ref,
                   biasc_ref, biasn_ref, lam_ref, subg_ref, wout_ref, lng_ref, lnb_ref, y_ref,
                   *, lambda_init, alpha):
    masks = _head_masks(A_WIDTH)
    sq = sq_ref[...].astype(F32)
    nq = nq_ref[...].astype(F32)
    a_keys = [cak_ref[0].astype(BF16), sk_ref[...].astype(BF16)]
    a_vals = [cav_ref[0].astype(BF16), sv_ref[...].astype(BF16)]
    mk = cmk_ref[0].astype(BF16)
    mv = cmv_ref[0].astype(BF16)
    rows = sq.shape[0]
    a_scores, m_scores = [], []
    for h in range(N_HEADS):
        qh = jnp.where(masks[h], sq, 0.0).astype(BF16)
        a_scores.append([_dot_nt(qh, a_keys[0]) + biasc_ref[h], _dot_nt(qh, a_keys[1]) + biasn_ref[h]])
        qmh = jnp.where(masks[h], nq, 0.0).astype(BF16)
        m_scores.append([_dot_nt(qmh, mk)])
    past_b = cbv_ref.shape[1] // N_HEADS
    b_scores = []
    for h in range(N_HEADS):
        q = tq_ref[:, h * B_VDIM:(h + 1) * B_VDIM].astype(F32)
        q_maps = [q[:, :HEAD_DIM], pltpu.roll(q, HEAD_DIM, 1)[:, :HEAD_DIM]]
        for mp in range(2):
            q_m = q_maps[mp].astype(BF16)
            k_new = tk_ref[pl.ds(2 * h + mp, rows, stride=2 * N_HEADS), :].astype(BF16)
            b_scores.append([jnp.dot(q_m, cbk_ref[0, h, mp].astype(BF16), preferred_element_type=F32),
                             _dot_nt(q_m, k_new)])

    o_a = jnp.zeros((rows, A_WIDTH), F32)
    o_m = jnp.zeros((rows, M_WIDTH), F32)
    for h in range(N_HEADS):
        o_a = jnp.where(masks[h], _softmax_pv(a_scores[h], a_vals), o_a)
        o_m = jnp.where(masks[h], _softmax_pv(m_scores[h], [mv]), o_m)
    lam = _lambda_value(lam_ref[...], lambda_init)
    subg = subg_ref[...]
    o_b = []
    for h in range(N_HEADS):
        vals = [cbv_ref[0, pl.ds(h, past_b, stride=N_HEADS), :].astype(BF16),
                tv_ref[pl.ds(h, rows, stride=N_HEADS), :].astype(BF16)]
        o_maps = [_softmax_pv(b_scores[2 * h + mp], vals) for mp in range(2)]
        o_b.append(_diff_post(o_maps[0] - lam * o_maps[1], subg, lambda_init))
    mixed = jnp.concatenate([o_a] + o_b + [o_m], axis=1)
    y_ref[...] = _merge_out(x_ref[...], mixed, g_ref[...], wout_ref, lng_ref[...], lnb_ref[...], alpha)


def _sample_step(xs2, sq, sk, sv, tq, tk, tv, nq, g, cak, cav, cbk, cbv, cmk, cmv, biasc, biasn,
                 lam_p, subln_g, w_out_b, lng, lnb, *, n_streams, t_new, lambda_init, alpha):
    past_a = cak.shape[1]
    bk_rows, bv_rows = 2 * N_HEADS, N_HEADS
    past_b = cbv.shape[1] // bv_rows
    row_spec = lambda w: pl.BlockSpec((t_new, w), lambda n: (n, 0))
    nrow_spec = lambda r, w: pl.BlockSpec((t_new * r, w), lambda n: (n, 0))
    cache_spec = lambda r, w: pl.BlockSpec((1, r, w), lambda n: (n, 0, 0))
    const = lambda shape: pl.BlockSpec(shape, lambda n: (0,) * len(shape))
    return pl.pallas_call(
        functools.partial(_sample_kernel, lambda_init=lambda_init, alpha=alpha),
        out_shape=jax.ShapeDtypeStruct((n_streams * t_new, D_MODEL), F32),
        grid=(n_streams,),
        in_specs=[
            row_spec(D_MODEL), row_spec(A_WIDTH), row_spec(A_WIDTH), row_spec(A_WIDTH),
            row_spec(B_WIDTH), nrow_spec(bk_rows, HEAD_DIM), nrow_spec(bv_rows, B_VDIM),
            row_spec(M_WIDTH), row_spec(MIX_WIDTH),
            cache_spec(past_a, A_WIDTH), cache_spec(past_a, A_WIDTH),
            pl.BlockSpec((1, N_HEADS, 2, HEAD_DIM, past_b), lambda n: (n, 0, 0, 0, 0)),
            cache_spec(past_b * bv_rows, B_VDIM),
            cache_spec(N_MEM, M_WIDTH), cache_spec(N_MEM, M_WIDTH),
            const((N_HEADS, t_new, past_a)), const((N_HEADS, t_new, t_new)),
            const((4, HEAD_DIM)), const((1, B_VDIM)),
            const((MIX_WIDTH, D_MODEL)), const((1, D_MODEL)), const((1, D_MODEL)),
        ],
        out_specs=row_spec(D_MODEL),
        compiler_params=_cparams(("arbitrary",)),
        name="sample_step",
    )(xs2, sq, sk, sv, tq, tk, tv, nq, g, cak, cav, cbk, cbv, cmk, cmv, biasc, biasn,
      lam_p, subln_g, w_out_b, lng, lnb)


def _rope_tables(pos):
    half = HEAD_DIM // 2
    inv = ROPE_THETA ** (-jnp.arange(half, dtype=F32) / half)
    ang = pos.astype(F32)[:, None] * inv[None, :]
    cos = jnp.cos(ang)
    sin = jnp.sin(ang)
    return jnp.tile(cos, (1, LANES // half)), jnp.concatenate([-sin, sin, -sin, sin], axis=-1)


def _rel_bias(table, dist):
    return table[:, jnp.clip(dist, -REL_CLIP, REL_CLIP) + REL_CLIP].astype(F32)


def _toeplitz_bias(table, n, width, offset):
    period = -(-(n + width) // LANES) * LANES
    u = jnp.arange(period)
    dist = jnp.where(u < width, offset - u, offset + period - u)
    diagonals = _rel_bias(table, dist)
    flat = jnp.tile(diagonals, (1, n))[:, :n * (period - 1)]
    return flat.reshape(table.shape[0], n, period - 1)[:, :, :width]


def _band_bias_prompt(table):
    n, width = MERGE_TILE, 3 * MERGE_TILE
    bias = _toeplitz_bias(table, n, width, BAND_ROWS)
    i = jnp.arange(n)[:, None]
    j = jnp.arange(width)[None, :]
    qc = i // CHUNK
    kc = j // CHUNK
    visible = (kc >= qc) & (kc <= qc + N_BAND_CHUNKS)
    return jnp.where(visible[None], bias, NEG_INF)


def kernel(x_prompt, x_sample, cache_a_k, cache_a_v, cache_b_k, cache_b_v, cache_mem_k, cache_mem_v, mem_prompt,
           w_in, w_mem_kv, a_rel_bias, diff_lambda, diff_subln_g, w_out, ln_g, ln_b):
    depth = w_in.shape[0]
    assert depth == 1, "single-layer step only"
    n_seq, seq, _ = x_prompt.shape
    n_streams, t_new, _ = x_sample.shape
    past_a = cache_a_k.shape[2]
    past_b = cache_b_k.shape[2]
    assert seq % ROW_TILE == 0 and BAND_ROWS == ROW_TILE == 2 * MERGE_TILE and past_a == BAND_ROWS
    assert n_streams * t_new == ROW_TILE
    layer = 0
    lambda_init = 0.8 - 0.6 * math.exp(-0.3 * layer)
    alpha = (2.0 * depth) ** 0.25

    w_in_b = w_in[layer].astype(BF16)
    w_mem_b = w_mem_kv[layer].astype(BF16)
    w_out_b = w_out[layer].astype(BF16)
    table = a_rel_bias[layer]
    lam_p = diff_lambda[layer]
    subln_g = diff_subln_g[layer].reshape(1, B_VDIM)
    lng = ln_g[layer].reshape(1, D_MODEL)
    lnb = ln_b[layer].reshape(1, D_MODEL)

    rows = n_seq * seq
    x2 = x_prompt.reshape(rows, D_MODEL)
    cos_p, sin_p = _rope_tables(jnp.arange(seq))
    (aq, akb, avb, akf, avf, bqT, bkb, bvT, bkf, bvf, mq, g) = _project(
        x2, w_in_b, cos_p, sin_p, prompt=True, rows_per_seq=seq)
    mk, mv = _mem_kv(mem_prompt.reshape(n_seq * N_MEM, D_MODEL), w_mem_b)
    ob = _diff_attention(bqT, bkb, bvT, lam_p, subln_g, n_seq=n_seq, seq=seq, lambda_init=lambda_init)
    y_p = _merge_prompt(x2, aq, akb, avb, mq, mk, mv, ob, g, _band_bias_prompt(table), w_out_b, lng, lnb,
                        seq=seq, alpha=alpha)

    xs2 = x_sample.reshape(n_streams * t_new, D_MODEL)
    pos_s = past_b + jnp.arange(t_new)
    cos_s, sin_s = _rope_tables(jnp.tile(pos_s, n_streams))
    sq, sk, sv, tq, tk, tv, nq, sg = _project(xs2, w_in_b, cos_s, sin_s, prompt=False, rows_per_seq=t_new)
    bias_s = _toeplitz_bias(table, t_new, past_a + t_new, past_a)
    biasc, biasn = bias_s[:, :, :past_a], bias_s[:, :, past_a:]
    y_s = _sample_step(
        xs2, sq, sk, sv, tq, tk, tv, nq, sg,
        cache_a_k[layer].reshape(n_streams, past_a, A_WIDTH), cache_a_v[layer].reshape(n_streams, past_a, A_WIDTH),
        jnp.transpose(cache_b_k[layer], (0, 2, 3, 4, 1)),
        cache_b_v[layer].reshape(n_streams, past_b * N_HEADS, B_VDIM),
        cache_mem_k[layer].reshape(n_streams, N_MEM, M_WIDTH), cache_mem_v[layer].reshape(n_streams, N_MEM, M_WIDTH),
        biasc, biasn, lam_p, subln_g, w_out_b, lng, lnb,
        n_streams=n_streams, t_new=t_new, lambda_init=lambda_init, alpha=alpha)

    hd = (N_HEADS, HEAD_DIM)
    return (
        y_p.reshape(n_seq, seq, D_MODEL),
        y_s.reshape(n_streams, t_new, D_MODEL),
        akf.reshape(1, n_seq, BAND_ROWS, *hd),
        avf.reshape(1, n_seq, BAND_ROWS, *hd),
        jnp.transpose(bkf, (0, 4, 1, 2, 3))[None],
        bvf.reshape(1, n_seq, seq, N_HEADS, B_VDIM),
        mk.reshape(1, n_seq, N_MEM, *hd),
        mv.reshape(1, n_seq, N_MEM, *hd),
        sk.reshape(1, n_streams, t_new, *hd),
        sv.reshape(1, n_streams, t_new, *hd),
        tk.reshape(1, n_streams, t_new, N_HEADS, 2, HEAD_DIM),
        tv.reshape(1, n_streams, t_new, N_HEADS, B_VDIM),
    )
```

```python
import functools
import math

import jax
import jax.numpy as jnp
from jax import lax
from jax.experimental import pallas as pl
from jax.experimental.pallas import tpu as pltpu

F32 = jnp.float32
BF16 = jnp.bfloat16

D_MODEL = 1024
CHUNK = 64
N_BAND_CHUNKS = 8
BAND_ROWS = N_BAND_CHUNKS * CHUNK
HEAD_DIM = 64
A_WIDTH = 256
B_WIDTH = 512
B_VDIM = 128
M_WIDTH = 256
N_MEM = 256
N_HEADS = 4
MIX_WIDTH = A_WIDTH + B_WIDTH + M_WIDTH
REL_CLIP = 128
ROPE_THETA = 10000.0
LN_EPS = 1e-5
RMS_EPS = 1e-5
NEG_INF = -1e30
QK_SCALE = HEAD_DIM ** -0.5
LOG2E = math.log2(math.e)

_OFF = {}
_o = 0
for _name, _w in (("aq", A_WIDTH), ("ak", A_WIDTH), ("av", A_WIDTH), ("bq", B_WIDTH), ("bk", B_WIDTH),
                  ("bv", B_WIDTH), ("mq", M_WIDTH), ("gate", MIX_WIDTH)):
    _OFF[_name] = (_o, _o + _w)
    _o += _w
PROJ_TOTAL = _o

LANES = 128
ROW_TILE = 512
KV_TILE = ROW_TILE // 2
ONES_ROWS = 16
MERGE_TILE = 256
VMEM_LIMIT = 56 * 1024 * 1024


def _cparams(sem):
    return pltpu.CompilerParams(dimension_semantics=sem, vmem_limit_bytes=VMEM_LIMIT)


def _rope_slab(x, cos, sin_signed, lo_half):
    left = pltpu.roll(x, LANES - 32, 1)
    right = pltpu.roll(x, 32, 1)
    swapped = jnp.where(lo_half, left, right)
    return x * cos + swapped * sin_signed


def _proj_kernel(x_ref, w_ref, cos_ref, sin_ref, *out_refs, prompt):
    if prompt:
        (aq_ref, akb_ref, avb_ref, akf_ref, avf_ref, bqT_ref, bkb_ref, bvT_ref,
         bkf_ref, bvf_ref, mq_ref, g_ref) = out_refs
    else:
        aq_ref, akf_ref, avf_ref, bq_ref, bkf_ref, bvf_ref, mq_ref, g_ref = out_refs
    xb = x_ref[...].astype(BF16)

    def seg(name):
        lo, hi = _OFF[name]
        return jnp.dot(xb, w_ref[:, lo:hi], preferred_element_type=F32)

    aq = seg("aq")
    ak = seg("ak")
    av = seg("av")
    akf_ref[...] = ak
    avf_ref[...] = av
    if prompt:
        aq_ref[0] = (aq * (QK_SCALE * LOG2E)).T.astype(BF16)
        akb_ref[...] = ak.astype(BF16)
        avT = av.T.astype(BF16)
        for half in range(ROW_TILE // MERGE_TILE):
            avb_ref[half] = avT[:, half * MERGE_TILE:(half + 1) * MERGE_TILE]
    else:
        aq_ref[...] = (aq * QK_SCALE).astype(BF16)

    cos = cos_ref[...]
    sin = sin_ref[...]
    lane = lax.broadcasted_iota(jnp.int32, cos.shape, 1)
    lo_half = (lane % HEAD_DIM) < (HEAD_DIM // 2)
    bq = seg("bq")
    bk = seg("bk")
    bv = seg("bv")
    n_rows = bv.shape[0]
    for c in range(B_WIDTH // LANES):
        sl = slice(c * LANES, (c + 1) * LANES)
        q_c = _rope_slab(bq[:, sl], cos, sin, lo_half)
        k_c = _rope_slab(bk[:, sl], cos, sin, lo_half)
        bvf_ref[pl.ds(c, n_rows, stride=N_HEADS), :] = bv[:, sl]
        if prompt:
            kT_c = k_c.T
            bkf_ref[0, c, 0] = kT_c[:HEAD_DIM]
            bkf_ref[0, c, 1] = kT_c[HEAD_DIM:]
            bkb_ref[:, sl] = k_c.astype(BF16)
            bqT_ref[0, sl, :] = (q_c * (QK_SCALE * LOG2E)).T.astype(BF16)
            vT_c = bv[:, sl].T.astype(BF16)
            for half in range(ROW_TILE // KV_TILE):
                bvT_ref[half, sl, :] = vT_c[:, half * KV_TILE:(half + 1) * KV_TILE]
        else:
            bkf_ref[pl.ds(2 * c, n_rows, stride=2 * N_HEADS), :] = k_c[:, :HEAD_DIM]
            bkf_ref[pl.ds(2 * c + 1, n_rows, stride=2 * N_HEADS), :] = k_c[:, HEAD_DIM:]
            bq_ref[:, sl] = (q_c * QK_SCALE).astype(BF16)

    if prompt:
        mq_ref[0] = (seg("mq") * (QK_SCALE * LOG2E)).T.astype(BF16)
    else:
        mq_ref[...] = (seg("mq") * QK_SCALE).astype(BF16)
    gate = seg("gate")
    g_ref[...] = (gate / (1.0 + jnp.exp(-gate))).astype(BF16)


def _project(x2, w_in_b, cos, sin, *, prompt, rows_per_seq):
    rows = x2.shape[0]
    tm = ROW_TILE
    n_tiles = rows // tm
    pos_tiles = cos.shape[0] // tm
    row_spec = lambda w: pl.BlockSpec((tm, w), lambda i: (i, 0))
    bk_rows, bv_rows = 2 * N_HEADS, N_HEADS
    bkf_shape = jax.ShapeDtypeStruct((rows * bk_rows, HEAD_DIM), F32)
    bvf_shape = jax.ShapeDtypeStruct((rows * bv_rows, B_VDIM), F32)
    bkf_spec = pl.BlockSpec((tm * bk_rows, HEAD_DIM), lambda i: (i, 0))
    bvf_spec = pl.BlockSpec((tm * bv_rows, B_VDIM), lambda i: (i, 0))
    in_specs = [
        row_spec(D_MODEL),
        pl.BlockSpec((D_MODEL, PROJ_TOTAL), lambda i: (0, 0)),
        pl.BlockSpec((tm, LANES), lambda i: (i % pos_tiles, 0)),
        pl.BlockSpec((tm, LANES), lambda i: (i % pos_tiles, 0)),
    ]
    sds = jax.ShapeDtypeStruct
    if prompt:
        tiles_per_seq = rows_per_seq // tm
        n_seq = rows // rows_per_seq
        tail_spec = pl.BlockSpec((tm, A_WIDTH), lambda i: (i // tiles_per_seq, 0))
        t_spec = pl.BlockSpec((1, B_WIDTH, tm), lambda i: (i, 0, 0))
        kv_per_row_tile = tm // KV_TILE
        vt_spec = pl.BlockSpec((kv_per_row_tile, B_WIDTH, KV_TILE), lambda i: (i, 0, 0))
        bkf_shape = jax.ShapeDtypeStruct((n_seq, N_HEADS, 2, HEAD_DIM, rows_per_seq), F32)
        bkf_spec = pl.BlockSpec((1, N_HEADS, 2, HEAD_DIM, tm),
                                lambda i: (i // tiles_per_seq, 0, 0, 0, i % tiles_per_seq))
        qt_spec = pl.BlockSpec((1, A_WIDTH, tm), lambda i: (i, 0, 0))
        merge_per_row_tile = tm // MERGE_TILE
        avt_spec = pl.BlockSpec((merge_per_row_tile, A_WIDTH, MERGE_TILE), lambda i: (i, 0, 0))
        out_shape = (
            sds((n_tiles, A_WIDTH, tm), BF16), sds((rows, A_WIDTH), BF16),
            sds((n_tiles * merge_per_row_tile, A_WIDTH, MERGE_TILE), BF16),
            sds((n_seq * tm, A_WIDTH), F32), sds((n_seq * tm, A_WIDTH), F32),
            sds((n_tiles, B_WIDTH, tm), BF16), sds((rows, B_WIDTH), BF16),
            sds((n_tiles * kv_per_row_tile, B_WIDTH, KV_TILE), BF16),
            bkf_shape, bvf_shape,
            sds((n_tiles, M_WIDTH, tm), BF16), sds((rows, MIX_WIDTH), BF16),
        )
        out_specs = (
            qt_spec, row_spec(A_WIDTH), avt_spec, tail_spec, tail_spec,
            t_spec, row_spec(B_WIDTH), vt_spec, bkf_spec, bvf_spec,
            qt_spec, row_spec(MIX_WIDTH),
        )
    else:
        out_shape = (
            sds((rows, A_WIDTH), BF16), sds((rows, A_WIDTH), F32), sds((rows, A_WIDTH), F32),
            sds((rows, B_WIDTH), BF16), bkf_shape, bvf_shape,
            sds((rows, M_WIDTH), BF16), sds((rows, MIX_WIDTH), BF16),
        )
        out_specs = (
            row_spec(A_WIDTH), row_spec(A_WIDTH), row_spec(A_WIDTH),
            row_spec(B_WIDTH), bkf_spec, bvf_spec,
            row_spec(M_WIDTH), row_spec(MIX_WIDTH),
        )
    return pl.pallas_call(
        functools.partial(_proj_kernel, prompt=prompt),
        out_shape=out_shape,
        grid=(n_tiles,),
        in_specs=in_specs,
        out_specs=out_specs,
        compiler_params=_cparams(("arbitrary",)),
        name="proj_prompt" if prompt else "proj_sample",
    )(x2, w_in_b, cos, sin)


def _mem_kv_kernel(mem_ref, w_ref, mk_ref, mv_ref, mkb_ref, mvT_ref):
    kv = jnp.dot(mem_ref[...].astype(BF16), w_ref[...], preferred_element_type=F32)
    mk_ref[...] = kv[:, :M_WIDTH]
    mv_ref[...] = kv[:, M_WIDTH:]
    mkb_ref[...] = kv[:, :M_WIDTH].astype(BF16)
    mvT_ref[...] = kv[:, M_WIDTH:].T.astype(BF16)


def _mem_kv(mem2, w_b):
    rows = mem2.shape[0]
    tm = N_MEM
    spec = pl.BlockSpec((tm, M_WIDTH), lambda i: (i, 0))
    sds = jax.ShapeDtypeStruct
    return pl.pallas_call(
        _mem_kv_kernel,
        out_shape=(sds((rows, M_WIDTH), F32), sds((rows, M_WIDTH), F32),
                   sds((rows, M_WIDTH), BF16), sds((rows, N_MEM), BF16)),
        grid=(rows // tm,),
        in_specs=[pl.BlockSpec((tm, D_MODEL), lambda i: (i, 0)),
                  pl.BlockSpec((D_MODEL, 2 * M_WIDTH), lambda i: (0, 0))],
        out_specs=(spec, spec, spec, spec),
        compiler_params=_cparams(("arbitrary",)),
        name="mem_kv",
    )(mem2, w_b)


def _lambda_value(lp, lambda_init):
    a = jnp.sum(lp[0:1, :] * lp[1:2, :], axis=1, keepdims=True)
    b = jnp.sum(lp[2:3, :] * lp[3:4, :], axis=1, keepdims=True)
    return jnp.exp(a) - jnp.exp(b) + lambda_init


def _diff_post(o, g, lambda_init):
    ms = jnp.mean(o * o, axis=-1, keepdims=True)
    return o * lax.rsqrt(ms + RMS_EPS) * g * (1.0 - lambda_init)


def _diff_attn_kernel(qT_ref, k_ref, vT_ref, lam_ref, g_ref, o_ref,
                      s0_ref, s1_ref, p0_ref, p1_ref, a0_ref, a1_ref, t0_ref, t1_ref, m_ref, acc_ref,
                      *, tq, tk, lambda_init):
    t = pl.program_id(2)
    qT = qT_ref[0].astype(F32)
    row = lax.broadcasted_iota(jnp.int32, qT.shape, 0)
    q_maps = (jnp.where(row < HEAD_DIM, qT, 0.0).astype(BF16),
              jnp.where(row >= HEAD_DIM, qT, 0.0).astype(BF16))
    acc_ref[...] = jnp.zeros_like(acc_ref)
    m_ref[...] = jnp.full(m_ref.shape, NEG_INF, F32)
    s_refs = (s0_ref, s1_ref)
    p_refs = (p0_ref, p1_ref)
    a_refs = (a0_ref, a1_ref)
    t_refs = (t0_ref, t1_ref)
    ones_rows = jnp.ones((ONES_ROWS, tk), BF16)

    def qk(j, buf, visible=None):
        k = k_ref[pl.ds(pl.multiple_of(j * tk, tk), tk), :]
        for mp in range(2):
            s = jnp.dot(k, q_maps[mp], preferred_element_type=F32)
            if visible is not None:
                s = jnp.where(visible, s, NEG_INF)
            s_refs[buf][mp] = s
            t_refs[buf][mp] = jnp.max(s, axis=0, keepdims=True)

    def softmax(buf):
        for mp in range(2):
            m_new = jnp.maximum(m_ref[mp], t_refs[buf][mp])
            a_refs[buf][mp] = jnp.exp2(m_ref[mp] - m_new)
            m_ref[mp] = m_new
            p_refs[buf][mp] = jnp.exp2(s_refs[buf][mp] - m_new).astype(BF16)

    def pv(j, buf):
        vT = jnp.concatenate([vT_ref[j], ones_rows], axis=0)
        for mp in range(2):
            upd = jnp.dot(vT, p_refs[buf][mp], preferred_element_type=F32)
            acc_ref[mp] = a_refs[buf][mp] * acc_ref[mp] + upd

    def diag_visible(half):
        kc = lax.broadcasted_iota(jnp.int32, (tk, tq), 0) // CHUNK + half * (tk // CHUNK)
        qc = lax.broadcasted_iota(jnp.int32, (tk, tq), 1) // CHUNK
        return kc <= qc

    def tick_pair(j, masked):
        pv(j - 2, 0)
        qk(j, 0, diag_visible(0) if masked else None)
        softmax(1)
        pv(j - 1, 1)
        qk(j + 1, 1, diag_visible(1) if masked else None)
        softmax(0)

    qk(0, 0, diag_visible(0) | (t > 0))
    qk(1, 1, diag_visible(1) | (t > 0))
    softmax(0)

    def body(i, c):
        tick_pair(2 + 4 * i, False)
        tick_pair(4 + 4 * i, False)
        return c

    n_plain = jnp.maximum(t - 1, 0)
    lax.fori_loop(0, n_plain // 2, body, 0)

    @pl.when(n_plain % 2 == 1)
    def _():
        tick_pair(2 * t - 2, False)

    @pl.when(t > 0)
    def _():
        tick_pair(2 * t, True)

    pv(2 * t, 0)
    softmax(1)
    pv(2 * t + 1, 1)

    lam = _lambda_value(lam_ref[...], lambda_init)
    inv0 = 1.0 / acc_ref[0, B_VDIM:B_VDIM + 1]
    inv1 = 1.0 / acc_ref[1, B_VDIM:B_VDIM + 1]
    oT = acc_ref[0, :B_VDIM] * inv0 - lam * (acc_ref[1, :B_VDIM] * inv1)
    o_ref[...] = _diff_post(oT.T, g_ref[...], lambda_init)


def _diff_attention(bqT, bkb, bvT, lam_p, subln_g, *, n_seq, seq, lambda_init):
    tq, tk = ROW_TILE, KV_TILE
    nq = seq // tq
    nk = seq // tk
    rows = n_seq * seq
    bvT4 = bvT.reshape(n_seq, nk, B_WIDTH, tk)
    return pl.pallas_call(
        functools.partial(_diff_attn_kernel, tq=tq, tk=tk, lambda_init=lambda_init),
        out_shape=jax.ShapeDtypeStruct((rows, B_WIDTH), F32),
        grid=(n_seq, N_HEADS, nq),
        in_specs=[
            pl.BlockSpec((1, B_VDIM, tq), lambda b, h, t: (b * nq + t, h, 0)),
            pl.BlockSpec((seq, B_VDIM), lambda b, h, t: (b, h)),
            pl.BlockSpec((None, nk, B_VDIM, tk), lambda b, h, t: (b, 0, h, 0)),
            pl.BlockSpec((4, HEAD_DIM), lambda b, h, t: (0, 0)),
            pl.BlockSpec((1, B_VDIM), lambda b, h, t: (0, 0)),
        ],
        out_specs=pl.BlockSpec((tq, B_VDIM), lambda b, h, t: (b * nq + t, h)),
        scratch_shapes=[
            pltpu.VMEM((2, tk, tq), F32), pltpu.VMEM((2, tk, tq), F32),
            pltpu.VMEM((2, tk, tq), BF16), pltpu.VMEM((2, tk, tq), BF16),
            pltpu.VMEM((2, 1, tq), F32), pltpu.VMEM((2, 1, tq), F32),
            pltpu.VMEM((2, 1, tq), F32), pltpu.VMEM((2, 1, tq), F32),
            pltpu.VMEM((2, 1, tq), F32),
            pltpu.VMEM((2, B_VDIM + ONES_ROWS, tq), F32),
        ],
        compiler_params=_cparams(("arbitrary", "arbitrary", "arbitrary")),
        name="diff_attn",
    )(bqT, bkb, bvT4, lam_p, subln_g)


def _head_masks(width):
    lane = lax.broadcasted_iota(jnp.int32, (1, width), 1)
    return [(lane >= h * HEAD_DIM) & (lane < (h + 1) * HEAD_DIM) for h in range(N_HEADS)]


def _dot_nt(a, b):
    return lax.dot_general(a, b, (((1,), (1,)), ((), ())), preferred_element_type=F32)


def _softmax_parts(blocks):
    m = functools.reduce(jnp.maximum, [jnp.max(s, axis=-1, keepdims=True) for s in blocks])
    ps = [jnp.exp(s - m) for s in blocks]
    l = functools.reduce(jnp.add, [jnp.sum(p, axis=-1, keepdims=True) for p in ps])
    return ps, l


def _attend(q_masked, keys, vals, biases):
    blocks = []
    for kk, bb in zip(keys, biases):
        s = _dot_nt(q_masked, kk)
        blocks.append(s if bb is None else s + bb)
    return _softmax_pv(blocks, vals)


def _softmax_pv(blocks, vals):
    ps, l = _softmax_parts(blocks)
    o = functools.reduce(jnp.add, [jnp.dot(p.astype(BF16), vv, preferred_element_type=F32)
                                   for p, vv in zip(ps, vals)])
    return o * (1.0 / l)


def _gate_mix(mixed, g):
    return (mixed * g.astype(F32)).astype(BF16)


def _merge_out(x, mixed, g, w_out_ref, lng, lnb, alpha):
    return _out_norm(x, _gate_mix(mixed, g), w_out_ref, lng, lnb, alpha)


def _out_norm(x, mg, w_out_ref, lng, lnb, alpha):
    y = jnp.dot(mg, w_out_ref[...], preferred_element_type=F32)
    z = alpha * x + y
    mu = jnp.mean(z, axis=-1, keepdims=True)
    zc = z - mu
    var = jnp.mean(zc * zc, axis=-1, keepdims=True)
    return zc * lax.rsqrt(var + LN_EPS) * lng + lnb


def _softmax_pv_t(blocks, vT_blocks, ones_rows):
    m = functools.reduce(jnp.maximum, [jnp.max(s, axis=0, keepdims=True) for s in blocks])
    d = vT_blocks[0].shape[0]
    acc = functools.reduce(jnp.add, [
        jnp.dot(jnp.concatenate([vT, ones_rows], axis=0), jnp.exp2(s - m).astype(BF16),
                preferred_element_type=F32)
        for s, vT in zip(blocks, vT_blocks)])
    return acc[:d] * (1.0 / acc[d:d + 1])


def _merge_kernel(xprev_ref, aqT_ref, k0_ref, k1_ref, k2_ref, vT0_ref, vT1_ref, vT2_ref, mqT_ref, mk_ref, mvT_ref,
                  ob_ref, g_ref, biasT_ref, wout_ref, lng_ref, lnb_ref, y_ref, mg_ref,
                  *, tiles_per_seq, n_tiles, alpha):
    step = pl.program_id(0)

    @pl.when(step == 0)
    def _():
        mg_ref[...] = jnp.zeros_like(mg_ref)

    t = jnp.minimum(step, n_tiles - 1) % tiles_per_seq
    tile = MERGE_TILE
    aqT = aqT_ref[0].astype(F32)
    mqT = mqT_ref[0].astype(F32)
    row = lax.broadcasted_iota(jnp.int32, aqT.shape, 0)
    keys = [k0_ref[...], k1_ref[...], k2_ref[...]]
    vTs = [vT0_ref[0], vT1_ref[0], vT2_ref[0]]
    mk = mk_ref[...]
    mvT = mvT_ref[...]
    ones_rows = jnp.ones((ONES_ROWS, tile), BF16)
    pen0 = jnp.where(t >= 2, 0.0, NEG_INF).astype(F32)
    pen1 = jnp.where(t >= 1, 0.0, NEG_INF).astype(F32)
    pens = (pen0, pen1, None)
    a_scores, m_scores = [], []
    for h in range(N_HEADS):
        in_head = (row >= h * HEAD_DIM) & (row < (h + 1) * HEAD_DIM)
        qh = jnp.where(in_head, aqT, 0.0).astype(BF16)
        blocks = []
        for b in range(3):
            s = jnp.dot(keys[b], qh, preferred_element_type=F32) + biasT_ref[h, b * tile:(b + 1) * tile, :]
            blocks.append(s if pens[b] is None else s + pens[b])
        a_scores.append(blocks)
        qmh = jnp.where(in_head, mqT, 0.0).astype(BF16)
        m_scores.append([jnp.dot(mk, qmh, preferred_element_type=F32)])
    y_ref[...] = _out_norm(xprev_ref[...], mg_ref[...], wout_ref, lng_ref[...], lnb_ref[...], alpha)
    o_aT, o_mT = [], []
    for h in range(N_HEADS):
        hs = slice(h * HEAD_DIM, (h + 1) * HEAD_DIM)
        o_aT.append(_softmax_pv_t(a_scores[h], [vT[hs] for vT in vTs], ones_rows))
        o_mT.append(_softmax_pv_t(m_scores[h], [mvT[hs]], ones_rows))
    o_a = jnp.concatenate(o_aT, axis=0).T
    o_m = jnp.concatenate(o_mT, axis=0).T
    mixed = jnp.concatenate([o_a, ob_ref[...], o_m], axis=1)
    mg_ref[...] = _gate_mix(mixed, g_ref[...])


def _merge_prompt(x2, aqT, akb, avT, mqT, mkb, mvT, ob, g, biasT, w_out_b, lng, lnb, *, seq, alpha):
    rows = x2.shape[0]
    tile = MERGE_TILE
    tps = seq // tile
    n_tiles = rows // tile
    per_row_tile = ROW_TILE // tile

    cur = lambda s: jnp.minimum(s, n_tiles - 1)
    done = lambda s: jnp.maximum(s - 1, 0)

    def band(d):
        return lambda s: (cur(s) // tps) * tps + jnp.maximum(cur(s) % tps - d, 0)

    cur_spec = lambda w: pl.BlockSpec((tile, w), lambda s: (cur(s), 0))
    done_spec = pl.BlockSpec((tile, D_MODEL), lambda s: (done(s), 0))
    qt_spec = pl.BlockSpec((1, A_WIDTH, tile), lambda s: (cur(s) // per_row_tile, 0, cur(s) % per_row_tile))
    k_specs = [pl.BlockSpec((tile, A_WIDTH), lambda s, f=band(d): (f(s), 0)) for d in (2, 1, 0)]
    vt_specs = [pl.BlockSpec((1, A_WIDTH, tile), lambda s, f=band(d): (f(s), 0, 0)) for d in (2, 1, 0)]
    mem_spec = pl.BlockSpec((N_MEM, M_WIDTH), lambda s: (cur(s) // tps, 0))
    const = lambda shape: pl.BlockSpec(shape, lambda s: (0,) * len(shape))
    return pl.pallas_call(
        functools.partial(_merge_kernel, tiles_per_seq=tps, n_tiles=n_tiles, alpha=alpha),
        out_shape=jax.ShapeDtypeStruct((rows, D_MODEL), F32),
        grid=(n_tiles + 1,),
        in_specs=[done_spec, qt_spec] + k_specs + vt_specs + [
            qt_spec, mem_spec, mem_spec,
            cur_spec(B_WIDTH), cur_spec(MIX_WIDTH),
            const((N_HEADS, 3 * tile, tile)), const((MIX_WIDTH, D_MODEL)), const((1, D_MODEL)), const((1, D_MODEL)),
        ],
        out_specs=done_spec,
        scratch_shapes=[pltpu.VMEM((tile, MIX_WIDTH), BF16)],
        compiler_params=_cparams(("arbitrary",)),
        name="merge_prompt",
    )(x2, aqT, akb, akb, akb, avT, avT, avT, mqT, mkb, mvT, ob, g, biasT, w_out_b, lng, lnb)


def _sample_kernel(x_ref, sq_ref, sk_ref, sv_ref, tq_ref, tk_ref, tv_ref, nq_ref, g_ref,
                   cak_ref, cav_ref, cbk_ref, cbv_ref, cmk_ref, cmv_ref,
                   biasc_ref, biasn_ref, lam_ref, subg_ref, wout_ref, lng_ref, lnb_ref, y_ref,
                   *, lambda_init, alpha):
    masks = _head_masks(A_WIDTH)
    sq = sq_ref[...].astype(F32)
    nq = nq_ref[...].astype(F32)
    a_keys = [cak_ref[0].astype(BF16), sk_ref[...].astype(BF16)]
    a_vals = [cav_ref[0].astype(BF16), sv_ref[...].astype(BF16)]
    mk = cmk_ref[0].astype(BF16)
    mv = cmv_ref[0].astype(BF16)
    rows = sq.shape[0]
    a_scores, m_scores = [], []
    for h in range(N_HEADS):
        qh = jnp.where(masks[h], sq, 0.0).astype(BF16)
        a_scores.append([_dot_nt(qh, a_keys[0]) + biasc_ref[h], _dot_nt(qh, a_keys[1]) + biasn_ref[h]])
        qmh = jnp.where(masks[h], nq, 0.0).astype(BF16)
        m_scores.append([_dot_nt(qmh, mk)])
    past_b = cbv_ref.shape[1] // N_HEADS
    b_scores = []
    for h in range(N_HEADS):
        q = tq_ref[:, h * B_VDIM:(h + 1) * B_VDIM].astype(F32)
        q_maps = [q[:, :HEAD_DIM], pltpu.roll(q, HEAD_DIM, 1)[:, :HEAD_DIM]]
        for mp in range(2):
            q_m = q_maps[mp].astype(BF16)
            k_new = tk_ref[pl.ds(2 * h + mp, rows, stride=2 * N_HEADS), :].astype(BF16)
            b_scores.append([jnp.dot(q_m, cbk_ref[0, h, mp].astype(BF16), preferred_element_type=F32),
                             _dot_nt(q_m, k_new)])

    o_a = jnp.zeros((rows, A_WIDTH), F32)
    o_m = jnp.zeros((rows, M_WIDTH), F32)
    for h in range(N_HEADS):
        o_a = jnp.where(masks[h], _softmax_pv(a_scores[h], a_vals), o_a)
        o_m = jnp.where(masks[h], _softmax_pv(m_scores[h], [mv]), o_m)
    lam = _lambda_value(lam_ref[...], lambda_init)
    subg = subg_ref[...]
    o_b = []
    for h in range(N_HEADS):
        vals = [cbv_ref[0, pl.ds(h, past_b, stride=N_HEADS), :].astype(BF16),
                tv_ref[pl.ds(h, rows, stride=N_HEADS), :].astype(BF16)]
        o_maps = [_softmax_pv(b_scores[2 * h + mp], vals) for mp in range(2)]
        o_b.append(_diff_post(o_maps[0] - lam * o_maps[1], subg, lambda_init))
    mixed = jnp.concatenate([o_a] + o_b + [o_m], axis=1)
    y_ref[...] = _merge_out(x_ref[...], mixed, g_ref[...], wout_ref, lng_ref[...], lnb_ref[...], alpha)


def _sample_step(xs2, sq, sk, sv, tq, tk, tv, nq, g, cak, cav, cbk, cbv, cmk, cmv, biasc, biasn,
                 lam_p, subln_g, w_out_b, lng, lnb, *, n_streams, t_new, lambda_init, alpha):
    past_a = cak.shape[1]
    bk_rows, bv_rows = 2 * N_HEADS, N_HEADS
    past_b = cbv.shape[1] // bv_rows
    row_spec = lambda w: pl.BlockSpec((t_new, w), lambda n: (n, 0))
    nrow_spec = lambda r, w: pl.BlockSpec((t_new * r, w), lambda n: (n, 0))
    cache_spec = lambda r, w: pl.BlockSpec((1, r, w), lambda n: (n, 0, 0))
    const = lambda shape: pl.BlockSpec(shape, lambda n: (0,) * len(shape))
    return pl.pallas_call(
        functools.partial(_sample_kernel, lambda_init=lambda_init, alpha=alpha),
        out_shape=jax.ShapeDtypeStruct((n_streams * t_new, D_MODEL), F32),
        grid=(n_streams,),
        in_specs=[
            row_spec(D_MODEL), row_spec(A_WIDTH), row_spec(A_WIDTH), row_spec(A_WIDTH),
            row_spec(B_WIDTH), nrow_spec(bk_rows, HEAD_DIM), nrow_spec(bv_rows, B_VDIM),
            row_spec(M_WIDTH), row_spec(MIX_WIDTH),
            cache_spec(past_a, A_WIDTH), cache_spec(past_a, A_WIDTH),
            pl.BlockSpec((1, N_HEADS, 2, HEAD_DIM, past_b), lambda n: (n, 0, 0, 0, 0)),
            cache_spec(past_b * bv_rows, B_VDIM),
            cache_spec(N_MEM, M_WIDTH), cache_spec(N_MEM, M_WIDTH),
            const((N_HEADS, t_new, past_a)), const((N_HEADS, t_new, t_new)),
            const((4, HEAD_DIM)), const((1, B_VDIM)),
            const((MIX_WIDTH, D_MODEL)), const((1, D_MODEL)), const((1, D_MODEL)),
        ],
        out_specs=row_spec(D_MODEL),
        compiler_params=_cparams(("arbitrary",)),
        name="sample_step",
    )(xs2, sq, sk, sv, tq, tk, tv, nq, g, cak, cav, cbk, cbv, cmk, cmv, biasc, biasn,
      lam_p, subln_g, w_out_b, lng, lnb)


def _rope_tables(pos):
    half = HEAD_DIM // 2
    inv = ROPE_THETA ** (-jnp.arange(half, dtype=F32) / half)
    ang = pos.astype(F32)[:, None] * inv[None, :]
    cos = jnp.cos(ang)
    sin = jnp.sin(ang)
    return jnp.tile(cos, (1, LANES // half)), jnp.concatenate([-sin, sin, -sin, sin], axis=-1)


def _rel_bias(table, dist):
    return table[:, jnp.clip(dist, -REL_CLIP, REL_CLIP) + REL_CLIP].astype(F32)


def _toeplitz_bias(table, n, width, offset):
    period = -(-(n + width) // LANES) * LANES
    u = jnp.arange(period)
    dist = jnp.where(u < width, offset - u, offset + period - u)
    diagonals = _rel_bias(table, dist)
    flat = jnp.tile(diagonals, (1, n))[:, :n * (period - 1)]
    return flat.reshape(table.shape[0], n, period - 1)[:, :, :width]


def _band_bias_prompt(table):
    n, width = MERGE_TILE, 3 * MERGE_TILE
    bias = _toeplitz_bias(table, n, width, BAND_ROWS)
    i = jnp.arange(n)[:, None]
    j = jnp.arange(width)[None, :]
    qc = i // CHUNK
    kc = j // CHUNK
    visible = (kc >= qc) & (kc <= qc + N_BAND_CHUNKS)
    return jnp.where(visible[None], bias, NEG_INF)


def kernel(x_prompt, x_sample, cache_a_k, cache_a_v, cache_b_k, cache_b_v, cache_mem_k, cache_mem_v, mem_prompt,
           w_in, w_mem_kv, a_rel_bias, diff_lambda, diff_subln_g, w_out, ln_g, ln_b):
    depth = w_in.shape[0]
    assert depth == 1, "single-layer step only"
    n_seq, seq, _ = x_prompt.shape
    n_streams, t_new, _ = x_sample.shape
    past_a = cache_a_k.shape[2]
    past_b = cache_b_k.shape[2]
    assert seq % ROW_TILE == 0 and BAND_ROWS == ROW_TILE == 2 * MERGE_TILE and past_a == BAND_ROWS
    assert n_streams * t_new == ROW_TILE
    layer = 0
    lambda_init = 0.8 - 0.6 * math.exp(-0.3 * layer)
    alpha = (2.0 * depth) ** 0.25

    w_in_b = w_in[layer].astype(BF16)
    w_mem_b = w_mem_kv[layer].astype(BF16)
    w_out_b = w_out[layer].astype(BF16)
    table = a_rel_bias[layer]
    lam_p = diff_lambda[layer]
    subln_g = diff_subln_g[layer].reshape(1, B_VDIM)
    lng = ln_g[layer].reshape(1, D_MODEL)
    lnb = ln_b[layer].reshape(1, D_MODEL)

    rows = n_seq * seq
    x2 = x_prompt.reshape(rows, D_MODEL)
    cos_p, sin_p = _rope_tables(jnp.arange(seq))
    (aqT, akb, avT, akf, avf, bqT, bkb, bvT, bkf, bvf, mqT, g) = _project(
        x2, w_in_b, cos_p, sin_p, prompt=True, rows_per_seq=seq)
    mk, mv, mkb, mvT = _mem_kv(mem_prompt.reshape(n_seq * N_MEM, D_MODEL), w_mem_b)
    ob = _diff_attention(bqT, bkb, bvT, lam_p, subln_g, n_seq=n_seq, seq=seq, lambda_init=lambda_init)
    biasT = jnp.swapaxes(_band_bias_prompt(table), 1, 2) * LOG2E
    y_p = _merge_prompt(x2, aqT, akb, avT, mqT, mkb, mvT, ob, g, biasT, w_out_b, lng, lnb,
                        seq=seq, alpha=alpha)

    xs2 = x_sample.reshape(n_streams * t_new, D_MODEL)
    pos_s = past_b + jnp.arange(t_new)
    cos_s, sin_s = _rope_tables(jnp.tile(pos_s, n_streams))
    sq, sk, sv, tq, tk, tv, nq, sg = _project(xs2, w_in_b, cos_s, sin_s, prompt=False, rows_per_seq=t_new)
    bias_s = _toeplitz_bias(table, t_new, past_a + t_new, past_a)
    biasc, biasn = bias_s[:, :, :past_a], bias_s[:, :, past_a:]
    y_s = _sample_step(
        xs2, sq, sk, sv, tq, tk, tv, nq, sg,
        cache_a_k[layer].reshape(n_streams, past_a, A_WIDTH), cache_a_v[layer].reshape(n_streams, past_a, A_WIDTH),
        jnp.transpose(cache_b_k[layer], (0, 2, 3, 4, 1)),
        cache_b_v[layer].reshape(n_streams, past_b * N_HEADS, B_VDIM),
        cache_mem_k[layer].reshape(n_streams, N_MEM, M_WIDTH), cache_mem_v[layer].reshape(n_streams, N_MEM, M_WIDTH),
        biasc, biasn, lam_p, subln_g, w_out_b, lng, lnb,
        n_streams=n_streams, t_new=t_new, lambda_init=lambda_init, alpha=alpha)

    hd = (N_HEADS, HEAD_DIM)
    return (
        y_p.reshape(n_seq, seq, D_MODEL),
        y_s.reshape(n_streams, t_new, D_MODEL),
        akf.reshape(1, n_seq, BAND_ROWS, *hd),
        avf.reshape(1, n_seq, BAND_ROWS, *hd),
        jnp.transpose(bkf, (0, 4, 1, 2, 3))[None],
        bvf.reshape(1, n_seq, seq, N_HEADS, B_VDIM),
        mk.reshape(1, n_seq, N_MEM, *hd),
        mv.reshape(1, n_seq, N_MEM, *hd),
        sk.reshape(1, n_streams, t_new, *hd),
        sv.reshape(1, n_streams, t_new, *hd),
        tk.reshape(1, n_streams, t_new, N_HEADS, 2, HEAD_DIM),
        tv.reshape(1, n_streams, t_new, N_HEADS, B_VDIM),
    )
```

```python
import functools
import math

import jax
import jax.numpy as jnp
from jax import lax
from jax.experimental import pallas as pl
from jax.experimental.pallas import tpu as pltpu

F32 = jnp.float32
BF16 = jnp.bfloat16

D_MODEL = 1024
CHUNK = 64
N_BAND_CHUNKS = 8
BAND_ROWS = N_BAND_CHUNKS * CHUNK
HEAD_DIM = 64
A_WIDTH = 256
B_WIDTH = 512
B_VDIM = 128
M_WIDTH = 256
N_MEM = 256
N_HEADS = 4
MIX_WIDTH = A_WIDTH + B_WIDTH + M_WIDTH
REL_CLIP = 128
ROPE_THETA = 10000.0
LN_EPS = 1e-5
RMS_EPS = 1e-5
NEG_INF = -1e30
QK_SCALE = HEAD_DIM ** -0.5
LOG2E = math.log2(math.e)

_OFF = {}
_o = 0
for _name, _w in (("aq", A_WIDTH), ("ak", A_WIDTH), ("av", A_WIDTH), ("bq", B_WIDTH), ("bk", B_WIDTH),
                  ("bv", B_WIDTH), ("mq", M_WIDTH), ("gate", MIX_WIDTH)):
    _OFF[_name] = (_o, _o + _w)
    _o += _w
PROJ_TOTAL = _o

LANES = 128
ROW_TILE = 512
KV_TILE = ROW_TILE // 2
ONES_ROWS = 16
MERGE_TILE = 256
VMEM_LIMIT = 56 * 1024 * 1024


def _cparams(sem):
    return pltpu.CompilerParams(dimension_semantics=sem, vmem_limit_bytes=VMEM_LIMIT)


def _rope_slab(x, cos, sin_signed, lo_half):
    left = pltpu.roll(x, LANES - 32, 1)
    right = pltpu.roll(x, 32, 1)
    swapped = jnp.where(lo_half, left, right)
    return x * cos + swapped * sin_signed


def _proj_kernel(x_ref, w_ref, cos_ref, sin_ref, *out_refs, prompt):
    if prompt:
        (aq_ref, akb_ref, avb_ref, akf_ref, avf_ref, bqT_ref, bkb_ref, bvT_ref,
         bkf_ref, bvf_ref, mq_ref, g_ref) = out_refs
    else:
        aq_ref, akf_ref, avf_ref, bq_ref, bkf_ref, bvf_ref, mq_ref, g_ref = out_refs
    xb = x_ref[...].astype(BF16)

    def seg(name):
        lo, hi = _OFF[name]
        return jnp.dot(xb, w_ref[:, lo:hi], preferred_element_type=F32)

    aq = seg("aq")
    ak = seg("ak")
    av = seg("av")
    akf_ref[...] = ak
    avf_ref[...] = av
    if prompt:
        aq_ref[0] = (aq * (QK_SCALE * LOG2E)).T.astype(BF16)
        akb_ref[...] = ak.astype(BF16)
        avT = av.T.astype(BF16)
        for half in range(ROW_TILE // MERGE_TILE):
            avb_ref[half] = avT[:, half * MERGE_TILE:(half + 1) * MERGE_TILE]
    else:
        aq_ref[...] = (aq * QK_SCALE).astype(BF16)

    cos = cos_ref[...]
    sin = sin_ref[...]
    lane = lax.broadcasted_iota(jnp.int32, cos.shape, 1)
    lo_half = (lane % HEAD_DIM) < (HEAD_DIM // 2)
    bq = seg("bq")
    bk = seg("bk")
    bv = seg("bv")
    n_rows = bv.shape[0]
    for c in range(B_WIDTH // LANES):
        sl = slice(c * LANES, (c + 1) * LANES)
        q_c = _rope_slab(bq[:, sl], cos, sin, lo_half)
        k_c = _rope_slab(bk[:, sl], cos, sin, lo_half)
        bvf_ref[pl.ds(c, n_rows, stride=N_HEADS), :] = bv[:, sl]
        if prompt:
            kT_c = k_c.T
            bkf_ref[0, c, 0] = kT_c[:HEAD_DIM]
            bkf_ref[0, c, 1] = kT_c[HEAD_DIM:]
            bkb_ref[:, sl] = k_c.astype(BF16)
            bqT_ref[0, sl, :] = (q_c * (QK_SCALE * LOG2E)).T.astype(BF16)
            vT_c = bv[:, sl].T.astype(BF16)
            for half in range(ROW_TILE // KV_TILE):
                bvT_ref[half, sl, :] = vT_c[:, half * KV_TILE:(half + 1) * KV_TILE]
        else:
            bkf_ref[pl.ds(2 * c, n_rows, stride=2 * N_HEADS), :] = k_c[:, :HEAD_DIM]
            bkf_ref[pl.ds(2 * c + 1, n_rows, stride=2 * N_HEADS), :] = k_c[:, HEAD_DIM:]
            bq_ref[:, sl] = (q_c * QK_SCALE).astype(BF16)

    if prompt:
        mq_ref[0] = (seg("mq") * (QK_SCALE * LOG2E)).T.astype(BF16)
    else:
        mq_ref[...] = (seg("mq") * QK_SCALE).astype(BF16)
    gate = seg("gate")
    g_ref[...] = (gate / (1.0 + jnp.exp(-gate))).astype(BF16)


def _project(x2, w_in_b, cos, sin, *, prompt, rows_per_seq):
    rows = x2.shape[0]
    tm = ROW_TILE
    n_tiles = rows // tm
    pos_tiles = cos.shape[0] // tm
    row_spec = lambda w: pl.BlockSpec((tm, w), lambda i: (i, 0))
    bk_rows, bv_rows = 2 * N_HEADS, N_HEADS
    bkf_shape = jax.ShapeDtypeStruct((rows * bk_rows, HEAD_DIM), F32)
    bvf_shape = jax.ShapeDtypeStruct((rows * bv_rows, B_VDIM), F32)
    bkf_spec = pl.BlockSpec((tm * bk_rows, HEAD_DIM), lambda i: (i, 0))
    bvf_spec = pl.BlockSpec((tm * bv_rows, B_VDIM), lambda i: (i, 0))
    in_specs = [
        row_spec(D_MODEL),
        pl.BlockSpec((D_MODEL, PROJ_TOTAL), lambda i: (0, 0)),
        pl.BlockSpec((tm, LANES), lambda i: (i % pos_tiles, 0)),
        pl.BlockSpec((tm, LANES), lambda i: (i % pos_tiles, 0)),
    ]
    sds = jax.ShapeDtypeStruct
    if prompt:
        tiles_per_seq = rows_per_seq // tm
        n_seq = rows // rows_per_seq
        tail_spec = pl.BlockSpec((tm, A_WIDTH), lambda i: (i // tiles_per_seq, 0))
        t_spec = pl.BlockSpec((1, B_WIDTH, tm), lambda i: (i, 0, 0))
        kv_per_row_tile = tm // KV_TILE
        vt_spec = pl.BlockSpec((kv_per_row_tile, B_WIDTH, KV_TILE), lambda i: (i, 0, 0))
        bkf_shape = jax.ShapeDtypeStruct((n_seq, N_HEADS, 2, HEAD_DIM, rows_per_seq), F32)
        bkf_spec = pl.BlockSpec((1, N_HEADS, 2, HEAD_DIM, tm),
                                lambda i: (i // tiles_per_seq, 0, 0, 0, i % tiles_per_seq))
        qt_spec = pl.BlockSpec((1, A_WIDTH, tm), lambda i: (i, 0, 0))
        merge_per_row_tile = tm // MERGE_TILE
        avt_spec = pl.BlockSpec((merge_per_row_tile, A_WIDTH, MERGE_TILE), lambda i: (i, 0, 0))
        out_shape = (
            sds((n_tiles, A_WIDTH, tm), BF16), sds((rows, A_WIDTH), BF16),
            sds((n_tiles * merge_per_row_tile, A_WIDTH, MERGE_TILE), BF16),
            sds((n_seq * tm, A_WIDTH), F32), sds((n_seq * tm, A_WIDTH), F32),
            sds((n_tiles, B_WIDTH, tm), BF16), sds((rows, B_WIDTH), BF16),
            sds((n_tiles * kv_per_row_tile, B_WIDTH, KV_TILE), BF16),
            bkf_shape, bvf_shape,
            sds((n_tiles, M_WIDTH, tm), BF16), sds((rows, MIX_WIDTH), BF16),
        )
        out_specs = (
            qt_spec, row_spec(A_WIDTH), avt_spec, tail_spec, tail_spec,
            t_spec, row_spec(B_WIDTH), vt_spec, bkf_spec, bvf_spec,
            qt_spec, row_spec(MIX_WIDTH),
        )
    else:
        out_shape = (
            sds((rows, A_WIDTH), BF16), sds((rows, A_WIDTH), F32), sds((rows, A_WIDTH), F32),
            sds((rows, B_WIDTH), BF16), bkf_shape, bvf_shape,
            sds((rows, M_WIDTH), BF16), sds((rows, MIX_WIDTH), BF16),
        )
        out_specs = (
            row_spec(A_WIDTH), row_spec(A_WIDTH), row_spec(A_WIDTH),
            row_spec(B_WIDTH), bkf_spec, bvf_spec,
            row_spec(M_WIDTH), row_spec(MIX_WIDTH),
        )
    return pl.pallas_call(
        functools.partial(_proj_kernel, prompt=prompt),
        out_shape=out_shape,
        grid=(n_tiles,),
        in_specs=in_specs,
        out_specs=out_specs,
        compiler_params=_cparams(("arbitrary",)),
        name="proj_prompt" if prompt else "proj_sample",
    )(x2, w_in_b, cos, sin)


def _mem_kv_kernel(mem_ref, w_ref, mk_ref, mv_ref, mkb_ref, mvT_ref):
    kv = jnp.dot(mem_ref[...].astype(BF16), w_ref[...], preferred_element_type=F32)
    mk_ref[...] = kv[:, :M_WIDTH]
    mv_ref[...] = kv[:, M_WIDTH:]
    mkb_ref[...] = kv[:, :M_WIDTH].astype(BF16)
    mvT_ref[...] = kv[:, M_WIDTH:].T.astype(BF16)


def _mem_kv(mem2, w_b):
    rows = mem2.shape[0]
    tm = N_MEM
    spec = pl.BlockSpec((tm, M_WIDTH), lambda i: (i, 0))
    sds = jax.ShapeDtypeStruct
    return pl.pallas_call(
        _mem_kv_kernel,
        out_shape=(sds((rows, M_WIDTH), F32), sds((rows, M_WIDTH), F32),
                   sds((rows, M_WIDTH), BF16), sds((rows, N_MEM), BF16)),
        grid=(rows // tm,),
        in_specs=[pl.BlockSpec((tm, D_MODEL), lambda i: (i, 0)),
                  pl.BlockSpec((D_MODEL, 2 * M_WIDTH), lambda i: (0, 0))],
        out_specs=(spec, spec, spec, spec),
        compiler_params=_cparams(("arbitrary",)),
        name="mem_kv",
    )(mem2, w_b)


def _lambda_value(lp, lambda_init):
    a = jnp.sum(lp[0:1, :] * lp[1:2, :], axis=1, keepdims=True)
    b = jnp.sum(lp[2:3, :] * lp[3:4, :], axis=1, keepdims=True)
    return jnp.exp(a) - jnp.exp(b) + lambda_init


def _diff_post(o, g, lambda_init):
    ms = jnp.mean(o * o, axis=-1, keepdims=True)
    return o * lax.rsqrt(ms + RMS_EPS) * g * (1.0 - lambda_init)


def _diff_attn_kernel(*refs, tq, tk, lambda_init):
    n_q = refs[0].shape[0]

    def query_tile(t, carry):
        _diff_query_tile(t, *refs, tq=tq, tk=tk, lambda_init=lambda_init)
        return carry

    lax.fori_loop(0, n_q, query_tile, 0)


def _diff_query_tile(t, qT_ref, k_ref, vT_ref, lam_ref, g_ref, o_ref,
                     s0_ref, s1_ref, p0_ref, p1_ref, a0_ref, a1_ref, t0_ref, t1_ref, m_ref, acc_ref,
                     *, tq, tk, lambda_init):
    qT = qT_ref[t].astype(F32)
    row = lax.broadcasted_iota(jnp.int32, qT.shape, 0)
    q_maps = (jnp.where(row < HEAD_DIM, qT, 0.0).astype(BF16),
              jnp.where(row >= HEAD_DIM, qT, 0.0).astype(BF16))
    acc_ref[...] = jnp.zeros_like(acc_ref)
    m_ref[...] = jnp.full(m_ref.shape, NEG_INF, F32)
    s_refs = (s0_ref, s1_ref)
    p_refs = (p0_ref, p1_ref)
    a_refs = (a0_ref, a1_ref)
    t_refs = (t0_ref, t1_ref)
    ones_rows = jnp.ones((ONES_ROWS, tk), BF16)

    def qk(j, buf, visible=None):
        k = k_ref[pl.ds(pl.multiple_of(j * tk, tk), tk), :]
        for mp in range(2):
            s = jnp.dot(k, q_maps[mp], preferred_element_type=F32)
            if visible is not None:
                s = jnp.where(visible, s, NEG_INF)
            s_refs[buf][mp] = s
            t_refs[buf][mp] = jnp.max(s, axis=0, keepdims=True)

    def softmax(buf):
        for mp in range(2):
            m_new = jnp.maximum(m_ref[mp], t_refs[buf][mp])
            a_refs[buf][mp] = jnp.exp2(m_ref[mp] - m_new)
            m_ref[mp] = m_new
            p_refs[buf][mp] = jnp.exp2(s_refs[buf][mp] - m_new).astype(BF16)

    def pv(j, buf):
        vT = jnp.concatenate([vT_ref[j], ones_rows], axis=0)
        for mp in range(2):
            upd = jnp.dot(vT, p_refs[buf][mp], preferred_element_type=F32)
            acc_ref[mp] = a_refs[buf][mp] * acc_ref[mp] + upd

    def diag_visible(half):
        kc = lax.broadcasted_iota(jnp.int32, (tk, tq), 0) // CHUNK + half * (tk // CHUNK)
        qc = lax.broadcasted_iota(jnp.int32, (tk, tq), 1) // CHUNK
        return kc <= qc

    def tick_pair(j, masked):
        pv(j - 2, 0)
        qk(j, 0, diag_visible(0) if masked else None)
        softmax(1)
        pv(j - 1, 1)
        qk(j + 1, 1, diag_visible(1) if masked else None)
        softmax(0)

    qk(0, 0, diag_visible(0) | (t > 0))
    qk(1, 1, diag_visible(1) | (t > 0))
    softmax(0)

    def body(i, c):
        tick_pair(2 + 4 * i, False)
        tick_pair(4 + 4 * i, False)
        return c

    n_plain = jnp.maximum(t - 1, 0)
    lax.fori_loop(0, n_plain // 2, body, 0)

    @pl.when(n_plain % 2 == 1)
    def _():
        tick_pair(2 * t - 2, False)

    @pl.when(t > 0)
    def _():
        tick_pair(2 * t, True)

    pv(2 * t, 0)
    softmax(1)
    pv(2 * t + 1, 1)

    lam = _lambda_value(lam_ref[...], lambda_init)
    inv0 = 1.0 / acc_ref[0, B_VDIM:B_VDIM + 1]
    inv1 = 1.0 / acc_ref[1, B_VDIM:B_VDIM + 1]
    oT = acc_ref[0, :B_VDIM] * inv0 - lam * (acc_ref[1, :B_VDIM] * inv1)
    o_ref[pl.ds(pl.multiple_of(t * tq, tq), tq), :] = _diff_post(oT.T, g_ref[...], lambda_init)


def _diff_attention(bqT, bkb, bvT, lam_p, subln_g, *, n_seq, seq, lambda_init):
    tq, tk = ROW_TILE, KV_TILE
    nq = seq // tq
    nk = seq // tk
    rows = n_seq * seq
    bvT4 = bvT.reshape(n_seq, nk, B_WIDTH, tk)
    return pl.pallas_call(
        functools.partial(_diff_attn_kernel, tq=tq, tk=tk, lambda_init=lambda_init),
        out_shape=jax.ShapeDtypeStruct((rows, B_WIDTH), F32),
        grid=(n_seq, N_HEADS),
        in_specs=[
            pl.BlockSpec((nq, B_VDIM, tq), lambda b, h: (b, h, 0)),
            pl.BlockSpec((seq, B_VDIM), lambda b, h: (b, h)),
            pl.BlockSpec((None, nk, B_VDIM, tk), lambda b, h: (b, 0, h, 0)),
            pl.BlockSpec((4, HEAD_DIM), lambda b, h: (0, 0)),
            pl.BlockSpec((1, B_VDIM), lambda b, h: (0, 0)),
        ],
        out_specs=pl.BlockSpec((seq, B_VDIM), lambda b, h: (b, h)),
        scratch_shapes=[
            pltpu.VMEM((2, tk, tq), F32), pltpu.VMEM((2, tk, tq), F32),
            pltpu.VMEM((2, tk, tq), BF16), pltpu.VMEM((2, tk, tq), BF16),
            pltpu.VMEM((2, 1, tq), F32), pltpu.VMEM((2, 1, tq), F32),
            pltpu.VMEM((2, 1, tq), F32), pltpu.VMEM((2, 1, tq), F32),
            pltpu.VMEM((2, 1, tq), F32),
            pltpu.VMEM((2, B_VDIM + ONES_ROWS, tq), F32),
        ],
        compiler_params=_cparams(("arbitrary", "arbitrary")),
        name="diff_attn",
    )(bqT, bkb, bvT4, lam_p, subln_g)


def _head_masks(width):
    lane = lax.broadcasted_iota(jnp.int32, (1, width), 1)
    return [(lane >= h * HEAD_DIM) & (lane < (h + 1) * HEAD_DIM) for h in range(N_HEADS)]


def _dot_nt(a, b):
    return lax.dot_general(a, b, (((1,), (1,)), ((), ())), preferred_element_type=F32)


def _softmax_parts(blocks):
    m = functools.reduce(jnp.maximum, [jnp.max(s, axis=-1, keepdims=True) for s in blocks])
    ps = [jnp.exp(s - m) for s in blocks]
    l = functools.reduce(jnp.add, [jnp.sum(p, axis=-1, keepdims=True) for p in ps])
    return ps, l


def _attend(q_masked, keys, vals, biases):
    blocks = []
    for kk, bb in zip(keys, biases):
        s = _dot_nt(q_masked, kk)
        blocks.append(s if bb is None else s + bb)
    return _softmax_pv(blocks, vals)


def _softmax_pv(blocks, vals):
    ps, l = _softmax_parts(blocks)
    o = functools.reduce(jnp.add, [jnp.dot(p.astype(BF16), vv, preferred_element_type=F32)
                                   for p, vv in zip(ps, vals)])
    return o * (1.0 / l)


def _gate_mix(mixed, g):
    return (mixed * g.astype(F32)).astype(BF16)


def _merge_out(x, mixed, g, w_out_ref, lng, lnb, alpha):
    return _out_norm(x, _gate_mix(mixed, g), w_out_ref, lng, lnb, alpha)


def _out_norm(x, mg, w_out_ref, lng, lnb, alpha):
    y = jnp.dot(mg, w_out_ref[...], preferred_element_type=F32)
    z = alpha * x + y
    mu = jnp.mean(z, axis=-1, keepdims=True)
    zc = z - mu
    var = jnp.mean(zc * zc, axis=-1, keepdims=True)
    return zc * lax.rsqrt(var + LN_EPS) * lng + lnb


def _softmax_pv_t(blocks, vT_blocks, ones_rows):
    m = functools.reduce(jnp.maximum, [jnp.max(s, axis=0, keepdims=True) for s in blocks])
    d = vT_blocks[0].shape[0]
    acc = functools.reduce(jnp.add, [
        jnp.dot(jnp.concatenate([vT, ones_rows], axis=0), jnp.exp2(s - m).astype(BF16),
                preferred_element_type=F32)
        for s, vT in zip(blocks, vT_blocks)])
    return acc[:d] * (1.0 / acc[d:d + 1])


def _merge_kernel(xprev_ref, aqT_ref, k0_ref, k1_ref, k2_ref, vT0_ref, vT1_ref, vT2_ref, mqT_ref, mk_ref, mvT_ref,
                  ob_ref, g_ref, biasT_ref, wout_ref, lng_ref, lnb_ref, y_ref, mg_ref,
                  *, tiles_per_seq, n_tiles, alpha):
    step = pl.program_id(0)

    @pl.when(step == 0)
    def _():
        mg_ref[...] = jnp.zeros_like(mg_ref)

    t = jnp.minimum(step, n_tiles - 1) % tiles_per_seq
    tile = MERGE_TILE
    aqT = aqT_ref[0].astype(F32)
    mqT = mqT_ref[0].astype(F32)
    row = lax.broadcasted_iota(jnp.int32, aqT.shape, 0)
    keys = [k0_ref[...], k1_ref[...], k2_ref[...]]
    vTs = [vT0_ref[0], vT1_ref[0], vT2_ref[0]]
    mk = mk_ref[...]
    mvT = mvT_ref[...]
    ones_rows = jnp.ones((ONES_ROWS, tile), BF16)
    pen0 = jnp.where(t >= 2, 0.0, NEG_INF).astype(F32)
    pen1 = jnp.where(t >= 1, 0.0, NEG_INF).astype(F32)
    pens = (pen0, pen1, None)
    a_scores, m_scores = [], []
    for h in range(N_HEADS):
        in_head = (row >= h * HEAD_DIM) & (row < (h + 1) * HEAD_DIM)
        qh = jnp.where(in_head, aqT, 0.0).astype(BF16)
        blocks = []
        for b in range(3):
            s = jnp.dot(keys[b], qh, preferred_element_type=F32) + biasT_ref[h, b * tile:(b + 1) * tile, :]
            blocks.append(s if pens[b] is None else s + pens[b])
        a_scores.append(blocks)
        qmh = jnp.where(in_head, mqT, 0.0).astype(BF16)
        m_scores.append([jnp.dot(mk, qmh, preferred_element_type=F32)])
    y_ref[...] = _out_norm(xprev_ref[...], mg_ref[...], wout_ref, lng_ref[...], lnb_ref[...], alpha)
    o_aT, o_mT = [], []
    for h in range(N_HEADS):
        hs = slice(h * HEAD_DIM, (h + 1) * HEAD_DIM)
        o_aT.append(_softmax_pv_t(a_scores[h], [vT[hs] for vT in vTs], ones_rows))
        o_mT.append(_softmax_pv_t(m_scores[h], [mvT[hs]], ones_rows))
    o_a = jnp.concatenate(o_aT, axis=0).T
    o_m = jnp.concatenate(o_mT, axis=0).T
    mixed = jnp.concatenate([o_a, ob_ref[...], o_m], axis=1)
    mg_ref[...] = _gate_mix(mixed, g_ref[...])


def _merge_prompt(x2, aqT, akb, avT, mqT, mkb, mvT, ob, g, biasT, w_out_b, lng, lnb, *, seq, alpha):
    rows = x2.shape[0]
    tile = MERGE_TILE
    tps = seq // tile
    n_tiles = rows // tile
    per_row_tile = ROW_TILE // tile

    cur = lambda s: jnp.minimum(s, n_tiles - 1)
    done = lambda s: jnp.maximum(s - 1, 0)

    def band(d):
        return lambda s: (cur(s) // tps) * tps + jnp.maximum(cur(s) % tps - d, 0)

    cur_spec = lambda w: pl.BlockSpec((tile, w), lambda s: (cur(s), 0))
    done_spec = pl.BlockSpec((tile, D_MODEL), lambda s: (done(s), 0))
    qt_spec = pl.BlockSpec((1, A_WIDTH, tile), lambda s: (cur(s) // per_row_tile, 0, cur(s) % per_row_tile))
    k_specs = [pl.BlockSpec((tile, A_WIDTH), lambda s, f=band(d): (f(s), 0)) for d in (2, 1, 0)]
    vt_specs = [pl.BlockSpec((1, A_WIDTH, tile), lambda s, f=band(d): (f(s), 0, 0)) for d in (2, 1, 0)]
    mem_spec = pl.BlockSpec((N_MEM, M_WIDTH), lambda s: (cur(s) // tps, 0))
    const = lambda shape: pl.BlockSpec(shape, lambda s: (0,) * len(shape))
    return pl.pallas_call(
        functools.partial(_merge_kernel, tiles_per_seq=tps, n_tiles=n_tiles, alpha=alpha),
        out_shape=jax.ShapeDtypeStruct((rows, D_MODEL), F32),
        grid=(n_tiles + 1,),
        in_specs=[done_spec, qt_spec] + k_specs + vt_specs + [
            qt_spec, mem_spec, mem_spec,
            cur_spec(B_WIDTH), cur_spec(MIX_WIDTH),
            const((N_HEADS, 3 * tile, tile)), const((MIX_WIDTH, D_MODEL)), const((1, D_MODEL)), const((1, D_MODEL)),
        ],
        out_specs=done_spec,
        scratch_shapes=[pltpu.VMEM((tile, MIX_WIDTH), BF16)],
        compiler_params=_cparams(("arbitrary",)),
        name="merge_prompt",
    )(x2, aqT, akb, akb, akb, avT, avT, avT, mqT, mkb, mvT, ob, g, biasT, w_out_b, lng, lnb)


def _sample_kernel(x_ref, sq_ref, sk_ref, sv_ref, tq_ref, tk_ref, tv_ref, nq_ref, g_ref,
                   cak_ref, cav_ref, cbk_ref, cbv_ref, cmk_ref, cmv_ref,
                   biasc_ref, biasn_ref, lam_ref, subg_ref, wout_ref, lng_ref, lnb_ref, y_ref,
                   *, lambda_init, alpha):
    masks = _head_masks(A_WIDTH)
    sq = sq_ref[...].astype(F32)
    nq = nq_ref[...].astype(F32)
    a_keys = [cak_ref[0].astype(BF16), sk_ref[...].astype(BF16)]
    a_vals = [cav_ref[0].astype(BF16), sv_ref[...].astype(BF16)]
    mk = cmk_ref[0].astype(BF16)
    mv = cmv_ref[0].astype(BF16)
    rows = sq.shape[0]
    a_scores, m_scores = [], []
    for h in range(N_HEADS):
        qh = jnp.where(masks[h], sq, 0.0).astype(BF16)
        a_scores.append([_dot_nt(qh, a_keys[0]) + biasc_ref[h], _dot_nt(qh, a_keys[1]) + biasn_ref[h]])
        qmh = jnp.where(masks[h], nq, 0.0).astype(BF16)
        m_scores.append([_dot_nt(qmh, mk)])
    past_b = cbv_ref.shape[1] // N_HEADS
    b_scores = []
    for h in range(N_HEADS):
        q = tq_ref[:, h * B_VDIM:(h + 1) * B_VDIM].astype(F32)
        q_maps = [q[:, :HEAD_DIM], pltpu.roll(q, HEAD_DIM, 1)[:, :HEAD_DIM]]
        for mp in range(2):
            q_m = q_maps[mp].astype(BF16)
            k_new = tk_ref[pl.ds(2 * h + mp, rows, stride=2 * N_HEADS), :].astype(BF16)
            b_scores.append([jnp.dot(q_m, cbk_ref[0, h, mp].astype(BF16), preferred_element_type=F32),
                             _dot_nt(q_m, k_new)])

    o_a = jnp.zeros((rows, A_WIDTH), F32)
    o_m = jnp.zeros((rows, M_WIDTH), F32)
    for h in range(N_HEADS):
        o_a = jnp.where(masks[h], _softmax_pv(a_scores[h], a_vals), o_a)
        o_m = jnp.where(masks[h], _softmax_pv(m_scores[h], [mv]), o_m)
    lam = _lambda_value(lam_ref[...], lambda_init)
    subg = subg_ref[...]
    o_b = []
    for h in range(N_HEADS):
        vals = [cbv_ref[0, pl.ds(h, past_b, stride=N_HEADS), :].astype(BF16),
                tv_ref[pl.ds(h, rows, stride=N_HEADS), :].astype(BF16)]
        o_maps = [_softmax_pv(b_scores[2 * h + mp], vals) for mp in range(2)]
        o_b.append(_diff_post(o_maps[0] - lam * o_maps[1], subg, lambda_init))
    mixed = jnp.concatenate([o_a] + o_b + [o_m], axis=1)
    y_ref[...] = _merge_out(x_ref[...], mixed, g_ref[...], wout_ref, lng_ref[...], lnb_ref[...], alpha)


def _sample_step(xs2, sq, sk, sv, tq, tk, tv, nq, g, cak, cav, cbk, cbv, cmk, cmv, biasc, biasn,
                 lam_p, subln_g, w_out_b, lng, lnb, *, n_streams, t_new, lambda_init, alpha):
    past_a = cak.shape[1]
    bk_rows, bv_rows = 2 * N_HEADS, N_HEADS
    past_b = cbv.shape[1] // bv_rows
    row_spec = lambda w: pl.BlockSpec((t_new, w), lambda n: (n, 0))
    nrow_spec = lambda r, w: pl.BlockSpec((t_new * r, w), lambda n: (n, 0))
    cache_spec = lambda r, w: pl.BlockSpec((1, r, w), lambda n: (n, 0, 0))
    const = lambda shape: pl.BlockSpec(shape, lambda n: (0,) * len(shape))
    return pl.pallas_call(
        functools.partial(_sample_kernel, lambda_init=lambda_init, alpha=alpha),
        out_shape=jax.ShapeDtypeStruct((n_streams * t_new, D_MODEL), F32),
        grid=(n_streams,),
        in_specs=[
            row_spec(D_MODEL), row_spec(A_WIDTH), row_spec(A_WIDTH), row_spec(A_WIDTH),
            row_spec(B_WIDTH), nrow_spec(bk_rows, HEAD_DIM), nrow_spec(bv_rows, B_VDIM),
            row_spec(M_WIDTH), row_spec(MIX_WIDTH),
            cache_spec(past_a, A_WIDTH), cache_spec(past_a, A_WIDTH),
            pl.BlockSpec((1, N_HEADS, 2, HEAD_DIM, past_b), lambda n: (n, 0, 0, 0, 0)),
            cache_spec(past_b * bv_rows, B_VDIM),
            cache_spec(N_MEM, M_WIDTH), cache_spec(N_MEM, M_WIDTH),
            const((N_HEADS, t_new, past_a)), const((N_HEADS, t_new, t_new)),
            const((4, HEAD_DIM)), const((1, B_VDIM)),
            const((MIX_WIDTH, D_MODEL)), const((1, D_MODEL)), const((1, D_MODEL)),
        ],
        out_specs=row_spec(D_MODEL),
        compiler_params=_cparams(("arbitrary",)),
        name="sample_step",
    )(xs2, sq, sk, sv, tq, tk, tv, nq, g, cak, cav, cbk, cbv, cmk, cmv, biasc, biasn,
      lam_p, subln_g, w_out_b, lng, lnb)


def _rope_tables(pos):
    half = HEAD_DIM // 2
    inv = ROPE_THETA ** (-jnp.arange(half, dtype=F32) / half)
    ang = pos.astype(F32)[:, None] * inv[None, :]
    cos = jnp.cos(ang)
    sin = jnp.sin(ang)
    return jnp.tile(cos, (1, LANES // half)), jnp.concatenate([-sin, sin, -sin, sin], axis=-1)


def _rel_bias(table, dist):
    return table[:, jnp.clip(dist, -REL_CLIP, REL_CLIP) + REL_CLIP].astype(F32)


def _toeplitz_bias(table, n, width, offset):
    period = -(-(n + width) // LANES) * LANES
    u = jnp.arange(period)
    dist = jnp.where(u < width, offset - u, offset + period - u)
    diagonals = _rel_bias(table, dist)
    flat = jnp.tile(diagonals, (1, n))[:, :n * (period - 1)]
    return flat.reshape(table.shape[0], n, period - 1)[:, :, :width]


def _band_bias_prompt(table):
    n, width = MERGE_TILE, 3 * MERGE_TILE
    bias = _toeplitz_bias(table, n, width, BAND_ROWS)
    i = jnp.arange(n)[:, None]
    j = jnp.arange(width)[None, :]
    qc = i // CHUNK
    kc = j // CHUNK
    visible = (kc >= qc) & (kc <= qc + N_BAND_CHUNKS)
    return jnp.where(visible[None], bias, NEG_INF)


def kernel(x_prompt, x_sample, cache_a_k, cache_a_v, cache_b_k, cache_b_v, cache_mem_k, cache_mem_v, mem_prompt,
           w_in, w_mem_kv, a_rel_bias, diff_lambda, diff_subln_g, w_out, ln_g, ln_b):
    depth = w_in.shape[0]
    assert depth == 1, "single-layer step only"
    n_seq, seq, _ = x_prompt.shape
    n_streams, t_new, _ = x_sample.shape
    past_a = cache_a_k.shape[2]
    past_b = cache_b_k.shape[2]
    assert seq % ROW_TILE == 0 and BAND_ROWS == ROW_TILE == 2 * MERGE_TILE and past_a == BAND_ROWS
    assert n_streams * t_new == ROW_TILE
    layer = 0
    lambda_init = 0.8 - 0.6 * math.exp(-0.3 * layer)
    alpha = (2.0 * depth) ** 0.25

    w_in_b = w_in[layer].astype(BF16)
    w_mem_b = w_mem_kv[layer].astype(BF16)
    w_out_b = w_out[layer].astype(BF16)
    table = a_rel_bias[layer]
    lam_p = diff_lambda[layer]
    subln_g = diff_subln_g[layer].reshape(1, B_VDIM)
    lng = ln_g[layer].reshape(1, D_MODEL)
    lnb = ln_b[layer].reshape(1, D_MODEL)

    rows = n_seq * seq
    x2 = x_prompt.reshape(rows, D_MODEL)
    cos_p, sin_p = _rope_tables(jnp.arange(seq))
    (aqT, akb, avT, akf, avf, bqT, bkb, bvT, bkf, bvf, mqT, g) = _project(
        x2, w_in_b, cos_p, sin_p, prompt=True, rows_per_seq=seq)
    mk, mv, mkb, mvT = _mem_kv(mem_prompt.reshape(n_seq * N_MEM, D_MODEL), w_mem_b)
    ob = _diff_attention(bqT, bkb, bvT, lam_p, subln_g, n_seq=n_seq, seq=seq, lambda_init=lambda_init)
    biasT = jnp.swapaxes(_band_bias_prompt(table), 1, 2) * LOG2E
    y_p = _merge_prompt(x2, aqT, akb, avT, mqT, mkb, mvT, ob, g, biasT, w_out_b, lng, lnb,
                        seq=seq, alpha=alpha)

    xs2 = x_sample.reshape(n_streams * t_new, D_MODEL)
    pos_s = past_b + jnp.arange(t_new)
    cos_s, sin_s = _rope_tables(jnp.tile(pos_s, n_streams))
    sq, sk, sv, tq, tk, tv, nq, sg = _project(xs2, w_in_b, cos_s, sin_s, prompt=False, rows_per_seq=t_new)
    bias_s = _toeplitz_bias(table, t_new, past_a + t_new, past_a)
    biasc, biasn = bias_s[:, :, :past_a], bias_s[:, :, past_a:]
    y_s = _sample_step(
        xs2, sq, sk, sv, tq, tk, tv, nq, sg,
        cache_a_k[layer].reshape(n_streams, past_a, A_WIDTH), cache_a_v[layer].reshape(n_streams, past_a, A_WIDTH),
        jnp.transpose(cache_b_k[layer], (0, 2, 3, 4, 1)),
        cache_b_v[layer].reshape(n_streams, past_b * N_HEADS, B_VDIM),
        cache_mem_k[layer].reshape(n_streams, N_MEM, M_WIDTH), cache_mem_v[layer].reshape(n_streams, N_MEM, M_WIDTH),
        biasc, biasn, lam_p, subln_g, w_out_b, lng, lnb,
        n_streams=n_streams, t_new=t_new, lambda_init=lambda_init, alpha=alpha)

    hd = (N_HEADS, HEAD_DIM)
    return (
        y_p.reshape(n_seq, seq, D_MODEL),
        y_s.reshape(n_streams, t_new, D_MODEL),
        akf.reshape(1, n_seq, BAND_ROWS, *hd),
        avf.reshape(1, n_seq, BAND_ROWS, *hd),
        jnp.transpose(bkf, (0, 4, 1, 2, 3))[None],
        bvf.reshape(1, n_seq, seq, N_HEADS, B_VDIM),
        mk.reshape(1, n_seq, N_MEM, *hd),
        mv.reshape(1, n_seq, N_MEM, *hd),
        sk.reshape(1, n_streams, t_new, *hd),
        sv.reshape(1, n_streams, t_new, *hd),
        tk.reshape(1, n_streams, t_new, N_HEADS, 2, HEAD_DIM),
        tv.reshape(1, n_streams, t_new, N_HEADS, B_VDIM),
    )
```

```python
import functools
import math

import jax
import jax.numpy as jnp
from jax import lax
from jax.experimental import pallas as pl
from jax.experimental.pallas import tpu as pltpu

F32 = jnp.float32
BF16 = jnp.bfloat16

D_MODEL = 1024
CHUNK = 64
N_BAND_CHUNKS = 8
BAND_ROWS = N_BAND_CHUNKS * CHUNK
HEAD_DIM = 64
A_WIDTH = 256
B_WIDTH = 512
B_VDIM = 128
M_WIDTH = 256
N_MEM = 256
N_HEADS = 4
MIX_WIDTH = A_WIDTH + B_WIDTH + M_WIDTH
REL_CLIP = 128
ROPE_THETA = 10000.0
LN_EPS = 1e-5
RMS_EPS = 1e-5
NEG_INF = -1e30
QK_SCALE = HEAD_DIM ** -0.5
LOG2E = math.log2(math.e)

_OFF = {}
_o = 0
for _name, _w in (("aq", A_WIDTH), ("ak", A_WIDTH), ("av", A_WIDTH), ("bq", B_WIDTH), ("bk", B_WIDTH),
                  ("bv", B_WIDTH), ("mq", M_WIDTH), ("gate", MIX_WIDTH)):
    _OFF[_name] = (_o, _o + _w)
    _o += _w
PROJ_TOTAL = _o

LANES = 128
ROW_TILE = 512
KV_TILE = ROW_TILE // 2
ONES_ROWS = 16
MERGE_TILE = 256
VMEM_LIMIT = 56 * 1024 * 1024


def _cparams(sem):
    return pltpu.CompilerParams(dimension_semantics=sem, vmem_limit_bytes=VMEM_LIMIT)


def _rope_slab(x, cos, sin_signed, lo_half):
    left = pltpu.roll(x, LANES - 32, 1)
    right = pltpu.roll(x, 32, 1)
    swapped = jnp.where(lo_half, left, right)
    return x * cos + swapped * sin_signed


def _proj_kernel(x_ref, w_ref, cos_ref, sin_ref, *out_refs, prompt):
    if prompt:
        (aq_ref, akb_ref, avb_ref, akf_ref, avf_ref, bqT_ref, bkb_ref, bvT_ref,
         bkf_ref, bvf_ref, mq_ref, g_ref) = out_refs
    else:
        aq_ref, akf_ref, avf_ref, bq_ref, bkf_ref, bvf_ref, mq_ref, g_ref = out_refs
    xb = x_ref[...].astype(BF16)

    def seg(name):
        lo, hi = _OFF[name]
        return jnp.dot(xb, w_ref[:, lo:hi], preferred_element_type=F32)

    aq = seg("aq")
    ak = seg("ak")
    av = seg("av")
    akf_ref[...] = ak
    avf_ref[...] = av
    if prompt:
        aq_ref[0] = (aq * (QK_SCALE * LOG2E)).T.astype(BF16)
        akb_ref[...] = ak.astype(BF16)
        avT = av.T.astype(BF16)
        for half in range(ROW_TILE // MERGE_TILE):
            avb_ref[half] = avT[:, half * MERGE_TILE:(half + 1) * MERGE_TILE]
    else:
        aq_ref[...] = (aq * QK_SCALE).astype(BF16)

    cos = cos_ref[...]
    sin = sin_ref[...]
    lane = lax.broadcasted_iota(jnp.int32, cos.shape, 1)
    lo_half = (lane % HEAD_DIM) < (HEAD_DIM // 2)
    bq = seg("bq")
    bk = seg("bk")
    bv = seg("bv")
    n_rows = bv.shape[0]
    for c in range(B_WIDTH // LANES):
        sl = slice(c * LANES, (c + 1) * LANES)
        q_c = _rope_slab(bq[:, sl], cos, sin, lo_half)
        k_c = _rope_slab(bk[:, sl], cos, sin, lo_half)
        bvf_ref[pl.ds(c, n_rows, stride=N_HEADS), :] = bv[:, sl]
        if prompt:
            kT_c = k_c.T
            bkf_ref[0, c, 0] = kT_c[:HEAD_DIM]
            bkf_ref[0, c, 1] = kT_c[HEAD_DIM:]
            bkb_ref[:, sl] = k_c.astype(BF16)
            bqT_ref[0, sl, :] = (q_c * (QK_SCALE * LOG2E)).T.astype(BF16)
            vT_c = bv[:, sl].T.astype(BF16)
            for half in range(ROW_TILE // KV_TILE):
                bvT_ref[half, sl, :] = vT_c[:, half * KV_TILE:(half + 1) * KV_TILE]
        else:
            bkf_ref[pl.ds(2 * c, n_rows, stride=2 * N_HEADS), :] = k_c[:, :HEAD_DIM]
            bkf_ref[pl.ds(2 * c + 1, n_rows, stride=2 * N_HEADS), :] = k_c[:, HEAD_DIM:]
            bq_ref[:, sl] = (q_c * QK_SCALE).astype(BF16)

    if prompt:
        mq_ref[0] = (seg("mq") * (QK_SCALE * LOG2E)).T.astype(BF16)
    else:
        mq_ref[...] = (seg("mq") * QK_SCALE).astype(BF16)
    gate = seg("gate")
    g_ref[...] = (gate / (1.0 + jnp.exp(-gate))).astype(BF16)


def _project(x2, w_in_b, cos, sin, *, prompt, rows_per_seq):
    rows = x2.shape[0]
    tm = ROW_TILE
    n_tiles = rows // tm
    pos_tiles = cos.shape[0] // tm
    row_spec = lambda w: pl.BlockSpec((tm, w), lambda i: (i, 0))
    bk_rows, bv_rows = 2 * N_HEADS, N_HEADS
    bkf_shape = jax.ShapeDtypeStruct((rows * bk_rows, HEAD_DIM), F32)
    bvf_shape = jax.ShapeDtypeStruct((rows * bv_rows, B_VDIM), F32)
    bkf_spec = pl.BlockSpec((tm * bk_rows, HEAD_DIM), lambda i: (i, 0))
    bvf_spec = pl.BlockSpec((tm * bv_rows, B_VDIM), lambda i: (i, 0))
    in_specs = [
        row_spec(D_MODEL),
        pl.BlockSpec((D_MODEL, PROJ_TOTAL), lambda i: (0, 0)),
        pl.BlockSpec((tm, LANES), lambda i: (i % pos_tiles, 0)),
        pl.BlockSpec((tm, LANES), lambda i: (i % pos_tiles, 0)),
    ]
    sds = jax.ShapeDtypeStruct
    if prompt:
        tiles_per_seq = rows_per_seq // tm
        n_seq = rows // rows_per_seq
        tail_spec = pl.BlockSpec((tm, A_WIDTH), lambda i: (i // tiles_per_seq, 0))
        t_spec = pl.BlockSpec((1, B_WIDTH, tm), lambda i: (i, 0, 0))
        kv_per_row_tile = tm // KV_TILE
        vt_spec = pl.BlockSpec((kv_per_row_tile, B_WIDTH, KV_TILE), lambda i: (i, 0, 0))
        bkf_shape = jax.ShapeDtypeStruct((n_seq, N_HEADS, 2, HEAD_DIM, rows_per_seq), F32)
        bkf_spec = pl.BlockSpec((1, N_HEADS, 2, HEAD_DIM, tm),
                                lambda i: (i // tiles_per_seq, 0, 0, 0, i % tiles_per_seq))
        qt_spec = pl.BlockSpec((1, A_WIDTH, tm), lambda i: (i, 0, 0))
        merge_per_row_tile = tm // MERGE_TILE
        avt_spec = pl.BlockSpec((merge_per_row_tile, A_WIDTH, MERGE_TILE), lambda i: (i, 0, 0))
        out_shape = (
            sds((n_tiles, A_WIDTH, tm), BF16), sds((rows, A_WIDTH), BF16),
            sds((n_tiles * merge_per_row_tile, A_WIDTH, MERGE_TILE), BF16),
            sds((n_seq * tm, A_WIDTH), F32), sds((n_seq * tm, A_WIDTH), F32),
            sds((n_tiles, B_WIDTH, tm), BF16), sds((rows, B_WIDTH), BF16),
            sds((n_tiles * kv_per_row_tile, B_WIDTH, KV_TILE), BF16),
            bkf_shape, bvf_shape,
            sds((n_tiles, M_WIDTH, tm), BF16), sds((rows, MIX_WIDTH), BF16),
        )
        out_specs = (
            qt_spec, row_spec(A_WIDTH), avt_spec, tail_spec, tail_spec,
            t_spec, row_spec(B_WIDTH), vt_spec, bkf_spec, bvf_spec,
            qt_spec, row_spec(MIX_WIDTH),
        )
    else:
        out_shape = (
            sds((rows, A_WIDTH), BF16), sds((rows, A_WIDTH), F32), sds((rows, A_WIDTH), F32),
            sds((rows, B_WIDTH), BF16), bkf_shape, bvf_shape,
            sds((rows, M_WIDTH), BF16), sds((rows, MIX_WIDTH), BF16),
        )
        out_specs = (
            row_spec(A_WIDTH), row_spec(A_WIDTH), row_spec(A_WIDTH),
            row_spec(B_WIDTH), bkf_spec, bvf_spec,
            row_spec(M_WIDTH), row_spec(MIX_WIDTH),
        )
    return pl.pallas_call(
        functools.partial(_proj_kernel, prompt=prompt),
        out_shape=out_shape,
        grid=(n_tiles,),
        in_specs=in_specs,
        out_specs=out_specs,
        compiler_params=_cparams(("arbitrary",)),
        name="proj_prompt" if prompt else "proj_sample",
    )(x2, w_in_b, cos, sin)


def _mem_kv_kernel(mem_ref, w_ref, mk_ref, mv_ref, mkb_ref, mvT_ref):
    kv = jnp.dot(mem_ref[...].astype(BF16), w_ref[...], preferred_element_type=F32)
    mk_ref[...] = kv[:, :M_WIDTH]
    mv_ref[...] = kv[:, M_WIDTH:]
    mkb_ref[...] = kv[:, :M_WIDTH].astype(BF16)
    mvT_ref[...] = kv[:, M_WIDTH:].T.astype(BF16)


def _mem_kv(mem2, w_b):
    rows = mem2.shape[0]
    tm = N_MEM
    spec = pl.BlockSpec((tm, M_WIDTH), lambda i: (i, 0))
    sds = jax.ShapeDtypeStruct
    return pl.pallas_call(
        _mem_kv_kernel,
        out_shape=(sds((rows, M_WIDTH), F32), sds((rows, M_WIDTH), F32),
                   sds((rows, M_WIDTH), BF16), sds((rows, N_MEM), BF16)),
        grid=(rows // tm,),
        in_specs=[pl.BlockSpec((tm, D_MODEL), lambda i: (i, 0)),
                  pl.BlockSpec((D_MODEL, 2 * M_WIDTH), lambda i: (0, 0))],
        out_specs=(spec, spec, spec, spec),
        compiler_params=_cparams(("arbitrary",)),
        name="mem_kv",
    )(mem2, w_b)


def _lambda_value(lp, lambda_init):
    a = jnp.sum(lp[0:1, :] * lp[1:2, :], axis=1, keepdims=True)
    b = jnp.sum(lp[2:3, :] * lp[3:4, :], axis=1, keepdims=True)
    return jnp.exp(a) - jnp.exp(b) + lambda_init


def _diff_post(o, g, lambda_init):
    ms = jnp.mean(o * o, axis=-1, keepdims=True)
    return o * lax.rsqrt(ms + RMS_EPS) * g * (1.0 - lambda_init)


def _diff_attn_kernel(*refs, tq, tk, lambda_init):
    n_q = refs[0].shape[0]

    def query_tile(t, carry):
        _diff_query_tile(t, *refs, tq=tq, tk=tk, lambda_init=lambda_init)
        return carry

    lax.fori_loop(0, n_q, query_tile, 0)


def _diff_query_tile(t, qT_ref, k_ref, vT_ref, lam_ref, g_ref, o_ref,
                     s0_ref, s1_ref, p0_ref, p1_ref, a0_ref, a1_ref, t0_ref, t1_ref, m_ref, acc_ref,
                     *, tq, tk, lambda_init):
    qT = qT_ref[t].astype(F32)
    row = lax.broadcasted_iota(jnp.int32, qT.shape, 0)
    q_maps = (jnp.where(row < HEAD_DIM, qT, 0.0).astype(BF16),
              jnp.where(row >= HEAD_DIM, qT, 0.0).astype(BF16))
    acc_ref[...] = jnp.zeros_like(acc_ref)
    m_ref[...] = jnp.full(m_ref.shape, NEG_INF, F32)
    s_refs = (s0_ref, s1_ref)
    p_refs = (p0_ref, p1_ref)
    a_refs = (a0_ref, a1_ref)
    t_refs = (t0_ref, t1_ref)
    ones_rows = jnp.ones((ONES_ROWS, tk), BF16)

    every = slice(None)
    upper = slice(tq // 2, tq)

    def qk(j, buf, visible=None, cols=every):
        k = k_ref[pl.ds(pl.multiple_of(j * tk, tk), tk), :]
        for mp in range(2):
            s = jnp.dot(k, q_maps[mp][:, cols], preferred_element_type=F32)
            if visible is not None:
                s = jnp.where(visible[:, cols], s, NEG_INF)
            s_refs[buf][mp, :, cols] = s
            t_refs[buf][mp, :, cols] = jnp.max(s, axis=0, keepdims=True)

    def softmax(buf, cols=every):
        for mp in range(2):
            m_old = m_ref[mp, :, cols]
            m_new = jnp.maximum(m_old, t_refs[buf][mp, :, cols])
            a_refs[buf][mp, :, cols] = jnp.exp2(m_old - m_new)
            m_ref[mp, :, cols] = m_new
            p_refs[buf][mp, :, cols] = jnp.exp2(s_refs[buf][mp, :, cols] - m_new).astype(BF16)

    def pv(j, buf, cols=every):
        vT = jnp.concatenate([vT_ref[j], ones_rows], axis=0)
        for mp in range(2):
            upd = jnp.dot(vT, p_refs[buf][mp, :, cols], preferred_element_type=F32)
            acc_ref[mp, :, cols] = a_refs[buf][mp, :, cols] * acc_ref[mp, :, cols] + upd

    def diag_visible(half):
        kc = lax.broadcasted_iota(jnp.int32, (tk, tq), 0) // CHUNK + half * (tk // CHUNK)
        qc = lax.broadcasted_iota(jnp.int32, (tk, tq), 1) // CHUNK
        return kc <= qc

    def tick_pair(j, masked):
        pv(j - 2, 0)
        qk(j, 0, diag_visible(0) if masked else None)
        softmax(1)
        pv(j - 1, 1)
        if masked:
            qk(j + 1, 1, diag_visible(1), cols=upper)
        else:
            qk(j + 1, 1)
        softmax(0)

    qk(0, 0, diag_visible(0) | (t > 0))
    qk(1, 1, diag_visible(1) | (t > 0))
    softmax(0)

    def body(i, c):
        tick_pair(2 + 4 * i, False)
        tick_pair(4 + 4 * i, False)
        return c

    n_plain = jnp.maximum(t - 1, 0)
    lax.fori_loop(0, n_plain // 2, body, 0)

    @pl.when(n_plain % 2 == 1)
    def _():
        tick_pair(2 * t - 2, False)

    @pl.when(t > 0)
    def _():
        tick_pair(2 * t, True)

    pv(2 * t, 0)
    softmax(1, cols=upper)
    pv(2 * t + 1, 1, cols=upper)

    lam = _lambda_value(lam_ref[...], lambda_init)
    inv0 = 1.0 / acc_ref[0, B_VDIM:B_VDIM + 1]
    inv1 = 1.0 / acc_ref[1, B_VDIM:B_VDIM + 1]
    oT = acc_ref[0, :B_VDIM] * inv0 - lam * (acc_ref[1, :B_VDIM] * inv1)
    o_ref[pl.ds(pl.multiple_of(t * tq, tq), tq), :] = _diff_post(oT.T, g_ref[...], lambda_init)


def _diff_attention(bqT, bkb, bvT, lam_p, subln_g, *, n_seq, seq, lambda_init):
    tq, tk = ROW_TILE, KV_TILE
    nq = seq // tq
    nk = seq // tk
    rows = n_seq * seq
    bvT4 = bvT.reshape(n_seq, nk, B_WIDTH, tk)
    return pl.pallas_call(
        functools.partial(_diff_attn_kernel, tq=tq, tk=tk, lambda_init=lambda_init),
        out_shape=jax.ShapeDtypeStruct((rows, B_WIDTH), F32),
        grid=(n_seq, N_HEADS),
        in_specs=[
            pl.BlockSpec((nq, B_VDIM, tq), lambda b, h: (b, h, 0)),
            pl.BlockSpec((seq, B_VDIM), lambda b, h: (b, h)),
            pl.BlockSpec((None, nk, B_VDIM, tk), lambda b, h: (b, 0, h, 0)),
            pl.BlockSpec((4, HEAD_DIM), lambda b, h: (0, 0)),
            pl.BlockSpec((1, B_VDIM), lambda b, h: (0, 0)),
        ],
        out_specs=pl.BlockSpec((seq, B_VDIM), lambda b, h: (b, h)),
        scratch_shapes=[
            pltpu.VMEM((2, tk, tq), F32), pltpu.VMEM((2, tk, tq), F32),
            pltpu.VMEM((2, tk, tq), BF16), pltpu.VMEM((2, tk, tq), BF16),
            pltpu.VMEM((2, 1, tq), F32), pltpu.VMEM((2, 1, tq), F32),
            pltpu.VMEM((2, 1, tq), F32), pltpu.VMEM((2, 1, tq), F32),
            pltpu.VMEM((2, 1, tq), F32),
            pltpu.VMEM((2, B_VDIM + ONES_ROWS, tq), F32),
        ],
        compiler_params=_cparams(("arbitrary", "arbitrary")),
        name="diff_attn",
    )(bqT, bkb, bvT4, lam_p, subln_g)


def _head_masks(width):
    lane = lax.broadcasted_iota(jnp.int32, (1, width), 1)
    return [(lane >= h * HEAD_DIM) & (lane < (h + 1) * HEAD_DIM) for h in range(N_HEADS)]


def _dot_nt(a, b):
    return lax.dot_general(a, b, (((1,), (1,)), ((), ())), preferred_element_type=F32)


def _softmax_parts(blocks):
    m = functools.reduce(jnp.maximum, [jnp.max(s, axis=-1, keepdims=True) for s in blocks])
    ps = [jnp.exp(s - m) for s in blocks]
    l = functools.reduce(jnp.add, [jnp.sum(p, axis=-1, keepdims=True) for p in ps])
    return ps, l


def _attend(q_masked, keys, vals, biases):
    blocks = []
    for kk, bb in zip(keys, biases):
        s = _dot_nt(q_masked, kk)
        blocks.append(s if bb is None else s + bb)
    return _softmax_pv(blocks, vals)


def _softmax_pv(blocks, vals):
    ps, l = _softmax_parts(blocks)
    o = functools.reduce(jnp.add, [jnp.dot(p.astype(BF16), vv, preferred_element_type=F32)
                                   for p, vv in zip(ps, vals)])
    return o * (1.0 / l)


def _gate_mix(mixed, g):
    return (mixed * g.astype(F32)).astype(BF16)


def _merge_out(x, mixed, g, w_out_ref, lng, lnb, alpha):
    return _out_norm(x, _gate_mix(mixed, g), w_out_ref, lng, lnb, alpha)


def _out_norm(x, mg, w_out_ref, lng, lnb, alpha):
    y = jnp.dot(mg, w_out_ref[...], preferred_element_type=F32)
    z = alpha * x + y
    mu = jnp.mean(z, axis=-1, keepdims=True)
    zc = z - mu
    var = jnp.mean(zc * zc, axis=-1, keepdims=True)
    return zc * lax.rsqrt(var + LN_EPS) * lng + lnb


def _softmax_pv_t(blocks, vT_blocks, ones_rows):
    m = functools.reduce(jnp.maximum, [jnp.max(s, axis=0, keepdims=True) for s in blocks])
    d = vT_blocks[0].shape[0]
    acc = functools.reduce(jnp.add, [
        jnp.dot(jnp.concatenate([vT, ones_rows], axis=0), jnp.exp2(s - m).astype(BF16),
                preferred_element_type=F32)
        for s, vT in zip(blocks, vT_blocks)])
    return acc[:d] * (1.0 / acc[d:d + 1])


def _merge_kernel(xprev_ref, aqT_ref, k0_ref, k1_ref, k2_ref, vT0_ref, vT1_ref, vT2_ref, mqT_ref, mk_ref, mvT_ref,
                  ob_ref, g_ref, biasT_ref, wout_ref, lng_ref, lnb_ref, y_ref, mg_ref,
                  *, tiles_per_seq, n_tiles, alpha):
    step = pl.program_id(0)

    @pl.when(step == 0)
    def _():
        mg_ref[...] = jnp.zeros_like(mg_ref)

    t = jnp.minimum(step, n_tiles - 1) % tiles_per_seq
    tile = MERGE_TILE
    aqT = aqT_ref[0].astype(F32)
    mqT = mqT_ref[0].astype(F32)
    row = lax.broadcasted_iota(jnp.int32, aqT.shape, 0)
    keys = [k0_ref[...], k1_ref[...], k2_ref[...]]
    vTs = [vT0_ref[0], vT1_ref[0], vT2_ref[0]]
    mk = mk_ref[...]
    mvT = mvT_ref[...]
    ones_rows = jnp.ones((ONES_ROWS, tile), BF16)
    pen0 = jnp.where(t >= 2, 0.0, NEG_INF).astype(F32)
    pen1 = jnp.where(t >= 1, 0.0, NEG_INF).astype(F32)
    pens = (pen0, pen1, None)
    a_scores, m_scores = [], []
    for h in range(N_HEADS):
        in_head = (row >= h * HEAD_DIM) & (row < (h + 1) * HEAD_DIM)
        qh = jnp.where(in_head, aqT, 0.0).astype(BF16)
        blocks = []
        for b in range(3):
            s = jnp.dot(keys[b], qh, preferred_element_type=F32) + biasT_ref[h, b * tile:(b + 1) * tile, :]
            blocks.append(s if pens[b] is None else s + pens[b])
        a_scores.append(blocks)
        qmh = jnp.where(in_head, mqT, 0.0).astype(BF16)
        m_scores.append([jnp.dot(mk, qmh, preferred_element_type=F32)])
    y_ref[...] = _out_norm(xprev_ref[...], mg_ref[...], wout_ref, lng_ref[...], lnb_ref[...], alpha)
    o_aT, o_mT = [], []
    for h in range(N_HEADS):
        hs = slice(h * HEAD_DIM, (h + 1) * HEAD_DIM)
        o_aT.append(_softmax_pv_t(a_scores[h], [vT[hs] for vT in vTs], ones_rows))
        o_mT.append(_softmax_pv_t(m_scores[h], [mvT[hs]], ones_rows))
    o_a = jnp.concatenate(o_aT, axis=0).T
    o_m = jnp.concatenate(o_mT, axis=0).T
    mixed = jnp.concatenate([o_a, ob_ref[...], o_m], axis=1)
    mg_ref[...] = _gate_mix(mixed, g_ref[...])


def _merge_prompt(x2, aqT, akb, avT, mqT, mkb, mvT, ob, g, biasT, w_out_b, lng, lnb, *, seq, alpha):
    rows = x2.shape[0]
    tile = MERGE_TILE
    tps = seq // tile
    n_tiles = rows // tile
    per_row_tile = ROW_TILE // tile

    cur = lambda s: jnp.minimum(s, n_tiles - 1)
    done = lambda s: jnp.maximum(s - 1, 0)

    def band(d):
        return lambda s: (cur(s) // tps) * tps + jnp.maximum(cur(s) % tps - d, 0)

    cur_spec = lambda w: pl.BlockSpec((tile, w), lambda s: (cur(s), 0))
    done_spec = pl.BlockSpec((tile, D_MODEL), lambda s: (done(s), 0))
    qt_spec = pl.BlockSpec((1, A_WIDTH, tile), lambda s: (cur(s) // per_row_tile, 0, cur(s) % per_row_tile))
    k_specs = [pl.BlockSpec((tile, A_WIDTH), lambda s, f=band(d): (f(s), 0)) for d in (2, 1, 0)]
    vt_specs = [pl.BlockSpec((1, A_WIDTH, tile), lambda s, f=band(d): (f(s), 0, 0)) for d in (2, 1, 0)]
    mem_spec = pl.BlockSpec((N_MEM, M_WIDTH), lambda s: (cur(s) // tps, 0))
    const = lambda shape: pl.BlockSpec(shape, lambda s: (0,) * len(shape))
    return pl.pallas_call(
        functools.partial(_merge_kernel, tiles_per_seq=tps, n_tiles=n_tiles, alpha=alpha),
        out_shape=jax.ShapeDtypeStruct((rows, D_MODEL), F32),
        grid=(n_tiles + 1,),
        in_specs=[done_spec, qt_spec] + k_specs + vt_specs + [
            qt_spec, mem_spec, mem_spec,
            cur_spec(B_WIDTH), cur_spec(MIX_WIDTH),
            const((N_HEADS, 3 * tile, tile)), const((MIX_WIDTH, D_MODEL)), const((1, D_MODEL)), const((1, D_MODEL)),
        ],
        out_specs=done_spec,
        scratch_shapes=[pltpu.VMEM((tile, MIX_WIDTH), BF16)],
        compiler_params=_cparams(("arbitrary",)),
        name="merge_prompt",
    )(x2, aqT, akb, akb, akb, avT, avT, avT, mqT, mkb, mvT, ob, g, biasT, w_out_b, lng, lnb)


def _sample_kernel(x_ref, sq_ref, sk_ref, sv_ref, tq_ref, tk_ref, tv_ref, nq_ref, g_ref,
                   cak_ref, cav_ref, cbk_ref, cbv_ref, cmk_ref, cmv_ref,
                   biasc_ref, biasn_ref, lam_ref, subg_ref, wout_ref, lng_ref, lnb_ref, y_ref,
                   *, lambda_init, alpha):
    masks = _head_masks(A_WIDTH)
    sq = sq_ref[...].astype(F32)
    nq = nq_ref[...].astype(F32)
    a_keys = [cak_ref[0].astype(BF16), sk_ref[...].astype(BF16)]
    a_vals = [cav_ref[0].astype(BF16), sv_ref[...].astype(BF16)]
    mk = cmk_ref[0].astype(BF16)
    mv = cmv_ref[0].astype(BF16)
    rows = sq.shape[0]
    a_scores, m_scores = [], []
    for h in range(N_HEADS):
        qh = jnp.where(masks[h], sq, 0.0).astype(BF16)
        a_scores.append([_dot_nt(qh, a_keys[0]) + biasc_ref[h], _dot_nt(qh, a_keys[1]) + biasn_ref[h]])
        qmh = jnp.where(masks[h], nq, 0.0).astype(BF16)
        m_scores.append([_dot_nt(qmh, mk)])
    past_b = cbv_ref.shape[1] // N_HEADS
    b_scores = []
    for h in range(N_HEADS):
        q = tq_ref[:, h * B_VDIM:(h + 1) * B_VDIM].astype(F32)
        q_maps = [q[:, :HEAD_DIM], pltpu.roll(q, HEAD_DIM, 1)[:, :HEAD_DIM]]
        for mp in range(2):
            q_m = q_maps[mp].astype(BF16)
            k_new = tk_ref[pl.ds(2 * h + mp, rows, stride=2 * N_HEADS), :].astype(BF16)
            b_scores.append([jnp.dot(q_m, cbk_ref[0, h, mp].astype(BF16), preferred_element_type=F32),
                             _dot_nt(q_m, k_new)])

    o_a = jnp.zeros((rows, A_WIDTH), F32)
    o_m = jnp.zeros((rows, M_WIDTH), F32)
    for h in range(N_HEADS):
        o_a = jnp.where(masks[h], _softmax_pv(a_scores[h], a_vals), o_a)
        o_m = jnp.where(masks[h], _softmax_pv(m_scores[h], [mv]), o_m)
    lam = _lambda_value(lam_ref[...], lambda_init)
    subg = subg_ref[...]
    o_b = []
    for h in range(N_HEADS):
        vals = [cbv_ref[0, pl.ds(h, past_b, stride=N_HEADS), :].astype(BF16),
                tv_ref[pl.ds(h, rows, stride=N_HEADS), :].astype(BF16)]
        o_maps = [_softmax_pv(b_scores[2 * h + mp], vals) for mp in range(2)]
        o_b.append(_diff_post(o_maps[0] - lam * o_maps[1], subg, lambda_init))
    mixed = jnp.concatenate([o_a] + o_b + [o_m], axis=1)
    y_ref[...] = _merge_out(x_ref[...], mixed, g_ref[...], wout_ref, lng_ref[...], lnb_ref[...], alpha)


def _sample_step(xs2, sq, sk, sv, tq, tk, tv, nq, g, cak, cav, cbk, cbv, cmk, cmv, biasc, biasn,
                 lam_p, subln_g, w_out_b, lng, lnb, *, n_streams, t_new, lambda_init, alpha):
    past_a = cak.shape[1]
    bk_rows, bv_rows = 2 * N_HEADS, N_HEADS
    past_b = cbv.shape[1] // bv_rows
    row_spec = lambda w: pl.BlockSpec((t_new, w), lambda n: (n, 0))
    nrow_spec = lambda r, w: pl.BlockSpec((t_new * r, w), lambda n: (n, 0))
    cache_spec = lambda r, w: pl.BlockSpec((1, r, w), lambda n: (n, 0, 0))
    const = lambda shape: pl.BlockSpec(shape, lambda n: (0,) * len(shape))
    return pl.pallas_call(
        functools.partial(_sample_kernel, lambda_init=lambda_init, alpha=alpha),
        out_shape=jax.ShapeDtypeStruct((n_streams * t_new, D_MODEL), F32),
        grid=(n_streams,),
        in_specs=[
            row_spec(D_MODEL), row_spec(A_WIDTH), row_spec(A_WIDTH), row_spec(A_WIDTH),
            row_spec(B_WIDTH), nrow_spec(bk_rows, HEAD_DIM), nrow_spec(bv_rows, B_VDIM),
            row_spec(M_WIDTH), row_spec(MIX_WIDTH),
            cache_spec(past_a, A_WIDTH), cache_spec(past_a, A_WIDTH),
            pl.BlockSpec((1, N_HEADS, 2, HEAD_DIM, past_b), lambda n: (n, 0, 0, 0, 0)),
            cache_spec(past_b * bv_rows, B_VDIM),
            cache_spec(N_MEM, M_WIDTH), cache_spec(N_MEM, M_WIDTH),
            const((N_HEADS, t_new, past_a)), const((N_HEADS, t_new, t_new)),
            const((4, HEAD_DIM)), const((1, B_VDIM)),
            const((MIX_WIDTH, D_MODEL)), const((1, D_MODEL)), const((1, D_MODEL)),
        ],
        out_specs=row_spec(D_MODEL),
        compiler_params=_cparams(("arbitrary",)),
        name="sample_step",
    )(xs2, sq, sk, sv, tq, tk, tv, nq, g, cak, cav, cbk, cbv, cmk, cmv, biasc, biasn,
      lam_p, subln_g, w_out_b, lng, lnb)


def _rope_tables(pos):
    half = HEAD_DIM // 2
    inv = ROPE_THETA ** (-jnp.arange(half, dtype=F32) / half)
    ang = pos.astype(F32)[:, None] * inv[None, :]
    cos = jnp.cos(ang)
    sin = jnp.sin(ang)
    return jnp.tile(cos, (1, LANES // half)), jnp.concatenate([-sin, sin, -sin, sin], axis=-1)


def _rel_bias(table, dist):
    return table[:, jnp.clip(dist, -REL_CLIP, REL_CLIP) + REL_CLIP].astype(F32)


def _toeplitz_bias(table, n, width, offset):
    period = -(-(n + width) // LANES) * LANES
    u = jnp.arange(period)
    dist = jnp.where(u < width, offset - u, offset + period - u)
    diagonals = _rel_bias(table, dist)
    flat = jnp.tile(diagonals, (1, n))[:, :n * (period - 1)]
    return flat.reshape(table.shape[0], n, period - 1)[:, :, :width]


def _band_bias_prompt(table):
    n, width = MERGE_TILE, 3 * MERGE_TILE
    bias = _toeplitz_bias(table, n, width, BAND_ROWS)
    i = jnp.arange(n)[:, None]
    j = jnp.arange(width)[None, :]
    qc = i // CHUNK
    kc = j // CHUNK
    visible = (kc >= qc) & (kc <= qc + N_BAND_CHUNKS)
    return jnp.where(visible[None], bias, NEG_INF)


def kernel(x_prompt, x_sample, cache_a_k, cache_a_v, cache_b_k, cache_b_v, cache_mem_k, cache_mem_v, mem_prompt,
           w_in, w_mem_kv, a_rel_bias, diff_lambda, diff_subln_g, w_out, ln_g, ln_b):
    depth = w_in.shape[0]
    assert depth == 1, "single-layer step only"
    n_seq, seq, _ = x_prompt.shape
    n_streams, t_new, _ = x_sample.shape
    past_a = cache_a_k.shape[2]
    past_b = cache_b_k.shape[2]
    assert seq % ROW_TILE == 0 and BAND_ROWS == ROW_TILE == 2 * MERGE_TILE and past_a == BAND_ROWS
    assert n_streams * t_new == ROW_TILE
    layer = 0
    lambda_init = 0.8 - 0.6 * math.exp(-0.3 * layer)
    alpha = (2.0 * depth) ** 0.25

    w_in_b = w_in[layer].astype(BF16)
    w_mem_b = w_mem_kv[layer].astype(BF16)
    w_out_b = w_out[layer].astype(BF16)
    table = a_rel_bias[layer]
    lam_p = diff_lambda[layer]
    subln_g = diff_subln_g[layer].reshape(1, B_VDIM)
    lng = ln_g[layer].reshape(1, D_MODEL)
    lnb = ln_b[layer].reshape(1, D_MODEL)

    rows = n_seq * seq
    x2 = x_prompt.reshape(rows, D_MODEL)
    cos_p, sin_p = _rope_tables(jnp.arange(seq))
    (aqT, akb, avT, akf, avf, bqT, bkb, bvT, bkf, bvf, mqT, g) = _project(
        x2, w_in_b, cos_p, sin_p, prompt=True, rows_per_seq=seq)
    mk, mv, mkb, mvT = _mem_kv(mem_prompt.reshape(n_seq * N_MEM, D_MODEL), w_mem_b)
    ob = _diff_attention(bqT, bkb, bvT, lam_p, subln_g, n_seq=n_seq, seq=seq, lambda_init=lambda_init)
    biasT = jnp.swapaxes(_band_bias_prompt(table), 1, 2) * LOG2E
    y_p = _merge_prompt(x2, aqT, akb, avT, mqT, mkb, mvT, ob, g, biasT, w_out_b, lng, lnb,
                        seq=seq, alpha=alpha)

    xs2 = x_sample.reshape(n_streams * t_new, D_MODEL)
    pos_s = past_b + jnp.arange(t_new)
    cos_s, sin_s = _rope_tables(jnp.tile(pos_s, n_streams))
    sq, sk, sv, tq, tk, tv, nq, sg = _project(xs2, w_in_b, cos_s, sin_s, prompt=False, rows_per_seq=t_new)
    bias_s = _toeplitz_bias(table, t_new, past_a + t_new, past_a)
    biasc, biasn = bias_s[:, :, :past_a], bias_s[:, :, past_a:]
    y_s = _sample_step(
        xs2, sq, sk, sv, tq, tk, tv, nq, sg,
        cache_a_k[layer].reshape(n_streams, past_a, A_WIDTH), cache_a_v[layer].reshape(n_streams, past_a, A_WIDTH),
        jnp.transpose(cache_b_k[layer], (0, 2, 3, 4, 1)),
        cache_b_v[layer].reshape(n_streams, past_b * N_HEADS, B_VDIM),
        cache_mem_k[layer].reshape(n_streams, N_MEM, M_WIDTH), cache_mem_v[layer].reshape(n_streams, N_MEM, M_WIDTH),
        biasc, biasn, lam_p, subln_g, w_out_b, lng, lnb,
        n_streams=n_streams, t_new=t_new, lambda_init=lambda_init, alpha=alpha)

    hd = (N_HEADS, HEAD_DIM)
    return (
        y_p.reshape(n_seq, seq, D_MODEL),
        y_s.reshape(n_streams, t_new, D_MODEL),
        akf.reshape(1, n_seq, BAND_ROWS, *hd),
        avf.reshape(1, n_seq, BAND_ROWS, *hd),
        jnp.transpose(bkf, (0, 4, 1, 2, 3))[None],
        bvf.reshape(1, n_seq, seq, N_HEADS, B_VDIM),
        mk.reshape(1, n_seq, N_MEM, *hd),
        mv.reshape(1, n_seq, N_MEM, *hd),
        sk.reshape(1, n_streams, t_new, *hd),
        sv.reshape(1, n_streams, t_new, *hd),
        tk.reshape(1, n_streams, t_new, N_HEADS, 2, HEAD_DIM),
        tv.reshape(1, n_streams, t_new, N_HEADS, B_VDIM),
    )
```

```python
import functools
import math

import jax
import jax.numpy as jnp
from jax import lax
from jax.experimental import pallas as pl
from jax.experimental.pallas import tpu as pltpu

F32 = jnp.float32
BF16 = jnp.bfloat16

D_MODEL = 1024
CHUNK = 64
N_BAND_CHUNKS = 8
BAND_ROWS = N_BAND_CHUNKS * CHUNK
HEAD_DIM = 64
A_WIDTH = 256
B_WIDTH = 512
B_VDIM = 128
M_WIDTH = 256
N_MEM = 256
N_HEADS = 4
MIX_WIDTH = A_WIDTH + B_WIDTH + M_WIDTH
REL_CLIP = 128
ROPE_THETA = 10000.0
LN_EPS = 1e-5
RMS_EPS = 1e-5
NEG_INF = -1e30
QK_SCALE = HEAD_DIM ** -0.5
LOG2E = math.log2(math.e)

_OFF = {}
_o = 0
for _name, _w in (("aq", A_WIDTH), ("ak", A_WIDTH), ("av", A_WIDTH), ("bq", B_WIDTH), ("bk", B_WIDTH),
                  ("bv", B_WIDTH), ("mq", M_WIDTH), ("gate", MIX_WIDTH)):
    _OFF[_name] = (_o, _o + _w)
    _o += _w
PROJ_TOTAL = _o

LANES = 128
ROW_TILE = 512
KV_TILE = ROW_TILE // 2
ONES_ROWS = 16
MERGE_TILE = 256
VMEM_LIMIT = 56 * 1024 * 1024


def _cparams(sem):
    return pltpu.CompilerParams(dimension_semantics=sem, vmem_limit_bytes=VMEM_LIMIT)


def _rope_slab(x, cos, sin_signed, lo_half):
    left = pltpu.roll(x, LANES - 32, 1)
    right = pltpu.roll(x, 32, 1)
    swapped = jnp.where(lo_half, left, right)
    return x * cos + swapped * sin_signed


def _proj_kernel(x_ref, w_ref, cos_ref, sin_ref, *out_refs, prompt):
    if prompt:
        (aq_ref, akb_ref, avb_ref, akf_ref, avf_ref, bqT_ref, bkb_ref, bvT_ref,
         bkf_ref, bvf_ref, mq_ref, g_ref) = out_refs
    else:
        aq_ref, akf_ref, avf_ref, bq_ref, bkf_ref, bvf_ref, mq_ref, g_ref = out_refs
    xb = x_ref[...].astype(BF16)

    def seg(name):
        lo, hi = _OFF[name]
        return jnp.dot(xb, w_ref[:, lo:hi], preferred_element_type=F32)

    aq = seg("aq")
    ak = seg("ak")
    av = seg("av")
    akf_ref[...] = ak
    avf_ref[...] = av
    if prompt:
        aq_ref[0] = (aq * (QK_SCALE * LOG2E)).T.astype(BF16)
        akb_ref[...] = ak.astype(BF16)
        avT = av.T.astype(BF16)
        for half in range(ROW_TILE // MERGE_TILE):
            avb_ref[half] = avT[:, half * MERGE_TILE:(half + 1) * MERGE_TILE]
    else:
        aq_ref[...] = (aq * QK_SCALE).astype(BF16)

    cos = cos_ref[...]
    sin = sin_ref[...]
    lane = lax.broadcasted_iota(jnp.int32, cos.shape, 1)
    lo_half = (lane % HEAD_DIM) < (HEAD_DIM // 2)
    bq = seg("bq")
    bk = seg("bk")
    bv = seg("bv")
    n_rows = bv.shape[0]
    for c in range(B_WIDTH // LANES):
        sl = slice(c * LANES, (c + 1) * LANES)
        q_c = _rope_slab(bq[:, sl], cos, sin, lo_half)
        k_c = _rope_slab(bk[:, sl], cos, sin, lo_half)
        bvf_ref[pl.ds(c, n_rows, stride=N_HEADS), :] = bv[:, sl]
        if prompt:
            kT_c = k_c.T
            bkf_ref[0, c, 0] = kT_c[:HEAD_DIM]
            bkf_ref[0, c, 1] = kT_c[HEAD_DIM:]
            bkb_ref[:, sl] = k_c.astype(BF16)
            bqT_ref[0, sl, :] = (q_c * (QK_SCALE * LOG2E)).T.astype(BF16)
            vT_c = bv[:, sl].T.astype(BF16)
            for half in range(ROW_TILE // KV_TILE):
                bvT_ref[half, sl, :] = vT_c[:, half * KV_TILE:(half + 1) * KV_TILE]
        else:
            bkf_ref[pl.ds(2 * c, n_rows, stride=2 * N_HEADS), :] = k_c[:, :HEAD_DIM]
            bkf_ref[pl.ds(2 * c + 1, n_rows, stride=2 * N_HEADS), :] = k_c[:, HEAD_DIM:]
            bq_ref[:, sl] = (q_c * QK_SCALE).astype(BF16)

    if prompt:
        mq_ref[0] = (seg("mq") * (QK_SCALE * LOG2E)).T.astype(BF16)
    else:
        mq_ref[...] = (seg("mq") * QK_SCALE).astype(BF16)
    gate = seg("gate")
    g_ref[...] = (gate / (1.0 + jnp.exp(-gate))).astype(BF16)


def _project(x2, w_in_b, cos, sin, *, prompt, rows_per_seq):
    rows = x2.shape[0]
    tm = ROW_TILE
    n_tiles = rows // tm
    pos_tiles = cos.shape[0] // tm
    row_spec = lambda w: pl.BlockSpec((tm, w), lambda i: (i, 0))
    bk_rows, bv_rows = 2 * N_HEADS, N_HEADS
    bkf_shape = jax.ShapeDtypeStruct((rows * bk_rows, HEAD_DIM), F32)
    bvf_shape = jax.ShapeDtypeStruct((rows * bv_rows, B_VDIM), F32)
    bkf_spec = pl.BlockSpec((tm * bk_rows, HEAD_DIM), lambda i: (i, 0))
    bvf_spec = pl.BlockSpec((tm * bv_rows, B_VDIM), lambda i: (i, 0))
    in_specs = [
        row_spec(D_MODEL),
        pl.BlockSpec((D_MODEL, PROJ_TOTAL), lambda i: (0, 0)),
        pl.BlockSpec((tm, LANES), lambda i: (i % pos_tiles, 0)),
        pl.BlockSpec((tm, LANES), lambda i: (i % pos_tiles, 0)),
    ]
    sds = jax.ShapeDtypeStruct
    if prompt:
        tiles_per_seq = rows_per_seq // tm
        n_seq = rows // rows_per_seq
        tail_spec = pl.BlockSpec((tm, A_WIDTH), lambda i: (i // tiles_per_seq, 0))
        t_spec = pl.BlockSpec((1, B_WIDTH, tm), lambda i: (i, 0, 0))
        kv_per_row_tile = tm // KV_TILE
        vt_spec = pl.BlockSpec((kv_per_row_tile, B_WIDTH, KV_TILE), lambda i: (i, 0, 0))
        bkf_shape = jax.ShapeDtypeStruct((n_seq, N_HEADS, 2, HEAD_DIM, rows_per_seq), F32)
        bkf_spec = pl.BlockSpec((1, N_HEADS, 2, HEAD_DIM, tm),
                                lambda i: (i // tiles_per_seq, 0, 0, 0, i % tiles_per_seq))
        qt_spec = pl.BlockSpec((1, A_WIDTH, tm), lambda i: (i, 0, 0))
        merge_per_row_tile = tm // MERGE_TILE
        avt_spec = pl.BlockSpec((merge_per_row_tile, A_WIDTH, MERGE_TILE), lambda i: (i, 0, 0))
        out_shape = (
            sds((n_tiles, A_WIDTH, tm), BF16), sds((rows, A_WIDTH), BF16),
            sds((n_tiles * merge_per_row_tile, A_WIDTH, MERGE_TILE), BF16),
            sds((n_seq * tm, A_WIDTH), F32), sds((n_seq * tm, A_WIDTH), F32),
            sds((n_tiles, B_WIDTH, tm), BF16), sds((rows, B_WIDTH), BF16),
            sds((n_tiles * kv_per_row_tile, B_WIDTH, KV_TILE), BF16),
            bkf_shape, bvf_shape,
            sds((n_tiles, M_WIDTH, tm), BF16), sds((rows, MIX_WIDTH), BF16),
        )
        out_specs = (
            qt_spec, row_spec(A_WIDTH), avt_spec, tail_spec, tail_spec,
            t_spec, row_spec(B_WIDTH), vt_spec, bkf_spec, bvf_spec,
            qt_spec, row_spec(MIX_WIDTH),
        )
    else:
        out_shape = (
            sds((rows, A_WIDTH), BF16), sds((rows, A_WIDTH), F32), sds((rows, A_WIDTH), F32),
            sds((rows, B_WIDTH), BF16), bkf_shape, bvf_shape,
            sds((rows, M_WIDTH), BF16), sds((rows, MIX_WIDTH), BF16),
        )
        out_specs = (
            row_spec(A_WIDTH), row_spec(A_WIDTH), row_spec(A_WIDTH),
            row_spec(B_WIDTH), bkf_spec, bvf_spec,
            row_spec(M_WIDTH), row_spec(MIX_WIDTH),
        )
    return pl.pallas_call(
        functools.partial(_proj_kernel, prompt=prompt),
        out_shape=out_shape,
        grid=(n_tiles,),
        in_specs=in_specs,
        out_specs=out_specs,
        compiler_params=_cparams(("arbitrary",)),
        name="proj_prompt" if prompt else "proj_sample",
    )(x2, w_in_b, cos, sin)


def _mem_kv_kernel(mem_ref, w_ref, mk_ref, mv_ref, mkb_ref, mvT_ref):
    kv = jnp.dot(mem_ref[...].astype(BF16), w_ref[...], preferred_element_type=F32)
    mk_ref[...] = kv[:, :M_WIDTH]
    mv_ref[...] = kv[:, M_WIDTH:]
    mkb_ref[...] = kv[:, :M_WIDTH].astype(BF16)
    mvT_ref[...] = kv[:, M_WIDTH:].T.astype(BF16)


def _mem_kv(mem2, w_b):
    rows = mem2.shape[0]
    tm = N_MEM
    spec = pl.BlockSpec((tm, M_WIDTH), lambda i: (i, 0))
    sds = jax.ShapeDtypeStruct
    return pl.pallas_call(
        _mem_kv_kernel,
        out_shape=(sds((rows, M_WIDTH), F32), sds((rows, M_WIDTH), F32),
                   sds((rows, M_WIDTH), BF16), sds((rows, N_MEM), BF16)),
        grid=(rows // tm,),
        in_specs=[pl.BlockSpec((tm, D_MODEL), lambda i: (i, 0)),
                  pl.BlockSpec((D_MODEL, 2 * M_WIDTH), lambda i: (0, 0))],
        out_specs=(spec, spec, spec, spec),
        compiler_params=_cparams(("arbitrary",)),
        name="mem_kv",
    )(mem2, w_b)


def _lambda_value(lp, lambda_init):
    a = jnp.sum(lp[0:1, :] * lp[1:2, :], axis=1, keepdims=True)
    b = jnp.sum(lp[2:3, :] * lp[3:4, :], axis=1, keepdims=True)
    return jnp.exp(a) - jnp.exp(b) + lambda_init


def _diff_post(o, g, lambda_init):
    ms = jnp.mean(o * o, axis=-1, keepdims=True)
    return o * lax.rsqrt(ms + RMS_EPS) * g * (1.0 - lambda_init)


def _diff_attn_kernel(qT_ref, k_ref, vT_ref, lam_ref, g_ref, o_ref,
                      s0_ref, s1_ref, p0_ref, p1_ref, a0_ref, a1_ref, t0_ref, t1_ref, m_ref, acc_ref,
                      *, tq, tk, lambda_init):
    n_q = qT_ref.shape[0]

    def query_maps(t):
        qT = qT_ref[t].astype(F32)
        row = lax.broadcasted_iota(jnp.int32, qT.shape, 0)
        return (jnp.where(row < HEAD_DIM, qT, 0.0).astype(BF16),
                jnp.where(row >= HEAD_DIM, qT, 0.0).astype(BF16))

    s_refs = (s0_ref, s1_ref)
    p_refs = (p0_ref, p1_ref)
    a_refs = (a0_ref, a1_ref)
    t_refs = (t0_ref, t1_ref)
    ones_rows = jnp.ones((ONES_ROWS, tk), BF16)

    every = slice(None)
    upper = slice(tq // 2, tq)

    def qk(q_maps, j, buf, visible=None, cols=every):
        k = k_ref[pl.ds(pl.multiple_of(j * tk, tk), tk), :]
        for mp in range(2):
            s = jnp.dot(k, q_maps[mp][:, cols], preferred_element_type=F32)
            if visible is not None:
                s = jnp.where(visible[:, cols], s, NEG_INF)
            s_refs[buf][mp, :, cols] = s
            t_refs[buf][mp, :, cols] = jnp.max(s, axis=0, keepdims=True)

    def softmax(buf, cols=every):
        for mp in range(2):
            m_old = m_ref[mp, :, cols]
            m_new = jnp.maximum(m_old, t_refs[buf][mp, :, cols])
            a_refs[buf][mp, :, cols] = jnp.exp2(m_old - m_new)
            m_ref[mp, :, cols] = m_new
            p_refs[buf][mp, :, cols] = jnp.exp2(s_refs[buf][mp, :, cols] - m_new).astype(BF16)

    def pv(j, buf, cols=every):
        vT = jnp.concatenate([vT_ref[j], ones_rows], axis=0)
        for mp in range(2):
            upd = jnp.dot(vT, p_refs[buf][mp, :, cols], preferred_element_type=F32)
            acc_ref[mp, :, cols] = a_refs[buf][mp, :, cols] * acc_ref[mp, :, cols] + upd

    def diag_visible(half):
        kc = lax.broadcasted_iota(jnp.int32, (tk, tq), 0) // CHUNK + half * (tk // CHUNK)
        qc = lax.broadcasted_iota(jnp.int32, (tk, tq), 1) // CHUNK
        return kc <= qc

    def tick_pair(q_maps, j, masked):
        pv(j - 2, 0)
        qk(q_maps, j, 0, diag_visible(0) if masked else None)
        softmax(1)
        pv(j - 1, 1)
        if masked:
            qk(q_maps, j + 1, 1, diag_visible(1), cols=upper)
        else:
            qk(q_maps, j + 1, 1)
        softmax(0)

    def start_tile(t, masked):
        q_maps = query_maps(t)
        acc_ref[...] = jnp.zeros_like(acc_ref)
        m_ref[...] = jnp.full(m_ref.shape, NEG_INF, F32)
        qk(q_maps, 0, 0, diag_visible(0) if masked else None)
        qk(q_maps, 1, 1, diag_visible(1) if masked else None)
        softmax(0)

    lam = _lambda_value(lam_ref[...], lambda_init)

    def query_tile(t, carry):
        q_maps = query_maps(t)

        def body(i, c):
            tick_pair(q_maps, 2 + 4 * i, False)
            tick_pair(q_maps, 4 + 4 * i, False)
            return c

        n_plain = jnp.maximum(t - 1, 0)
        lax.fori_loop(0, n_plain // 2, body, 0)

        @pl.when(n_plain % 2 == 1)
        def _():
            tick_pair(q_maps, 2 * t - 2, False)

        @pl.when(t > 0)
        def _():
            tick_pair(q_maps, 2 * t, True)

        pv(2 * t, 0)
        softmax(1, cols=upper)
        pv(2 * t + 1, 1, cols=upper)
        inv0 = 1.0 / acc_ref[0, B_VDIM:B_VDIM + 1]
        inv1 = 1.0 / acc_ref[1, B_VDIM:B_VDIM + 1]
        oT = acc_ref[0, :B_VDIM] * inv0 - lam * (acc_ref[1, :B_VDIM] * inv1)
        o_ref[pl.ds(pl.multiple_of(t * tq, tq), tq), :] = _diff_post(oT.T, g_ref[...], lambda_init)
        start_tile(jnp.minimum(t + 1, n_q - 1), False)
        return carry

    start_tile(0, True)
    lax.fori_loop(0, n_q, query_tile, 0)


def _diff_attention(bqT, bkb, bvT, lam_p, subln_g, *, n_seq, seq, lambda_init):
    tq, tk = ROW_TILE, KV_TILE
    nq = seq // tq
    nk = seq // tk
    rows = n_seq * seq
    bvT4 = bvT.reshape(n_seq, nk, B_WIDTH, tk)
    return pl.pallas_call(
        functools.partial(_diff_attn_kernel, tq=tq, tk=tk, lambda_init=lambda_init),
        out_shape=jax.ShapeDtypeStruct((rows, B_WIDTH), F32),
        grid=(n_seq, N_HEADS),
        in_specs=[
            pl.BlockSpec((nq, B_VDIM, tq), lambda b, h: (b, h, 0)),
            pl.BlockSpec((seq, B_VDIM), lambda b, h: (b, h)),
            pl.BlockSpec((None, nk, B_VDIM, tk), lambda b, h: (b, 0, h, 0)),
            pl.BlockSpec((4, HEAD_DIM), lambda b, h: (0, 0)),
            pl.BlockSpec((1, B_VDIM), lambda b, h: (0, 0)),
        ],
        out_specs=pl.BlockSpec((seq, B_VDIM), lambda b, h: (b, h)),
        scratch_shapes=[
            pltpu.VMEM((2, tk, tq), F32), pltpu.VMEM((2, tk, tq), F32),
            pltpu.VMEM((2, tk, tq), BF16), pltpu.VMEM((2, tk, tq), BF16),
            pltpu.VMEM((2, 1, tq), F32), pltpu.VMEM((2, 1, tq), F32),
            pltpu.VMEM((2, 1, tq), F32), pltpu.VMEM((2, 1, tq), F32),
            pltpu.VMEM((2, 1, tq), F32),
            pltpu.VMEM((2, B_VDIM + ONES_ROWS, tq), F32),
        ],
        compiler_params=_cparams(("arbitrary", "arbitrary")),
        name="diff_attn",
    )(bqT, bkb, bvT4, lam_p, subln_g)


def _head_masks(width):
    lane = lax.broadcasted_iota(jnp.int32, (1, width), 1)
    return [(lane >= h * HEAD_DIM) & (lane < (h + 1) * HEAD_DIM) for h in range(N_HEADS)]


def _dot_nt(a, b):
    return lax.dot_general(a, b, (((1,), (1,)), ((), ())), preferred_element_type=F32)


def _softmax_parts(blocks):
    m = functools.reduce(jnp.maximum, [jnp.max(s, axis=-1, keepdims=True) for s in blocks])
    ps = [jnp.exp(s - m) for s in blocks]
    l = functools.reduce(jnp.add, [jnp.sum(p, axis=-1, keepdims=True) for p in ps])
    return ps, l


def _attend(q_masked, keys, vals, biases):
    blocks = []
    for kk, bb in zip(keys, biases):
        s = _dot_nt(q_masked, kk)
        blocks.append(s if bb is None else s + bb)
    return _softmax_pv(blocks, vals)


def _softmax_pv(blocks, vals):
    ps, l = _softmax_parts(blocks)
    o = functools.reduce(jnp.add, [jnp.dot(p.astype(BF16), vv, preferred_element_type=F32)
                                   for p, vv in zip(ps, vals)])
    return o * (1.0 / l)


def _gate_mix(mixed, g):
    return (mixed * g.astype(F32)).astype(BF16)


def _merge_out(x, mixed, g, w_out_ref, lng, lnb, alpha):
    return _out_norm(x, _gate_mix(mixed, g), w_out_ref, lng, lnb, alpha)


def _out_norm(x, mg, w_out_ref, lng, lnb, alpha):
    y = jnp.dot(mg, w_out_ref[...], preferred_element_type=F32)
    z = alpha * x + y
    mu = jnp.mean(z, axis=-1, keepdims=True)
    zc = z - mu
    var = jnp.mean(zc * zc, axis=-1, keepdims=True)
    return zc * lax.rsqrt(var + LN_EPS) * lng + lnb


def _softmax_pv_t(blocks, vT_blocks, ones_rows):
    m = functools.reduce(jnp.maximum, [jnp.max(s, axis=0, keepdims=True) for s in blocks])
    d = vT_blocks[0].shape[0]
    acc = functools.reduce(jnp.add, [
        jnp.dot(jnp.concatenate([vT, ones_rows], axis=0), jnp.exp2(s - m).astype(BF16),
                preferred_element_type=F32)
        for s, vT in zip(blocks, vT_blocks)])
    return acc[:d] * (1.0 / acc[d:d + 1])


def _merge_kernel(xprev_ref, aqT_ref, k0_ref, k1_ref, k2_ref, vT0_ref, vT1_ref, vT2_ref, mqT_ref, mk_ref, mvT_ref,
                  ob_ref, g_ref, biasT_ref, wout_ref, lng_ref, lnb_ref, y_ref, mg_ref,
                  *, tiles_per_seq, n_tiles, alpha):
    step = pl.program_id(0)

    @pl.when(step == 0)
    def _():
        mg_ref[...] = jnp.zeros_like(mg_ref)

    t = jnp.minimum(step, n_tiles - 1) % tiles_per_seq
    tile = MERGE_TILE
    aqT = aqT_ref[0].astype(F32)
    mqT = mqT_ref[0].astype(F32)
    row = lax.broadcasted_iota(jnp.int32, aqT.shape, 0)
    keys = [k0_ref[...], k1_ref[...], k2_ref[...]]
    vTs = [vT0_ref[0], vT1_ref[0], vT2_ref[0]]
    mk = mk_ref[...]
    mvT = mvT_ref[...]
    ones_rows = jnp.ones((ONES_ROWS, tile), BF16)
    pen0 = jnp.where(t >= 2, 0.0, NEG_INF).astype(F32)
    pen1 = jnp.where(t >= 1, 0.0, NEG_INF).astype(F32)
    pens = (pen0, pen1, None)
    a_scores, m_scores = [], []
    for h in range(N_HEADS):
        in_head = (row >= h * HEAD_DIM) & (row < (h + 1) * HEAD_DIM)
        qh = jnp.where(in_head, aqT, 0.0).astype(BF16)
        blocks = []
        for b in range(3):
            s = jnp.dot(keys[b], qh, preferred_element_type=F32) + biasT_ref[h, b * tile:(b + 1) * tile, :]
            blocks.append(s if pens[b] is None else s + pens[b])
        a_scores.append(blocks)
        qmh = jnp.where(in_head, mqT, 0.0).astype(BF16)
        m_scores.append([jnp.dot(mk, qmh, preferred_element_type=F32)])
    y_ref[...] = _out_norm(xprev_ref[...], mg_ref[...], wout_ref, lng_ref[...], lnb_ref[...], alpha)
    o_aT, o_mT = [], []
    for h in range(N_HEADS):
        hs = slice(h * HEAD_DIM, (h + 1) * HEAD_DIM)
        o_aT.append(_softmax_pv_t(a_scores[h], [vT[hs] for vT in vTs], ones_rows))
        o_mT.append(_softmax_pv_t(m_scores[h], [mvT[hs]], ones_rows))
    o_a = jnp.concatenate(o_aT, axis=0).T
    o_m = jnp.concatenate(o_mT, axis=0).T
    mixed = jnp.concatenate([o_a, ob_ref[...], o_m], axis=1)
    mg_ref[...] = _gate_mix(mixed, g_ref[...])


def _merge_prompt(x2, aqT, akb, avT, mqT, mkb, mvT, ob, g, biasT, w_out_b, lng, lnb, *, seq, alpha):
    rows = x2.shape[0]
    tile = MERGE_TILE
    tps = seq // tile
    n_tiles = rows // tile
    per_row_tile = ROW_TILE // tile

    cur = lambda s: jnp.minimum(s, n_tiles - 1)
    done = lambda s: jnp.maximum(s - 1, 0)

    def band(d):
        return lambda s: (cur(s) // tps) * tps + jnp.maximum(cur(s) % tps - d, 0)

    cur_spec = lambda w: pl.BlockSpec((tile, w), lambda s: (cur(s), 0))
    done_spec = pl.BlockSpec((tile, D_MODEL), lambda s: (done(s), 0))
    qt_spec = pl.BlockSpec((1, A_WIDTH, tile), lambda s: (cur(s) // per_row_tile, 0, cur(s) % per_row_tile))
    k_specs = [pl.BlockSpec((tile, A_WIDTH), lambda s, f=band(d): (f(s), 0)) for d in (2, 1, 0)]
    vt_specs = [pl.BlockSpec((1, A_WIDTH, tile), lambda s, f=band(d): (f(s), 0, 0)) for d in (2, 1, 0)]
    mem_spec = pl.BlockSpec((N_MEM, M_WIDTH), lambda s: (cur(s) // tps, 0))
    const = lambda shape: pl.BlockSpec(shape, lambda s: (0,) * len(shape))
    return pl.pallas_call(
        functools.partial(_merge_kernel, tiles_per_seq=tps, n_tiles=n_tiles, alpha=alpha),
        out_shape=jax.ShapeDtypeStruct((rows, D_MODEL), F32),
        grid=(n_tiles + 1,),
        in_specs=[done_spec, qt_spec] + k_specs + vt_specs + [
            qt_spec, mem_spec, mem_spec,
            cur_spec(B_WIDTH), cur_spec(MIX_WIDTH),
            const((N_HEADS, 3 * tile, tile)), const((MIX_WIDTH, D_MODEL)), const((1, D_MODEL)), const((1, D_MODEL)),
        ],
        out_specs=done_spec,
        scratch_shapes=[pltpu.VMEM((tile, MIX_WIDTH), BF16)],
        compiler_params=_cparams(("arbitrary",)),
        name="merge_prompt",
    )(x2, aqT, akb, akb, akb, avT, avT, avT, mqT, mkb, mvT, ob, g, biasT, w_out_b, lng, lnb)


def _sample_kernel(x_ref, sq_ref, sk_ref, sv_ref, tq_ref, tk_ref, tv_ref, nq_ref, g_ref,
                   cak_ref, cav_ref, cbk_ref, cbv_ref, cmk_ref, cmv_ref,
                   biasc_ref, biasn_ref, lam_ref, subg_ref, wout_ref, lng_ref, lnb_ref, y_ref,
                   *, lambda_init, alpha):
    masks = _head_masks(A_WIDTH)
    sq = sq_ref[...].astype(F32)
    nq = nq_ref[...].astype(F32)
    a_keys = [cak_ref[0].astype(BF16), sk_ref[...].astype(BF16)]
    a_vals = [cav_ref[0].astype(BF16), sv_ref[...].astype(BF16)]
    mk = cmk_ref[0].astype(BF16)
    mv = cmv_ref[0].astype(BF16)
    rows = sq.shape[0]
    a_scores, m_scores = [], []
    for h in range(N_HEADS):
        qh = jnp.where(masks[h], sq, 0.0).astype(BF16)
        a_scores.append([_dot_nt(qh, a_keys[0]) + biasc_ref[h], _dot_nt(qh, a_keys[1]) + biasn_ref[h]])
        qmh = jnp.where(masks[h], nq, 0.0).astype(BF16)
        m_scores.append([_dot_nt(qmh, mk)])
    past_b = cbv_ref.shape[1] // N_HEADS
    b_scores = []
    for h in range(N_HEADS):
        q = tq_ref[:, h * B_VDIM:(h + 1) * B_VDIM].astype(F32)
        q_maps = [q[:, :HEAD_DIM], pltpu.roll(q, HEAD_DIM, 1)[:, :HEAD_DIM]]
        for mp in range(2):
            q_m = q_maps[mp].astype(BF16)
            k_new = tk_ref[pl.ds(2 * h + mp, rows, stride=2 * N_HEADS), :].astype(BF16)
            b_scores.append([jnp.dot(q_m, cbk_ref[0, h, mp].astype(BF16), preferred_element_type=F32),
                             _dot_nt(q_m, k_new)])

    o_a = jnp.zeros((rows, A_WIDTH), F32)
    o_m = jnp.zeros((rows, M_WIDTH), F32)
    for h in range(N_HEADS):
        o_a = jnp.where(masks[h], _softmax_pv(a_scores[h], a_vals), o_a)
        o_m = jnp.where(masks[h], _softmax_pv(m_scores[h], [mv]), o_m)
    lam = _lambda_value(lam_ref[...], lambda_init)
    subg = subg_ref[...]
    o_b = []
    for h in range(N_HEADS):
        vals = [cbv_ref[0, pl.ds(h, past_b, stride=N_HEADS), :].astype(BF16),
                tv_ref[pl.ds(h, rows, stride=N_HEADS), :].astype(BF16)]
        o_maps = [_softmax_pv(b_scores[2 * h + mp], vals) for mp in range(2)]
        o_b.append(_diff_post(o_maps[0] - lam * o_maps[1], subg, lambda_init))
    mixed = jnp.concatenate([o_a] + o_b + [o_m], axis=1)
    y_ref[...] = _merge_out(x_ref[...], mixed, g_ref[...], wout_ref, lng_ref[...], lnb_ref[...], alpha)


def _sample_step(xs2, sq, sk, sv, tq, tk, tv, nq, g, cak, cav, cbk, cbv, cmk, cmv, biasc, biasn,
                 lam_p, subln_g, w_out_b, lng, lnb, *, n_streams, t_new, lambda_init, alpha):
    past_a = cak.shape[1]
    bk_rows, bv_rows = 2 * N_HEADS, N_HEADS
    past_b = cbv.shape[1] // bv_rows
    row_spec = lambda w: pl.BlockSpec((t_new, w), lambda n: (n, 0))
    nrow_spec = lambda r, w: pl.BlockSpec((t_new * r, w), lambda n: (n, 0))
    cache_spec = lambda r, w: pl.BlockSpec((1, r, w), lambda n: (n, 0, 0))
    const = lambda shape: pl.BlockSpec(shape, lambda n: (0,) * len(shape))
    return pl.pallas_call(
        functools.partial(_sample_kernel, lambda_init=lambda_init, alpha=alpha),
        out_shape=jax.ShapeDtypeStruct((n_streams * t_new, D_MODEL), F32),
        grid=(n_streams,),
        in_specs=[
            row_spec(D_MODEL), row_spec(A_WIDTH), row_spec(A_WIDTH), row_spec(A_WIDTH),
            row_spec(B_WIDTH), nrow_spec(bk_rows, HEAD_DIM), nrow_spec(bv_rows, B_VDIM),
            row_spec(M_WIDTH), row_spec(MIX_WIDTH),
            cache_spec(past_a, A_WIDTH), cache_spec(past_a, A_WIDTH),
            pl.BlockSpec((1, N_HEADS, 2, HEAD_DIM, past_b), lambda n: (n, 0, 0, 0, 0)),
            cache_spec(past_b * bv_rows, B_VDIM),
            cache_spec(N_MEM, M_WIDTH), cache_spec(N_MEM, M_WIDTH),
            const((N_HEADS, t_new, past_a)), const((N_HEADS, t_new, t_new)),
            const((4, HEAD_DIM)), const((1, B_VDIM)),
            const((MIX_WIDTH, D_MODEL)), const((1, D_MODEL)), const((1, D_MODEL)),
        ],
        out_specs=row_spec(D_MODEL),
        compiler_params=_cparams(("arbitrary",)),
        name="sample_step",
    )(xs2, sq, sk, sv, tq, tk, tv, nq, g, cak, cav, cbk, cbv, cmk, cmv, biasc, biasn,
      lam_p, subln_g, w_out_b, lng, lnb)


def _rope_tables(pos):
    half = HEAD_DIM // 2
    inv = ROPE_THETA ** (-jnp.arange(half, dtype=F32) / half)
    ang = pos.astype(F32)[:, None] * inv[None, :]
    cos = jnp.cos(ang)
    sin = jnp.sin(ang)
    return jnp.tile(cos, (1, LANES // half)), jnp.concatenate([-sin, sin, -sin, sin], axis=-1)


def _rel_bias(table, dist):
    return table[:, jnp.clip(dist, -REL_CLIP, REL_CLIP) + REL_CLIP].astype(F32)


def _toeplitz_bias(table, n, width, offset):
    period = -(-(n + width) // LANES) * LANES
    u = jnp.arange(period)
    dist = jnp.where(u < width, offset - u, offset + period - u)
    diagonals = _rel_bias(table, dist)
    flat = jnp.tile(diagonals, (1, n))[:, :n * (period - 1)]
    return flat.reshape(table.shape[0], n, period - 1)[:, :, :width]


def _band_bias_prompt(table):
    n, width = MERGE_TILE, 3 * MERGE_TILE
    bias = _toeplitz_bias(table, n, width, BAND_ROWS)
    i = jnp.arange(n)[:, None]
    j = jnp.arange(width)[None, :]
    qc = i // CHUNK
    kc = j // CHUNK
    visible = (kc >= qc) & (kc <= qc + N_BAND_CHUNKS)
    return jnp.where(visible[None], bias, NEG_INF)


def kernel(x_prompt, x_sample, cache_a_k, cache_a_v, cache_b_k, cache_b_v, cache_mem_k, cache_mem_v, mem_prompt,
           w_in, w_mem_kv, a_rel_bias, diff_lambda, diff_subln_g, w_out, ln_g, ln_b):
    depth = w_in.shape[0]
    assert depth == 1, "single-layer step only"
    n_seq, seq, _ = x_prompt.shape
    n_streams, t_new, _ = x_sample.shape
    past_a = cache_a_k.shape[2]
    past_b = cache_b_k.shape[2]
    assert seq % ROW_TILE == 0 and BAND_ROWS == ROW_TILE == 2 * MERGE_TILE and past_a == BAND_ROWS
    assert n_streams * t_new == ROW_TILE
    layer = 0
    lambda_init = 0.8 - 0.6 * math.exp(-0.3 * layer)
    alpha = (2.0 * depth) ** 0.25

    w_in_b = w_in[layer].astype(BF16)
    w_mem_b = w_mem_kv[layer].astype(BF16)
    w_out_b = w_out[layer].astype(BF16)
    table = a_rel_bias[layer]
    lam_p = diff_lambda[layer]
    subln_g = diff_subln_g[layer].reshape(1, B_VDIM)
    lng = ln_g[layer].reshape(1, D_MODEL)
    lnb = ln_b[layer].reshape(1, D_MODEL)

    rows = n_seq * seq
    x2 = x_prompt.reshape(rows, D_MODEL)
    cos_p, sin_p = _rope_tables(jnp.arange(seq))
    (aqT, akb, avT, akf, avf, bqT, bkb, bvT, bkf, bvf, mqT, g) = _project(
        x2, w_in_b, cos_p, sin_p, prompt=True, rows_per_seq=seq)
    mk, mv, mkb, mvT = _mem_kv(mem_prompt.reshape(n_seq * N_MEM, D_MODEL), w_mem_b)
    ob = _diff_attention(bqT, bkb, bvT, lam_p, subln_g, n_seq=n_seq, seq=seq, lambda_init=lambda_init)
    biasT = jnp.swapaxes(_band_bias_prompt(table), 1, 2) * LOG2E
    y_p = _merge_prompt(x2, aqT, akb, avT, mqT, mkb, mvT, ob, g, biasT, w_out_b, lng, lnb,
                        seq=seq, alpha=alpha)

    xs2 = x_sample.reshape(n_streams * t_new, D_MODEL)
    pos_s = past_b + jnp.arange(t_new)
    cos_s, sin_s = _rope_tables(jnp.tile(pos_s, n_streams))
    sq, sk, sv, tq, tk, tv, nq, sg = _project(xs2, w_in_b, cos_s, sin_s, prompt=False, rows_per_seq=t_new)
    bias_s = _toeplitz_bias(table, t_new, past_a + t_new, past_a)
    biasc, biasn = bias_s[:, :, :past_a], bias_s[:, :, past_a:]
    y_s = _sample_step(
        xs2, sq, sk, sv, tq, tk, tv, nq, sg,
        cache_a_k[layer].reshape(n_streams, past_a, A_WIDTH), cache_a_v[layer].reshape(n_streams, past_a, A_WIDTH),
        jnp.transpose(cache_b_k[layer], (0, 2, 3, 4, 1)),
        cache_b_v[layer].reshape(n_streams, past_b * N_HEADS, B_VDIM),
        cache_mem_k[layer].reshape(n_streams, N_MEM, M_WIDTH), cache_mem_v[layer].reshape(n_streams, N_MEM, M_WIDTH),
        biasc, biasn, lam_p, subln_g, w_out_b, lng, lnb,
        n_streams=n_streams, t_new=t_new, lambda_init=lambda_init, alpha=alpha)

    hd = (N_HEADS, HEAD_DIM)
    return (
        y_p.reshape(n_seq, seq, D_MODEL),
        y_s.reshape(n_streams, t_new, D_MODEL),
        akf.reshape(1, n_seq, BAND_ROWS, *hd),
        avf.reshape(1, n_seq, BAND_ROWS, *hd),
        jnp.transpose(bkf, (0, 4, 1, 2, 3))[None],
        bvf.reshape(1, n_seq, seq, N_HEADS, B_VDIM),
        mk.reshape(1, n_seq, N_MEM, *hd),
        mv.reshape(1, n_seq, N_MEM, *hd),
        sk.reshape(1, n_streams, t_new, *hd),
        sv.reshape(1, n_streams, t_new, *hd),
        tk.reshape(1, n_streams, t_new, N_HEADS, 2, HEAD_DIM),
        tv.reshape(1, n_streams, t_new, N_HEADS, B_VDIM),
    )
```

```python
import functools
import math

import jax
import jax.numpy as jnp
from jax import lax
from jax.experimental import pallas as pl
from jax.experimental.pallas import tpu as pltpu

F32 = jnp.float32
BF16 = jnp.bfloat16

D_MODEL = 1024
CHUNK = 64
N_BAND_CHUNKS = 8
BAND_ROWS = N_BAND_CHUNKS * CHUNK
HEAD_DIM = 64
A_WIDTH = 256
B_WIDTH = 512
B_VDIM = 128
M_WIDTH = 256
N_MEM = 256
N_HEADS = 4
MIX_WIDTH = A_WIDTH + B_WIDTH + M_WIDTH
REL_CLIP = 128
ROPE_THETA = 10000.0
LN_EPS = 1e-5
RMS_EPS = 1e-5
NEG_INF = -1e30
QK_SCALE = HEAD_DIM ** -0.5
LOG2E = math.log2(math.e)

_OFF = {}
_o = 0
for _name, _w in (("aq", A_WIDTH), ("ak", A_WIDTH), ("av", A_WIDTH), ("bq", B_WIDTH), ("bk", B_WIDTH),
                  ("bv", B_WIDTH), ("mq", M_WIDTH), ("gate", MIX_WIDTH)):
    _OFF[_name] = (_o, _o + _w)
    _o += _w
PROJ_TOTAL = _o

LANES = 128
ROW_TILE = 512
KV_TILE = ROW_TILE // 2
ONES_ROWS = 16
MERGE_TILE = 256
VMEM_LIMIT = 56 * 1024 * 1024


def _cparams(sem):
    return pltpu.CompilerParams(dimension_semantics=sem, vmem_limit_bytes=VMEM_LIMIT)


def _rope_slab(x, cos, sin_signed, lo_half):
    left = pltpu.roll(x, LANES - 32, 1)
    right = pltpu.roll(x, 32, 1)
    swapped = jnp.where(lo_half, left, right)
    return x * cos + swapped * sin_signed


def _proj_kernel(x_ref, w_ref, cos_ref, sin_ref, *out_refs, prompt):
    if prompt:
        (aq_ref, akb_ref, avb_ref, akf_ref, avf_ref, bqT_ref, bkb_ref, bvT_ref,
         bkf_ref, bvf_ref, mq_ref, g_ref) = out_refs
    else:
        aq_ref, akf_ref, avf_ref, bq_ref, bkf_ref, bvf_ref, mq_ref, g_ref = out_refs
    xb = x_ref[...].astype(BF16)

    def seg(name):
        lo, hi = _OFF[name]
        return jnp.dot(xb, w_ref[:, lo:hi], preferred_element_type=F32)

    aq = seg("aq")
    ak = seg("ak")
    av = seg("av")
    akf_ref[...] = ak
    avf_ref[...] = av
    if prompt:
        aq_ref[0] = (aq * (QK_SCALE * LOG2E)).T.astype(BF16)
        akb_ref[...] = ak.astype(BF16)
        avT = av.T.astype(BF16)
        for half in range(ROW_TILE // MERGE_TILE):
            avb_ref[half] = avT[:, half * MERGE_TILE:(half + 1) * MERGE_TILE]
    else:
        aq_ref[...] = (aq * QK_SCALE).astype(BF16)

    cos = cos_ref[...]
    sin = sin_ref[...]
    lane = lax.broadcasted_iota(jnp.int32, cos.shape, 1)
    lo_half = (lane % HEAD_DIM) < (HEAD_DIM // 2)
    bq = seg("bq")
    bk = seg("bk")
    bv = seg("bv")
    n_rows = bv.shape[0]
    for c in range(B_WIDTH // LANES):
        sl = slice(c * LANES, (c + 1) * LANES)
        q_c = _rope_slab(bq[:, sl], cos, sin, lo_half)
        k_c = _rope_slab(bk[:, sl], cos, sin, lo_half)
        bvf_ref[pl.ds(c, n_rows, stride=N_HEADS), :] = bv[:, sl]
        if prompt:
            kT_c = k_c.T
            bkf_ref[0, c, 0] = kT_c[:HEAD_DIM]
            bkf_ref[0, c, 1] = kT_c[HEAD_DIM:]
            bkb_ref[:, sl] = k_c.astype(BF16)
            bqT_ref[0, sl, :] = (q_c * (QK_SCALE * LOG2E)).T.astype(BF16)
            vT_c = bv[:, sl].T.astype(BF16)
            for half in range(ROW_TILE // KV_TILE):
                bvT_ref[half, sl, :] = vT_c[:, half * KV_TILE:(half + 1) * KV_TILE]
        else:
            bkf_ref[pl.ds(2 * c, n_rows, stride=2 * N_HEADS), :] = k_c[:, :HEAD_DIM]
            bkf_ref[pl.ds(2 * c + 1, n_rows, stride=2 * N_HEADS), :] = k_c[:, HEAD_DIM:]
            bq_ref[:, sl] = (q_c * QK_SCALE).astype(BF16)

    if prompt:
        mq_ref[0] = (seg("mq") * (QK_SCALE * LOG2E)).T.astype(BF16)
    else:
        mq_ref[...] = (seg("mq") * QK_SCALE).astype(BF16)
    gate = seg("gate")
    g_ref[...] = (gate / (1.0 + jnp.exp(-gate))).astype(BF16)


def _project(x2, w_in_b, cos, sin, *, prompt, rows_per_seq):
    rows = x2.shape[0]
    tm = ROW_TILE
    n_tiles = rows // tm
    pos_tiles = cos.shape[0] // tm
    row_spec = lambda w: pl.BlockSpec((tm, w), lambda i: (i, 0))
    bk_rows, bv_rows = 2 * N_HEADS, N_HEADS
    bkf_shape = jax.ShapeDtypeStruct((rows * bk_rows, HEAD_DIM), F32)
    bvf_shape = jax.ShapeDtypeStruct((rows * bv_rows, B_VDIM), F32)
    bkf_spec = pl.BlockSpec((tm * bk_rows, HEAD_DIM), lambda i: (i, 0))
    bvf_spec = pl.BlockSpec((tm * bv_rows, B_VDIM), lambda i: (i, 0))
    in_specs = [
        row_spec(D_MODEL),
        pl.BlockSpec((D_MODEL, PROJ_TOTAL), lambda i: (0, 0)),
        pl.BlockSpec((tm, LANES), lambda i: (i % pos_tiles, 0)),
        pl.BlockSpec((tm, LANES), lambda i: (i % pos_tiles, 0)),
    ]
    sds = jax.ShapeDtypeStruct
    if prompt:
        tiles_per_seq = rows_per_seq // tm
        n_seq = rows // rows_per_seq
        tail_spec = pl.BlockSpec((tm, A_WIDTH), lambda i: (i // tiles_per_seq, 0))
        t_spec = pl.BlockSpec((1, B_WIDTH, tm), lambda i: (i, 0, 0))
        kv_per_row_tile = tm // KV_TILE
        vt_spec = pl.BlockSpec((kv_per_row_tile, B_WIDTH, KV_TILE), lambda i: (i, 0, 0))
        bkf_shape = jax.ShapeDtypeStruct((n_seq, N_HEADS, 2, HEAD_DIM, rows_per_seq), F32)
        bkf_spec = pl.BlockSpec((1, N_HEADS, 2, HEAD_DIM, tm),
                                lambda i: (i // tiles_per_seq, 0, 0, 0, i % tiles_per_seq))
        qt_spec = pl.BlockSpec((1, A_WIDTH, tm), lambda i: (i, 0, 0))
        merge_per_row_tile = tm // MERGE_TILE
        avt_spec = pl.BlockSpec((merge_per_row_tile, A_WIDTH, MERGE_TILE), lambda i: (i, 0, 0))
        out_shape = (
            sds((n_tiles, A_WIDTH, tm), BF16), sds((rows, A_WIDTH), BF16),
            sds((n_tiles * merge_per_row_tile, A_WIDTH, MERGE_TILE), BF16),
            sds((n_seq * tm, A_WIDTH), F32), sds((n_seq * tm, A_WIDTH), F32),
            sds((n_tiles, B_WIDTH, tm), BF16), sds((rows, B_WIDTH), BF16),
            sds((n_tiles * kv_per_row_tile, B_WIDTH, KV_TILE), BF16),
            bkf_shape, bvf_shape,
            sds((n_tiles, M_WIDTH, tm), BF16), sds((rows, MIX_WIDTH), BF16),
        )
        out_specs = (
            qt_spec, row_spec(A_WIDTH), avt_spec, tail_spec, tail_spec,
            t_spec, row_spec(B_WIDTH), vt_spec, bkf_spec, bvf_spec,
            qt_spec, row_spec(MIX_WIDTH),
        )
    else:
        out_shape = (
            sds((rows, A_WIDTH), BF16), sds((rows, A_WIDTH), F32), sds((rows, A_WIDTH), F32),
            sds((rows, B_WIDTH), BF16), bkf_shape, bvf_shape,
            sds((rows, M_WIDTH), BF16), sds((rows, MIX_WIDTH), BF16),
        )
        out_specs = (
            row_spec(A_WIDTH), row_spec(A_WIDTH), row_spec(A_WIDTH),
            row_spec(B_WIDTH), bkf_spec, bvf_spec,
            row_spec(M_WIDTH), row_spec(MIX_WIDTH),
        )
    return pl.pallas_call(
        functools.partial(_proj_kernel, prompt=prompt),
        out_shape=out_shape,
        grid=(n_tiles,),
        in_specs=in_specs,
        out_specs=out_specs,
        compiler_params=_cparams(("arbitrary",)),
        name="proj_prompt" if prompt else "proj_sample",
    )(x2, w_in_b, cos, sin)


def _mem_kv_kernel(mem_ref, w_ref, mk_ref, mv_ref, mkb_ref, mvT_ref):
    kv = jnp.dot(mem_ref[...].astype(BF16), w_ref[...], preferred_element_type=F32)
    mk_ref[...] = kv[:, :M_WIDTH]
    mv_ref[...] = kv[:, M_WIDTH:]
    mkb_ref[...] = kv[:, :M_WIDTH].astype(BF16)
    mvT_ref[...] = kv[:, M_WIDTH:].T.astype(BF16)


def _mem_kv(mem2, w_b):
    rows = mem2.shape[0]
    tm = N_MEM
    spec = pl.BlockSpec((tm, M_WIDTH), lambda i: (i, 0))
    sds = jax.ShapeDtypeStruct
    return pl.pallas_call(
        _mem_kv_kernel,
        out_shape=(sds((rows, M_WIDTH), F32), sds((rows, M_WIDTH), F32),
                   sds((rows, M_WIDTH), BF16), sds((rows, N_MEM), BF16)),
        grid=(rows // tm,),
        in_specs=[pl.BlockSpec((tm, D_MODEL), lambda i: (i, 0)),
                  pl.BlockSpec((D_MODEL, 2 * M_WIDTH), lambda i: (0, 0))],
        out_specs=(spec, spec, spec, spec),
        compiler_params=_cparams(("arbitrary",)),
        name="mem_kv",
    )(mem2, w_b)


def _lambda_value(lp, lambda_init):
    a = jnp.sum(lp[0:1, :] * lp[1:2, :], axis=1, keepdims=True)
    b = jnp.sum(lp[2:3, :] * lp[3:4, :], axis=1, keepdims=True)
    return jnp.exp(a) - jnp.exp(b) + lambda_init


def _diff_post(o, g, lambda_init):
    ms = jnp.mean(o * o, axis=-1, keepdims=True)
    return o * lax.rsqrt(ms + RMS_EPS) * g * (1.0 - lambda_init)


def _diff_attn_kernel(qT_ref, k_ref, vT_ref, lam_ref, g_ref, o_ref,
                      s0_ref, s1_ref, p0_ref, p1_ref, a0_ref, a1_ref, t0_ref, t1_ref, m_ref, acc_ref,
                      *, tq, tk, lambda_init):
    n_q = qT_ref.shape[0]

    def query_maps(t):
        qT = qT_ref[t].astype(F32)
        row = lax.broadcasted_iota(jnp.int32, qT.shape, 0)
        return (jnp.where(row < HEAD_DIM, qT, 0.0).astype(BF16),
                jnp.where(row >= HEAD_DIM, qT, 0.0).astype(BF16))

    s_refs = (s0_ref, s1_ref)
    p_refs = (p0_ref, p1_ref)
    a_refs = (a0_ref, a1_ref)
    t_refs = (t0_ref, t1_ref)
    ones_rows = jnp.ones((ONES_ROWS, tk), BF16)

    every = slice(None)
    upper = slice(tq // 2, tq)

    def qk(q_maps, j, buf, visible=None, cols=every):
        k = k_ref[pl.ds(pl.multiple_of(j * tk, tk), tk), :]
        for mp in range(2):
            s = jnp.dot(k, q_maps[mp][:, cols], preferred_element_type=F32)
            if visible is not None:
                s = jnp.where(visible[:, cols], s, NEG_INF)
            s_refs[buf][mp, :, cols] = s
            t_refs[buf][mp, :, cols] = jnp.max(s, axis=0, keepdims=True)

    def softmax(buf, cols=every):
        for mp in range(2):
            m_old = m_ref[mp, :, cols]
            m_new = jnp.maximum(m_old, t_refs[buf][mp, :, cols])
            a_refs[buf][mp, :, cols] = jnp.exp2(m_old - m_new)
            m_ref[mp, :, cols] = m_new
            p_refs[buf][mp, :, cols] = jnp.exp2(s_refs[buf][mp, :, cols] - m_new).astype(BF16)

    def pv(j, buf, cols=every):
        vT = jnp.concatenate([vT_ref[j], ones_rows], axis=0)
        for mp in range(2):
            upd = jnp.dot(vT, p_refs[buf][mp, :, cols], preferred_element_type=F32)
            acc_ref[mp, :, cols] = a_refs[buf][mp, :, cols] * acc_ref[mp, :, cols] + upd

    def diag_visible(half):
        kc = lax.broadcasted_iota(jnp.int32, (tk, tq), 0) // CHUNK + half * (tk // CHUNK)
        qc = lax.broadcasted_iota(jnp.int32, (tk, tq), 1) // CHUNK
        return kc <= qc

    def tick_pair(q_maps, j, masked):
        pv(j - 2, 0)
        qk(q_maps, j, 0, diag_visible(0) if masked else None)
        softmax(1)
        pv(j - 1, 1)
        if masked:
            qk(q_maps, j + 1, 1, diag_visible(1), cols=upper)
        else:
            qk(q_maps, j + 1, 1)
        softmax(0)

    def start_tile(t, masked):
        q_maps = query_maps(t)
        acc_ref[...] = jnp.zeros_like(acc_ref)
        m_ref[...] = jnp.full(m_ref.shape, NEG_INF, F32)
        qk(q_maps, 0, 0, diag_visible(0) if masked else None)
        qk(q_maps, 1, 1, diag_visible(1) if masked else None)
        softmax(0)

    lam = _lambda_value(lam_ref[...], lambda_init)

    def query_tile(t, carry):
        q_maps = query_maps(t)

        def body(i, c):
            tick_pair(q_maps, 2 + 4 * i, False)
            tick_pair(q_maps, 4 + 4 * i, False)
            return c

        n_plain = jnp.maximum(t - 1, 0)
        lax.fori_loop(0, n_plain // 2, body, 0)

        @pl.when(n_plain % 2 == 1)
        def _():
            tick_pair(q_maps, 2 * t - 2, False)

        @pl.when(t > 0)
        def _():
            tick_pair(q_maps, 2 * t, True)

        pv(2 * t, 0)
        softmax(1, cols=upper)
        pv(2 * t + 1, 1, cols=upper)
        inv0 = 1.0 / acc_ref[0, B_VDIM:B_VDIM + 1]
        inv1 = 1.0 / acc_ref[1, B_VDIM:B_VDIM + 1]
        oT = acc_ref[0, :B_VDIM] * inv0 - lam * (acc_ref[1, :B_VDIM] * inv1)
        o_ref[pl.ds(pl.multiple_of(t * tq, tq), tq), :] = _diff_post(oT.T, g_ref[...], lambda_init)
        start_tile(jnp.minimum(t + 1, n_q - 1), False)
        return carry

    start_tile(0, True)
    lax.fori_loop(0, n_q, query_tile, 0)


def _diff_attention(bqT, bkb, bvT, lam_p, subln_g, *, n_seq, seq, lambda_init):
    tq, tk = ROW_TILE, KV_TILE
    nq = seq // tq
    nk = seq // tk
    rows = n_seq * seq
    bvT4 = bvT.reshape(n_seq, nk, B_WIDTH, tk)
    return pl.pallas_call(
        functools.partial(_diff_attn_kernel, tq=tq, tk=tk, lambda_init=lambda_init),
        out_shape=jax.ShapeDtypeStruct((rows, B_WIDTH), F32),
        grid=(n_seq, N_HEADS),
        in_specs=[
            pl.BlockSpec((nq, B_VDIM, tq), lambda b, h: (b, h, 0)),
            pl.BlockSpec((seq, B_VDIM), lambda b, h: (b, h)),
            pl.BlockSpec((None, nk, B_VDIM, tk), lambda b, h: (b, 0, h, 0)),
            pl.BlockSpec((4, HEAD_DIM), lambda b, h: (0, 0)),
            pl.BlockSpec((1, B_VDIM), lambda b, h: (0, 0)),
        ],
        out_specs=pl.BlockSpec((seq, B_VDIM), lambda b, h: (b, h)),
        scratch_shapes=[
            pltpu.VMEM((2, tk, tq), F32), pltpu.VMEM((2, tk, tq), F32),
            pltpu.VMEM((2, tk, tq), BF16), pltpu.VMEM((2, tk, tq), BF16),
            pltpu.VMEM((2, 1, tq), F32), pltpu.VMEM((2, 1, tq), F32),
            pltpu.VMEM((2, 1, tq), F32), pltpu.VMEM((2, 1, tq), F32),
            pltpu.VMEM((2, 1, tq), F32),
            pltpu.VMEM((2, B_VDIM + ONES_ROWS, tq), F32),
        ],
        compiler_params=_cparams(("arbitrary", "arbitrary")),
        name="diff_attn",
    )(bqT, bkb, bvT4, lam_p, subln_g)


def _head_masks(width):
    lane = lax.broadcasted_iota(jnp.int32, (1, width), 1)
    return [(lane >= h * HEAD_DIM) & (lane < (h + 1) * HEAD_DIM) for h in range(N_HEADS)]


def _dot_nt(a, b):
    return lax.dot_general(a, b, (((1,), (1,)), ((), ())), preferred_element_type=F32)


def _softmax_parts(blocks):
    m = functools.reduce(jnp.maximum, [jnp.max(s, axis=-1, keepdims=True) for s in blocks])
    ps = [jnp.exp(s - m) for s in blocks]
    l = functools.reduce(jnp.add, [jnp.sum(p, axis=-1, keepdims=True) for p in ps])
    return ps, l


def _softmax_pv(blocks, vals):
    ps, l = _softmax_parts(blocks)
    o = functools.reduce(jnp.add, [jnp.dot(p.astype(BF16), vv, preferred_element_type=F32)
                                   for p, vv in zip(ps, vals)])
    return o * (1.0 / l)


def _gate_mix(mixed, g):
    return (mixed * g.astype(F32)).astype(BF16)


def _out_norm(x, mg, w_out_ref, lng, lnb, alpha):
    return _residual_norm(x, jnp.dot(mg, w_out_ref[...], preferred_element_type=F32), lng, lnb, alpha)


def _residual_norm(x, y, lng, lnb, alpha):
    z = alpha * x + y
    mu = jnp.mean(z, axis=-1, keepdims=True)
    zc = z - mu
    var = jnp.mean(zc * zc, axis=-1, keepdims=True)
    return zc * lax.rsqrt(var + LN_EPS) * lng + lnb


def _softmax_pv_t(blocks, vT_blocks, ones_rows):
    m = functools.reduce(jnp.maximum, [jnp.max(s, axis=0, keepdims=True) for s in blocks])
    d = vT_blocks[0].shape[0]
    acc = functools.reduce(jnp.add, [
        jnp.dot(jnp.concatenate([vT, ones_rows], axis=0), jnp.exp2(s - m).astype(BF16),
                preferred_element_type=F32)
        for s, vT in zip(blocks, vT_blocks)])
    return acc[:d] * (1.0 / acc[d:d + 1])


def _merge_kernel(xprev_ref, aqT_ref, k0_ref, k1_ref, k2_ref, vT0_ref, vT1_ref, vT2_ref, mqT_ref, mk_ref, mvT_ref,
                  ob_ref, g_ref, biasT_ref, wout_ref, lng_ref, lnb_ref, y_ref, mg_ref,
                  *, tiles_per_seq, n_tiles, alpha):
    step = pl.program_id(0)

    @pl.when(step == 0)
    def _():
        mg_ref[...] = jnp.zeros_like(mg_ref)

    t = jnp.minimum(step, n_tiles - 1) % tiles_per_seq
    tile = MERGE_TILE
    aqT = aqT_ref[0].astype(F32)
    mqT = mqT_ref[0].astype(F32)
    row = lax.broadcasted_iota(jnp.int32, aqT.shape, 0)
    keys = [k0_ref[...], k1_ref[...], k2_ref[...]]
    vTs = [vT0_ref[0], vT1_ref[0], vT2_ref[0]]
    mk = mk_ref[...]
    mvT = mvT_ref[...]
    ones_rows = jnp.ones((ONES_ROWS, tile), BF16)
    pen0 = jnp.where(t >= 2, 0.0, NEG_INF).astype(F32)
    pen1 = jnp.where(t >= 1, 0.0, NEG_INF).astype(F32)
    pens = (pen0, pen1, None)
    a_scores, m_scores = [], []
    for h in range(N_HEADS):
        in_head = (row >= h * HEAD_DIM) & (row < (h + 1) * HEAD_DIM)
        qh = jnp.where(in_head, aqT, 0.0).astype(BF16)
        blocks = []
        for b in range(3):
            s = jnp.dot(keys[b], qh, preferred_element_type=F32) + biasT_ref[h, b * tile:(b + 1) * tile, :]
            blocks.append(s if pens[b] is None else s + pens[b])
        a_scores.append(blocks)
        qmh = jnp.where(in_head, mqT, 0.0).astype(BF16)
        m_scores.append([jnp.dot(mk, qmh, preferred_element_type=F32)])
    y_ref[...] = _out_norm(xprev_ref[...], mg_ref[...], wout_ref, lng_ref[...], lnb_ref[...], alpha)
    o_aT, o_mT = [], []
    for h in range(N_HEADS):
        hs = slice(h * HEAD_DIM, (h + 1) * HEAD_DIM)
        o_aT.append(_softmax_pv_t(a_scores[h], [vT[hs] for vT in vTs], ones_rows))
        o_mT.append(_softmax_pv_t(m_scores[h], [mvT[hs]], ones_rows))
    o_a = jnp.concatenate(o_aT, axis=0).T
    o_m = jnp.concatenate(o_mT, axis=0).T
    mixed = jnp.concatenate([o_a, ob_ref[...], o_m], axis=1)
    mg_ref[...] = _gate_mix(mixed, g_ref[...])


def _merge_prompt(x2, aqT, akb, avT, mqT, mkb, mvT, ob, g, biasT, w_out_b, lng, lnb, *, seq, alpha):
    rows = x2.shape[0]
    tile = MERGE_TILE
    tps = seq // tile
    n_tiles = rows // tile
    per_row_tile = ROW_TILE // tile

    cur = lambda s: jnp.minimum(s, n_tiles - 1)
    done = lambda s: jnp.maximum(s - 1, 0)

    def band(d):
        return lambda s: (cur(s) // tps) * tps + jnp.maximum(cur(s) % tps - d, 0)

    cur_spec = lambda w: pl.BlockSpec((tile, w), lambda s: (cur(s), 0))
    done_spec = pl.BlockSpec((tile, D_MODEL), lambda s: (done(s), 0))
    qt_spec = pl.BlockSpec((1, A_WIDTH, tile), lambda s: (cur(s) // per_row_tile, 0, cur(s) % per_row_tile))
    k_specs = [pl.BlockSpec((tile, A_WIDTH), lambda s, f=band(d): (f(s), 0)) for d in (2, 1, 0)]
    vt_specs = [pl.BlockSpec((1, A_WIDTH, tile), lambda s, f=band(d): (f(s), 0, 0)) for d in (2, 1, 0)]
    mem_spec = pl.BlockSpec((N_MEM, M_WIDTH), lambda s: (cur(s) // tps, 0))
    const = lambda shape: pl.BlockSpec(shape, lambda s: (0,) * len(shape))
    return pl.pallas_call(
        functools.partial(_merge_kernel, tiles_per_seq=tps, n_tiles=n_tiles, alpha=alpha),
        out_shape=jax.ShapeDtypeStruct((rows, D_MODEL), F32),
        grid=(n_tiles + 1,),
        in_specs=[done_spec, qt_spec] + k_specs + vt_specs + [
            qt_spec, mem_spec, mem_spec,
            cur_spec(B_WIDTH), cur_spec(MIX_WIDTH),
            const((N_HEADS, 3 * tile, tile)), const((MIX_WIDTH, D_MODEL)), const((1, D_MODEL)), const((1, D_MODEL)),
        ],
        out_specs=done_spec,
        scratch_shapes=[pltpu.VMEM((tile, MIX_WIDTH), BF16)],
        compiler_params=_cparams(("arbitrary",)),
        name="merge_prompt",
    )(x2, aqT, akb, akb, akb, avT, avT, avT, mqT, mkb, mvT, ob, g, biasT, w_out_b, lng, lnb)


def _sample_kernel(x_ref, sq_ref, sk_ref, sv_ref, tq_ref, tk_ref, tv_ref, nq_ref, g_ref,
                   cak_ref, cav_ref, cbk_ref, cbv_ref, cmk_ref, cmv_ref,
                   biasc_ref, biasn_ref, lam_ref, subg_ref, wout_ref, lng_ref, lnb_ref, y_ref,
                   *, lambda_init, alpha):
    masks = _head_masks(A_WIDTH)
    sq = sq_ref[...].astype(F32)
    nq = nq_ref[...].astype(F32)
    sk_new = sk_ref[...].astype(BF16)
    sv = sv_ref[...]
    rows = sq.shape[0]

    def head_cols(x, h):
        pair = x[:, (h // 2) * LANES:(h // 2 + 1) * LANES]
        if h % 2:
            pair = pltpu.roll(pair, HEAD_DIM, 1)
        return pair[:, :HEAD_DIM].astype(BF16)

    a_scores, m_scores = [], []
    for h in range(N_HEADS):
        qh = jnp.where(masks[h], sq, 0.0).astype(BF16)
        s_cache = jnp.dot(head_cols(sq, h), cak_ref[0, h].astype(BF16), preferred_element_type=F32)
        a_scores.append([s_cache + biasc_ref[h], _dot_nt(qh, sk_new) + biasn_ref[h]])
        m_scores.append([jnp.dot(head_cols(nq, h), cmk_ref[0, h].astype(BF16), preferred_element_type=F32)])
    past_b = cbv_ref.shape[1] // N_HEADS
    b_scores = []
    for h in range(N_HEADS):
        q = tq_ref[:, h * B_VDIM:(h + 1) * B_VDIM].astype(F32)
        q_maps = [q[:, :HEAD_DIM], pltpu.roll(q, HEAD_DIM, 1)[:, :HEAD_DIM]]
        for mp in range(2):
            q_m = q_maps[mp].astype(BF16)
            k_new = tk_ref[pl.ds(2 * h + mp, rows, stride=2 * N_HEADS), :].astype(BF16)
            b_scores.append([jnp.dot(q_m, cbk_ref[0, h, mp].astype(BF16), preferred_element_type=F32),
                             _dot_nt(q_m, k_new)])

    pieces = []
    for h in range(N_HEADS):
        ps, l = _softmax_parts(a_scores[h])
        o = (_dot_nt(ps[0].astype(BF16), cav_ref[0, h].astype(BF16))
             + jnp.dot(ps[1].astype(BF16), head_cols(sv, h), preferred_element_type=F32))
        pieces.append((o * (1.0 / l), h * HEAD_DIM))
        ps, l = _softmax_parts(m_scores[h])
        o = _dot_nt(ps[0].astype(BF16), cmv_ref[0, h].astype(BF16))
        pieces.append((o * (1.0 / l), A_WIDTH + B_WIDTH + h * HEAD_DIM))
    lam = _lambda_value(lam_ref[...], lambda_init)
    subg = subg_ref[...]
    for h in range(N_HEADS):
        vals = [cbv_ref[0, pl.ds(h, past_b, stride=N_HEADS), :].astype(BF16),
                tv_ref[pl.ds(h, rows, stride=N_HEADS), :].astype(BF16)]
        o_maps = [_softmax_pv(b_scores[2 * h + mp], vals) for mp in range(2)]
        pieces.append((_diff_post(o_maps[0] - lam * o_maps[1], subg, lambda_init), A_WIDTH + h * B_VDIM))

    g = g_ref[...].astype(F32)
    y = None
    for o, col in pieces:
        width = o.shape[1]
        if width == LANES:
            g_cols = g[:, col:col + LANES]
        else:
            g_pair = g[:, (col // LANES) * LANES:(col // LANES + 1) * LANES]
            if col % LANES:
                g_pair = pltpu.roll(g_pair, LANES - col % LANES, 1)
            g_cols = g_pair[:, :width]
        part = jnp.dot((o * g_cols).astype(BF16), wout_ref[col:col + width, :], preferred_element_type=F32)
        y = part if y is None else y + part
    y_ref[...] = _residual_norm(x_ref[...], y, lng_ref[...], lnb_ref[...], alpha)


def _sample_step(xs2, sq, sk, sv, tq, tk, tv, nq, g, cak, cav, cbk, cbv, cmk, cmv, biasc, biasn,
                 lam_p, subln_g, w_out_b, lng, lnb, *, n_streams, t_new, lambda_init, alpha):
    past_a = cav.shape[3]
    bk_rows, bv_rows = 2 * N_HEADS, N_HEADS
    head_t_spec = lambda r: pl.BlockSpec((1, N_HEADS, HEAD_DIM, r), lambda n: (n, 0, 0, 0))
    past_b = cbv.shape[1] // bv_rows
    row_spec = lambda w: pl.BlockSpec((t_new, w), lambda n: (n, 0))
    nrow_spec = lambda r, w: pl.BlockSpec((t_new * r, w), lambda n: (n, 0))
    cache_spec = lambda r, w: pl.BlockSpec((1, r, w), lambda n: (n, 0, 0))
    const = lambda shape: pl.BlockSpec(shape, lambda n: (0,) * len(shape))
    return pl.pallas_call(
        functools.partial(_sample_kernel, lambda_init=lambda_init, alpha=alpha),
        out_shape=jax.ShapeDtypeStruct((n_streams * t_new, D_MODEL), F32),
        grid=(n_streams,),
        in_specs=[
            row_spec(D_MODEL), row_spec(A_WIDTH), row_spec(A_WIDTH), row_spec(A_WIDTH),
            row_spec(B_WIDTH), nrow_spec(bk_rows, HEAD_DIM), nrow_spec(bv_rows, B_VDIM),
            row_spec(M_WIDTH), row_spec(MIX_WIDTH),
            head_t_spec(past_a), head_t_spec(past_a),
            pl.BlockSpec((1, N_HEADS, 2, HEAD_DIM, past_b), lambda n: (n, 0, 0, 0, 0)),
            cache_spec(past_b * bv_rows, B_VDIM),
            head_t_spec(N_MEM), head_t_spec(N_MEM),
            const((N_HEADS, t_new, past_a)), const((N_HEADS, t_new, t_new)),
            const((4, HEAD_DIM)), const((1, B_VDIM)),
            const((MIX_WIDTH, D_MODEL)), const((1, D_MODEL)), const((1, D_MODEL)),
        ],
        out_specs=row_spec(D_MODEL),
        compiler_params=_cparams(("arbitrary",)),
        name="sample_step",
    )(xs2, sq, sk, sv, tq, tk, tv, nq, g, cak, cav, cbk, cbv, cmk, cmv, biasc, biasn,
      lam_p, subln_g, w_out_b, lng, lnb)


def _rope_tables(pos):
    half = HEAD_DIM // 2
    inv = ROPE_THETA ** (-jnp.arange(half, dtype=F32) / half)
    ang = pos.astype(F32)[:, None] * inv[None, :]
    cos = jnp.cos(ang)
    sin = jnp.sin(ang)
    return jnp.tile(cos, (1, LANES // half)), jnp.concatenate([-sin, sin, -sin, sin], axis=-1)


def _rel_bias(table, dist):
    return table[:, jnp.clip(dist, -REL_CLIP, REL_CLIP) + REL_CLIP].astype(F32)


def _toeplitz_bias(table, n, width, offset):
    period = -(-(n + width) // LANES) * LANES
    u = jnp.arange(period)
    dist = jnp.where(u < width, offset - u, offset + period - u)
    diagonals = _rel_bias(table, dist)
    flat = jnp.tile(diagonals, (1, n))[:, :n * (period - 1)]
    return flat.reshape(table.shape[0], n, period - 1)[:, :, :width]


def _band_bias_prompt(table):
    n, width = MERGE_TILE, 3 * MERGE_TILE
    bias = _toeplitz_bias(table, n, width, BAND_ROWS)
    i = jnp.arange(n)[:, None]
    j = jnp.arange(width)[None, :]
    qc = i // CHUNK
    kc = j // CHUNK
    visible = (kc >= qc) & (kc <= qc + N_BAND_CHUNKS)
    return jnp.where(visible[None], bias, NEG_INF)


def kernel(x_prompt, x_sample, cache_a_k, cache_a_v, cache_b_k, cache_b_v, cache_mem_k, cache_mem_v, mem_prompt,
           w_in, w_mem_kv, a_rel_bias, diff_lambda, diff_subln_g, w_out, ln_g, ln_b):
    depth = w_in.shape[0]
    assert depth == 1, "single-layer step only"
    n_seq, seq, _ = x_prompt.shape
    n_streams, t_new, _ = x_sample.shape
    past_a = cache_a_k.shape[2]
    past_b = cache_b_k.shape[2]
    assert seq % ROW_TILE == 0 and BAND_ROWS == ROW_TILE == 2 * MERGE_TILE and past_a == BAND_ROWS
    assert n_streams * t_new == ROW_TILE
    layer = 0
    lambda_init = 0.8 - 0.6 * math.exp(-0.3 * layer)
    alpha = (2.0 * depth) ** 0.25

    w_in_b = w_in[layer].astype(BF16)
    w_mem_b = w_mem_kv[layer].astype(BF16)
    w_out_b = w_out[layer].astype(BF16)
    table = a_rel_bias[layer]
    lam_p = diff_lambda[layer]
    subln_g = diff_subln_g[layer].reshape(1, B_VDIM)
    lng = ln_g[layer].reshape(1, D_MODEL)
    lnb = ln_b[layer].reshape(1, D_MODEL)

    rows = n_seq * seq
    x2 = x_prompt.reshape(rows, D_MODEL)
    cos_p, sin_p = _rope_tables(jnp.arange(seq))
    (aqT, akb, avT, akf, avf, bqT, bkb, bvT, bkf, bvf, mqT, g) = _project(
        x2, w_in_b, cos_p, sin_p, prompt=True, rows_per_seq=seq)
    mk, mv, mkb, mvT = _mem_kv(mem_prompt.reshape(n_seq * N_MEM, D_MODEL), w_mem_b)
    ob = _diff_attention(bqT, bkb, bvT, lam_p, subln_g, n_seq=n_seq, seq=seq, lambda_init=lambda_init)
    biasT = jnp.swapaxes(_band_bias_prompt(table), 1, 2) * LOG2E
    y_p = _merge_prompt(x2, aqT, akb, avT, mqT, mkb, mvT, ob, g, biasT, w_out_b, lng, lnb,
                        seq=seq, alpha=alpha)

    xs2 = x_sample.reshape(n_streams * t_new, D_MODEL)
    pos_s = past_b + jnp.arange(t_new)
    cos_s, sin_s = _rope_tables(jnp.tile(pos_s, n_streams))
    sq, sk, sv, tq, tk, tv, nq, sg = _project(xs2, w_in_b, cos_s, sin_s, prompt=False, rows_per_seq=t_new)
    bias_s = _toeplitz_bias(table, t_new, past_a + t_new, past_a)
    biasc, biasn = bias_s[:, :, :past_a], bias_s[:, :, past_a:]
    y_s = _sample_step(
        xs2, sq, sk, sv, tq, tk, tv, nq, sg,
        jnp.transpose(cache_a_k[layer], (0, 2, 3, 1)),
        jnp.transpose(cache_a_v[layer], (0, 2, 3, 1)),
        jnp.transpose(cache_b_k[layer], (0, 2, 3, 4, 1)),
        cache_b_v[layer].reshape(n_streams, past_b * N_HEADS, B_VDIM),
        jnp.transpose(cache_mem_k[layer], (0, 2, 3, 1)),
        jnp.transpose(cache_mem_v[layer], (0, 2, 3, 1)),
        biasc, biasn, lam_p, subln_g, w_out_b, lng, lnb,
        n_streams=n_streams, t_new=t_new, lambda_init=lambda_init, alpha=alpha)

    hd = (N_HEADS, HEAD_DIM)
    return (
        y_p.reshape(n_seq, seq, D_MODEL),
        y_s.reshape(n_streams, t_new, D_MODEL),
        akf.reshape(1, n_seq, BAND_ROWS, *hd),
        avf.reshape(1, n_seq, BAND_ROWS, *hd),
        jnp.transpose(bkf, (0, 4, 1, 2, 3))[None],
        bvf.reshape(1, n_seq, seq, N_HEADS, B_VDIM),
        mk.reshape(1, n_seq, N_MEM, *hd),
        mv.reshape(1, n_seq, N_MEM, *hd),
        sk.reshape(1, n_streams, t_new, *hd),
        sv.reshape(1, n_streams, t_new, *hd),
        tk.reshape(1, n_streams, t_new, N_HEADS, 2, HEAD_DIM),
        tv.reshape(1, n_streams, t_new, N_HEADS, B_VDIM),
    )
```

```python
import functools
import math

import jax
import jax.numpy as jnp
from jax import lax
from jax.experimental import pallas as pl
from jax.experimental.pallas import tpu as pltpu

F32 = jnp.float32
BF16 = jnp.bfloat16

D_MODEL = 1024
CHUNK = 64
N_BAND_CHUNKS = 8
BAND_ROWS = N_BAND_CHUNKS * CHUNK
HEAD_DIM = 64
A_WIDTH = 256
B_WIDTH = 512
B_VDIM = 128
M_WIDTH = 256
N_MEM = 256
N_HEADS = 4
MIX_WIDTH = A_WIDTH + B_WIDTH + M_WIDTH
REL_CLIP = 128
ROPE_THETA = 10000.0
LN_EPS = 1e-5
RMS_EPS = 1e-5
NEG_INF = -1e30
QK_SCALE = HEAD_DIM ** -0.5
LOG2E = math.log2(math.e)

_OFF = {}
_o = 0
for _name, _w in (("aq", A_WIDTH), ("ak", A_WIDTH), ("av", A_WIDTH), ("bq", B_WIDTH), ("bk", B_WIDTH),
                  ("bv", B_WIDTH), ("mq", M_WIDTH), ("gate", MIX_WIDTH)):
    _OFF[_name] = (_o, _o + _w)
    _o += _w
PROJ_TOTAL = _o

LANES = 128
ROW_TILE = 512
KV_TILE = ROW_TILE // 2
ONES_ROWS = 16
MERGE_TILE = 256
VMEM_LIMIT = 56 * 1024 * 1024


def _cparams(sem):
    return pltpu.CompilerParams(dimension_semantics=sem, vmem_limit_bytes=VMEM_LIMIT)


def _rope_slab(x, cos, sin_signed, lo_half):
    left = pltpu.roll(x, LANES - 32, 1)
    right = pltpu.roll(x, 32, 1)
    swapped = jnp.where(lo_half, left, right)
    return x * cos + swapped * sin_signed


def _proj_kernel(x_ref, w_ref, cos_ref, sin_ref, *out_refs, prompt):
    if prompt:
        (aq_ref, akb_ref, avb_ref, akf_ref, avf_ref, bqT_ref, bkb_ref, bvT_ref,
         bkf_ref, bvf_ref, mq_ref, g_ref) = out_refs
    else:
        aq_ref, akf_ref, avf_ref, bq_ref, bkf_ref, bvf_ref, mq_ref, g_ref = out_refs
    xb = x_ref[...].astype(BF16)

    def seg(name):
        lo, hi = _OFF[name]
        return jnp.dot(xb, w_ref[:, lo:hi], preferred_element_type=F32)

    aq = seg("aq")
    ak = seg("ak")
    av = seg("av")
    akf_ref[...] = ak
    avf_ref[...] = av
    if prompt:
        aq_ref[0] = (aq * (QK_SCALE * LOG2E)).T.astype(BF16)
        akb_ref[...] = ak.astype(BF16)
        avT = av.T.astype(BF16)
        for half in range(ROW_TILE // MERGE_TILE):
            avb_ref[half] = avT[:, half * MERGE_TILE:(half + 1) * MERGE_TILE]
    else:
        aq_ref[...] = (aq * QK_SCALE).astype(BF16)

    cos = cos_ref[...]
    sin = sin_ref[...]
    lane = lax.broadcasted_iota(jnp.int32, cos.shape, 1)
    lo_half = (lane % HEAD_DIM) < (HEAD_DIM // 2)
    bq = seg("bq")
    bk = seg("bk")
    bv = seg("bv")
    n_rows = bv.shape[0]
    for c in range(B_WIDTH // LANES):
        sl = slice(c * LANES, (c + 1) * LANES)
        q_c = _rope_slab(bq[:, sl], cos, sin, lo_half)
        k_c = _rope_slab(bk[:, sl], cos, sin, lo_half)
        bvf_ref[pl.ds(c, n_rows, stride=N_HEADS), :] = bv[:, sl]
        if prompt:
            kT_c = k_c.T
            bkf_ref[0, c, 0] = kT_c[:HEAD_DIM]
            bkf_ref[0, c, 1] = kT_c[HEAD_DIM:]
            bkb_ref[:, sl] = k_c.astype(BF16)
            bqT_ref[0, sl, :] = (q_c * (QK_SCALE * LOG2E)).T.astype(BF16)
            vT_c = bv[:, sl].T.astype(BF16)
            for half in range(ROW_TILE // KV_TILE):
                bvT_ref[half, sl, :] = vT_c[:, half * KV_TILE:(half + 1) * KV_TILE]
        else:
            bkf_ref[pl.ds(2 * c, n_rows, stride=2 * N_HEADS), :] = k_c[:, :HEAD_DIM]
            bkf_ref[pl.ds(2 * c + 1, n_rows, stride=2 * N_HEADS), :] = k_c[:, HEAD_DIM:]
            bq_ref[:, sl] = (q_c * QK_SCALE).astype(BF16)

    if prompt:
        mq_ref[0] = (seg("mq") * (QK_SCALE * LOG2E)).T.astype(BF16)
    else:
        mq_ref[...] = (seg("mq") * QK_SCALE).astype(BF16)
    gate = seg("gate")
    g_ref[...] = (gate / (1.0 + jnp.exp(-gate))).astype(BF16)


def _project(x2, w_in_b, cos, sin, *, prompt, rows_per_seq):
    rows = x2.shape[0]
    tm = ROW_TILE
    n_tiles = rows // tm
    pos_tiles = cos.shape[0] // tm
    row_spec = lambda w: pl.BlockSpec((tm, w), lambda i: (i, 0))
    bk_rows, bv_rows = 2 * N_HEADS, N_HEADS
    bkf_shape = jax.ShapeDtypeStruct((rows * bk_rows, HEAD_DIM), F32)
    bvf_shape = jax.ShapeDtypeStruct((rows * bv_rows, B_VDIM), F32)
    bkf_spec = pl.BlockSpec((tm * bk_rows, HEAD_DIM), lambda i: (i, 0))
    bvf_spec = pl.BlockSpec((tm * bv_rows, B_VDIM), lambda i: (i, 0))
    in_specs = [
        row_spec(D_MODEL),
        pl.BlockSpec((D_MODEL, PROJ_TOTAL), lambda i: (0, 0)),
        pl.BlockSpec((tm, LANES), lambda i: (i % pos_tiles, 0)),
        pl.BlockSpec((tm, LANES), lambda i: (i % pos_tiles, 0)),
    ]
    sds = jax.ShapeDtypeStruct
    if prompt:
        tiles_per_seq = rows_per_seq // tm
        n_seq = rows // rows_per_seq
        tail_spec = pl.BlockSpec((tm, A_WIDTH), lambda i: (i // tiles_per_seq, 0))
        t_spec = pl.BlockSpec((1, B_WIDTH, tm), lambda i: (i, 0, 0))
        kv_per_row_tile = tm // KV_TILE
        vt_spec = pl.BlockSpec((kv_per_row_tile, B_WIDTH, KV_TILE), lambda i: (i, 0, 0))
        bkf_shape = jax.ShapeDtypeStruct((n_seq, N_HEADS, 2, HEAD_DIM, rows_per_seq), F32)
        bkf_spec = pl.BlockSpec((1, N_HEADS, 2, HEAD_DIM, tm),
                                lambda i: (i // tiles_per_seq, 0, 0, 0, i % tiles_per_seq))
        qt_spec = pl.BlockSpec((1, A_WIDTH, tm), lambda i: (i, 0, 0))
        merge_per_row_tile = tm // MERGE_TILE
        avt_spec = pl.BlockSpec((merge_per_row_tile, A_WIDTH, MERGE_TILE), lambda i: (i, 0, 0))
        out_shape = (
            sds((n_tiles, A_WIDTH, tm), BF16), sds((rows, A_WIDTH), BF16),
            sds((n_tiles * merge_per_row_tile, A_WIDTH, MERGE_TILE), BF16),
            sds((n_seq * tm, A_WIDTH), F32), sds((n_seq * tm, A_WIDTH), F32),
            sds((n_tiles, B_WIDTH, tm), BF16), sds((rows, B_WIDTH), BF16),
            sds((n_tiles * kv_per_row_tile, B_WIDTH, KV_TILE), BF16),
            bkf_shape, bvf_shape,
            sds((n_tiles, M_WIDTH, tm), BF16), sds((rows, MIX_WIDTH), BF16),
        )
        out_specs = (
            qt_spec, row_spec(A_WIDTH), avt_spec, tail_spec, tail_spec,
            t_spec, row_spec(B_WIDTH), vt_spec, bkf_spec, bvf_spec,
            qt_spec, row_spec(MIX_WIDTH),
        )
    else:
        out_shape = (
            sds((rows, A_WIDTH), BF16), sds((rows, A_WIDTH), F32), sds((rows, A_WIDTH), F32),
            sds((rows, B_WIDTH), BF16), bkf_shape, bvf_shape,
            sds((rows, M_WIDTH), BF16), sds((rows, MIX_WIDTH), BF16),
        )
        out_specs = (
            row_spec(A_WIDTH), row_spec(A_WIDTH), row_spec(A_WIDTH),
            row_spec(B_WIDTH), bkf_spec, bvf_spec,
            row_spec(M_WIDTH), row_spec(MIX_WIDTH),
        )
    return pl.pallas_call(
        functools.partial(_proj_kernel, prompt=prompt),
        out_shape=out_shape,
        grid=(n_tiles,),
        in_specs=in_specs,
        out_specs=out_specs,
        compiler_params=_cparams(("arbitrary",)),
        name="proj_prompt" if prompt else "proj_sample",
    )(x2, w_in_b, cos, sin)


def _mem_kv_kernel(mem_ref, w_ref, mk_ref, mv_ref, mkb_ref, mvT_ref):
    kv = jnp.dot(mem_ref[...].astype(BF16), w_ref[...], preferred_element_type=F32)
    mk_ref[...] = kv[:, :M_WIDTH]
    mv_ref[...] = kv[:, M_WIDTH:]
    mkb_ref[...] = kv[:, :M_WIDTH].astype(BF16)
    mvT_ref[...] = kv[:, M_WIDTH:].T.astype(BF16)


def _mem_kv(mem2, w_b):
    rows = mem2.shape[0]
    tm = N_MEM
    spec = pl.BlockSpec((tm, M_WIDTH), lambda i: (i, 0))
    sds = jax.ShapeDtypeStruct
    return pl.pallas_call(
        _mem_kv_kernel,
        out_shape=(sds((rows, M_WIDTH), F32), sds((rows, M_WIDTH), F32),
                   sds((rows, M_WIDTH), BF16), sds((rows, N_MEM), BF16)),
        grid=(rows // tm,),
        in_specs=[pl.BlockSpec((tm, D_MODEL), lambda i: (i, 0)),
                  pl.BlockSpec((D_MODEL, 2 * M_WIDTH), lambda i: (0, 0))],
        out_specs=(spec, spec, spec, spec),
        compiler_params=_cparams(("arbitrary",)),
        name="mem_kv",
    )(mem2, w_b)


def _lambda_value(lp, lambda_init):
    a = jnp.sum(lp[0:1, :] * lp[1:2, :], axis=1, keepdims=True)
    b = jnp.sum(lp[2:3, :] * lp[3:4, :], axis=1, keepdims=True)
    return jnp.exp(a) - jnp.exp(b) + lambda_init


def _diff_post(o, g, lambda_init):
    ms = jnp.mean(o * o, axis=-1, keepdims=True)
    return o * lax.rsqrt(ms + RMS_EPS) * g * (1.0 - lambda_init)


def _diff_attn_kernel(qT_ref, k_ref, vT_ref, lam_ref, g_ref, o_ref,
                      s0_ref, s1_ref, p0_ref, p1_ref, a0_ref, a1_ref, t0_ref, t1_ref, m_ref, acc_ref,
                      *, tq, tk, lambda_init):
    n_q = qT_ref.shape[0]

    def query_maps(t):
        qT = qT_ref[t].astype(F32)
        row = lax.broadcasted_iota(jnp.int32, qT.shape, 0)
        return (jnp.where(row < HEAD_DIM, qT, 0.0).astype(BF16),
                jnp.where(row >= HEAD_DIM, qT, 0.0).astype(BF16))

    s_refs = (s0_ref, s1_ref)
    p_refs = (p0_ref, p1_ref)
    a_refs = (a0_ref, a1_ref)
    t_refs = (t0_ref, t1_ref)
    ones_rows = jnp.ones((ONES_ROWS, tk), BF16)

    every = slice(None)
    upper = slice(tq // 2, tq)

    def qk(q_maps, j, buf, visible=None, cols=every):
        k = k_ref[pl.ds(pl.multiple_of(j * tk, tk), tk), :]
        for mp in range(2):
            s = jnp.dot(k, q_maps[mp][:, cols], preferred_element_type=F32)
            if visible is not None:
                s = jnp.where(visible[:, cols], s, NEG_INF)
            s_refs[buf][mp, :, cols] = s
            t_refs[buf][mp, :, cols] = jnp.max(s, axis=0, keepdims=True)

    def softmax(buf, cols=every):
        for mp in range(2):
            m_old = m_ref[mp, :, cols]
            m_new = jnp.maximum(m_old, t_refs[buf][mp, :, cols])
            a_refs[buf][mp, :, cols] = jnp.exp2(m_old - m_new)
            m_ref[mp, :, cols] = m_new
            p_refs[buf][mp, :, cols] = jnp.exp2(s_refs[buf][mp, :, cols] - m_new).astype(BF16)

    def pv(j, buf, cols=every):
        vT = jnp.concatenate([vT_ref[j], ones_rows], axis=0)
        for mp in range(2):
            upd = jnp.dot(vT, p_refs[buf][mp, :, cols], preferred_element_type=F32)
            acc_ref[mp, :, cols] = a_refs[buf][mp, :, cols] * acc_ref[mp, :, cols] + upd

    def diag_visible(half):
        kc = lax.broadcasted_iota(jnp.int32, (tk, tq), 0) // CHUNK + half * (tk // CHUNK)
        qc = lax.broadcasted_iota(jnp.int32, (tk, tq), 1) // CHUNK
        return kc <= qc

    def tick_pair(q_maps, j, masked):
        pv(j - 2, 0)
        qk(q_maps, j, 0, diag_visible(0) if masked else None)
        softmax(1)
        pv(j - 1, 1)
        if masked:
            qk(q_maps, j + 1, 1, diag_visible(1), cols=upper)
        else:
            qk(q_maps, j + 1, 1)
        softmax(0)

    def start_tile(t, masked):
        q_maps = query_maps(t)
        acc_ref[...] = jnp.zeros_like(acc_ref)
        m_ref[...] = jnp.full(m_ref.shape, NEG_INF, F32)
        qk(q_maps, 0, 0, diag_visible(0) if masked else None)
        qk(q_maps, 1, 1, diag_visible(1) if masked else None)
        softmax(0)

    lam = _lambda_value(lam_ref[...], lambda_init)

    def query_tile(t, carry):
        q_maps = query_maps(t)

        def body(i, c):
            for u in range(3):
                tick_pair(q_maps, 2 + 2 * u + 6 * i, False)
            return c

        n_plain = jnp.maximum(t - 1, 0)
        lax.fori_loop(0, n_plain // 3, body, 0)
        left = n_plain % 3

        @pl.when((left == 0) & (t > 0))
        def _():
            tick_pair(q_maps, 2 * t, True)

        @pl.when(left == 1)
        def _():
            tick_pair(q_maps, 2 * t - 2, False)
            tick_pair(q_maps, 2 * t, True)

        @pl.when(left == 2)
        def _():
            tick_pair(q_maps, 2 * t - 4, False)
            tick_pair(q_maps, 2 * t - 2, False)
            tick_pair(q_maps, 2 * t, True)

        pv(2 * t, 0)
        softmax(1, cols=upper)
        pv(2 * t + 1, 1, cols=upper)
        inv0 = 1.0 / acc_ref[0, B_VDIM:B_VDIM + 1]
        inv1 = 1.0 / acc_ref[1, B_VDIM:B_VDIM + 1]
        oT = acc_ref[0, :B_VDIM] * inv0 - lam * (acc_ref[1, :B_VDIM] * inv1)
        o_ref[pl.ds(pl.multiple_of(t * tq, tq), tq), :] = _diff_post(oT.T, g_ref[...], lambda_init)
        start_tile(jnp.minimum(t + 1, n_q - 1), False)
        return carry

    start_tile(0, True)
    lax.fori_loop(0, n_q, query_tile, 0)


def _diff_attention(bqT, bkb, bvT, lam_p, subln_g, *, n_seq, seq, lambda_init):
    tq, tk = ROW_TILE, KV_TILE
    nq = seq // tq
    nk = seq // tk
    rows = n_seq * seq
    bvT4 = bvT.reshape(n_seq, nk, B_WIDTH, tk)
    return pl.pallas_call(
        functools.partial(_diff_attn_kernel, tq=tq, tk=tk, lambda_init=lambda_init),
        out_shape=jax.ShapeDtypeStruct((rows, B_WIDTH), F32),
        grid=(n_seq, N_HEADS),
        in_specs=[
            pl.BlockSpec((nq, B_VDIM, tq), lambda b, h: (b, h, 0)),
            pl.BlockSpec((seq, B_VDIM), lambda b, h: (b, h)),
            pl.BlockSpec((None, nk, B_VDIM, tk), lambda b, h: (b, 0, h, 0)),
            pl.BlockSpec((4, HEAD_DIM), lambda b, h: (0, 0)),
            pl.BlockSpec((1, B_VDIM), lambda b, h: (0, 0)),
        ],
        out_specs=pl.BlockSpec((seq, B_VDIM), lambda b, h: (b, h)),
        scratch_shapes=[
            pltpu.VMEM((2, tk, tq), F32), pltpu.VMEM((2, tk, tq), F32),
            pltpu.VMEM((2, tk, tq), BF16), pltpu.VMEM((2, tk, tq), BF16),
            pltpu.VMEM((2, 1, tq), F32), pltpu.VMEM((2, 1, tq), F32),
            pltpu.VMEM((2, 1, tq), F32), pltpu.VMEM((2, 1, tq), F32),
            pltpu.VMEM((2, 1, tq), F32),
            pltpu.VMEM((2, B_VDIM + ONES_ROWS, tq), F32),
        ],
        compiler_params=_cparams(("arbitrary", "arbitrary")),
        name="diff_attn",
    )(bqT, bkb, bvT4, lam_p, subln_g)


def _head_masks(width):
    lane = lax.broadcasted_iota(jnp.int32, (1, width), 1)
    return [(lane >= h * HEAD_DIM) & (lane < (h + 1) * HEAD_DIM) for h in range(N_HEADS)]


def _dot_nt(a, b):
    return lax.dot_general(a, b, (((1,), (1,)), ((), ())), preferred_element_type=F32)


def _softmax_parts(blocks):
    m = functools.reduce(jnp.maximum, [jnp.max(s, axis=-1, keepdims=True) for s in blocks])
    ps = [jnp.exp(s - m) for s in blocks]
    l = functools.reduce(jnp.add, [jnp.sum(p, axis=-1, keepdims=True) for p in ps])
    return ps, l


def _softmax_pv(blocks, vals):
    ps, l = _softmax_parts(blocks)
    o = functools.reduce(jnp.add, [jnp.dot(p.astype(BF16), vv, preferred_element_type=F32)
                                   for p, vv in zip(ps, vals)])
    return o * (1.0 / l)


def _gate_mix(mixed, g):
    return (mixed * g.astype(F32)).astype(BF16)


def _out_norm(x, mg, w_out_ref, lng, lnb, alpha):
    return _residual_norm(x, jnp.dot(mg, w_out_ref[...], preferred_element_type=F32), lng, lnb, alpha)


def _residual_norm(x, y, lng, lnb, alpha):
    z = alpha * x + y
    mu = jnp.mean(z, axis=-1, keepdims=True)
    zc = z - mu
    var = jnp.mean(zc * zc, axis=-1, keepdims=True)
    return zc * lax.rsqrt(var + LN_EPS) * lng + lnb


def _softmax_pv_t(blocks, vT_blocks, ones_rows):
    m = functools.reduce(jnp.maximum, [jnp.max(s, axis=0, keepdims=True) for s in blocks])
    d = vT_blocks[0].shape[0]
    acc = functools.reduce(jnp.add, [
        jnp.dot(jnp.concatenate([vT, ones_rows], axis=0), jnp.exp2(s - m).astype(BF16),
                preferred_element_type=F32)
        for s, vT in zip(blocks, vT_blocks)])
    return acc[:d] * (1.0 / acc[d:d + 1])


def _merge_kernel(xprev_ref, aqT_ref, k0_ref, k1_ref, k2_ref, vT0_ref, vT1_ref, vT2_ref, mqT_ref, mk_ref, mvT_ref,
                  ob_ref, g_ref, biasT_ref, wout_ref, lng_ref, lnb_ref, y_ref, mg_ref,
                  *, tiles_per_seq, n_tiles, alpha):
    step = pl.program_id(0)

    @pl.when(step == 0)
    def _():
        mg_ref[...] = jnp.zeros_like(mg_ref)

    t = jnp.minimum(step, n_tiles - 1) % tiles_per_seq
    tile = MERGE_TILE
    aqT = aqT_ref[0].astype(F32)
    mqT = mqT_ref[0].astype(F32)
    row = lax.broadcasted_iota(jnp.int32, aqT.shape, 0)
    keys = [k0_ref[...], k1_ref[...], k2_ref[...]]
    vTs = [vT0_ref[0], vT1_ref[0], vT2_ref[0]]
    mk = mk_ref[...]
    mvT = mvT_ref[...]
    ones_rows = jnp.ones((ONES_ROWS, tile), BF16)
    pen0 = jnp.where(t >= 2, 0.0, NEG_INF).astype(F32)
    pen1 = jnp.where(t >= 1, 0.0, NEG_INF).astype(F32)
    pens = (pen0, pen1, None)
    a_scores, m_scores = [], []
    for h in range(N_HEADS):
        in_head = (row >= h * HEAD_DIM) & (row < (h + 1) * HEAD_DIM)
        qh = jnp.where(in_head, aqT, 0.0).astype(BF16)
        blocks = []
        for b in range(3):
            s = jnp.dot(keys[b], qh, preferred_element_type=F32) + biasT_ref[h, b * tile:(b + 1) * tile, :]
            blocks.append(s if pens[b] is None else s + pens[b])
        a_scores.append(blocks)
        qmh = jnp.where(in_head, mqT, 0.0).astype(BF16)
        m_scores.append([jnp.dot(mk, qmh, preferred_element_type=F32)])
    y_ref[...] = _out_norm(xprev_ref[...], mg_ref[...], wout_ref, lng_ref[...], lnb_ref[...], alpha)
    o_aT, o_mT = [], []
    for h in range(N_HEADS):
        hs = slice(h * HEAD_DIM, (h + 1) * HEAD_DIM)
        o_aT.append(_softmax_pv_t(a_scores[h], [vT[hs] for vT in vTs], ones_rows))
        o_mT.append(_softmax_pv_t(m_scores[h], [mvT[hs]], ones_rows))
    o_a = jnp.concatenate(o_aT, axis=0).T
    o_m = jnp.concatenate(o_mT, axis=0).T
    mixed = jnp.concatenate([o_a, ob_ref[...], o_m], axis=1)
    mg_ref[...] = _gate_mix(mixed, g_ref[...])


def _merge_prompt(x2, aqT, akb, avT, mqT, mkb, mvT, ob, g, biasT, w_out_b, lng, lnb, *, seq, alpha):
    rows = x2.shape[0]
    tile = MERGE_TILE
    tps = seq // tile
    n_tiles = rows // tile
    per_row_tile = ROW_TILE // tile

    cur = lambda s: jnp.minimum(s, n_tiles - 1)
    done = lambda s: jnp.maximum(s - 1, 0)

    def band(d):
        return lambda s: (cur(s) // tps) * tps + jnp.maximum(cur(s) % tps - d, 0)

    cur_spec = lambda w: pl.BlockSpec((tile, w), lambda s: (cur(s), 0))
    done_spec = pl.BlockSpec((tile, D_MODEL), lambda s: (done(s), 0))
    qt_spec = pl.BlockSpec((1, A_WIDTH, tile), lambda s: (cur(s) // per_row_tile, 0, cur(s) % per_row_tile))
    k_specs = [pl.BlockSpec((tile, A_WIDTH), lambda s, f=band(d): (f(s), 0)) for d in (2, 1, 0)]
    vt_specs = [pl.BlockSpec((1, A_WIDTH, tile), lambda s, f=band(d): (f(s), 0, 0)) for d in (2, 1, 0)]
    mem_spec = pl.BlockSpec((N_MEM, M_WIDTH), lambda s: (cur(s) // tps, 0))
    const = lambda shape: pl.BlockSpec(shape, lambda s: (0,) * len(shape))
    return pl.pallas_call(
        functools.partial(_merge_kernel, tiles_per_seq=tps, n_tiles=n_tiles, alpha=alpha),
        out_shape=jax.ShapeDtypeStruct((rows, D_MODEL), F32),
        grid=(n_tiles + 1,),
        in_specs=[done_spec, qt_spec] + k_specs + vt_specs + [
            qt_spec, mem_spec, mem_spec,
            cur_spec(B_WIDTH), cur_spec(MIX_WIDTH),
            const((N_HEADS, 3 * tile, tile)), const((MIX_WIDTH, D_MODEL)), const((1, D_MODEL)), const((1, D_MODEL)),
        ],
        out_specs=done_spec,
        scratch_shapes=[pltpu.VMEM((tile, MIX_WIDTH), BF16)],
        compiler_params=_cparams(("arbitrary",)),
        name="merge_prompt",
    )(x2, aqT, akb, akb, akb, avT, avT, avT, mqT, mkb, mvT, ob, g, biasT, w_out_b, lng, lnb)


def _sample_kernel(x_ref, sq_ref, sk_ref, sv_ref, tq_ref, tk_ref, tv_ref, nq_ref, g_ref,
                   cak_ref, cav_ref, cbk_ref, cbv_ref, cmk_ref, cmv_ref,
                   biasc_ref, biasn_ref, lam_ref, subg_ref, wout_ref, lng_ref, lnb_ref, y_ref,
                   *, lambda_init, alpha):
    masks = _head_masks(A_WIDTH)
    sq = sq_ref[...].astype(F32)
    nq = nq_ref[...].astype(F32)
    sk_new = sk_ref[...].astype(BF16)
    sv = sv_ref[...]
    rows = sq.shape[0]

    def head_cols(x, h):
        pair = x[:, (h // 2) * LANES:(h // 2 + 1) * LANES]
        if h % 2:
            pair = pltpu.roll(pair, HEAD_DIM, 1)
        return pair[:, :HEAD_DIM].astype(BF16)

    a_scores, m_scores = [], []
    for h in range(N_HEADS):
        qh = jnp.where(masks[h], sq, 0.0).astype(BF16)
        s_cache = jnp.dot(head_cols(sq, h), cak_ref[0, h].astype(BF16), preferred_element_type=F32)
        a_scores.append([s_cache + biasc_ref[h], _dot_nt(qh, sk_new) + biasn_ref[h]])
        m_scores.append([jnp.dot(head_cols(nq, h), cmk_ref[0, h].astype(BF16), preferred_element_type=F32)])
    past_b = cbv_ref.shape[1] // N_HEADS
    b_scores = []
    for h in range(N_HEADS):
        q = tq_ref[:, h * B_VDIM:(h + 1) * B_VDIM].astype(F32)
        q_maps = [q[:, :HEAD_DIM], pltpu.roll(q, HEAD_DIM, 1)[:, :HEAD_DIM]]
        for mp in range(2):
            q_m = q_maps[mp].astype(BF16)
            k_new = tk_ref[pl.ds(2 * h + mp, rows, stride=2 * N_HEADS), :].astype(BF16)
            b_scores.append([jnp.dot(q_m, cbk_ref[0, h, mp].astype(BF16), preferred_element_type=F32),
                             _dot_nt(q_m, k_new)])

    pieces = []
    for h in range(N_HEADS):
        ps, l = _softmax_parts(a_scores[h])
        o = (_dot_nt(ps[0].astype(BF16), cav_ref[0, h].astype(BF16))
             + jnp.dot(ps[1].astype(BF16), head_cols(sv, h), preferred_element_type=F32))
        pieces.append((o * (1.0 / l), h * HEAD_DIM))
        ps, l = _softmax_parts(m_scores[h])
        o = _dot_nt(ps[0].astype(BF16), cmv_ref[0, h].astype(BF16))
        pieces.append((o * (1.0 / l), A_WIDTH + B_WIDTH + h * HEAD_DIM))
    lam = _lambda_value(lam_ref[...], lambda_init)
    subg = subg_ref[...]
    for h in range(N_HEADS):
        vals = [cbv_ref[0, pl.ds(h, past_b, stride=N_HEADS), :].astype(BF16),
                tv_ref[pl.ds(h, rows, stride=N_HEADS), :].astype(BF16)]
        o_maps = [_softmax_pv(b_scores[2 * h + mp], vals) for mp in range(2)]
        pieces.append((_diff_post(o_maps[0] - lam * o_maps[1], subg, lambda_init), A_WIDTH + h * B_VDIM))

    g = g_ref[...].astype(F32)
    y = None
    for o, col in pieces:
        width = o.shape[1]
        if width == LANES:
            g_cols = g[:, col:col + LANES]
        else:
            g_pair = g[:, (col // LANES) * LANES:(col // LANES + 1) * LANES]
            if col % LANES:
                g_pair = pltpu.roll(g_pair, LANES - col % LANES, 1)
            g_cols = g_pair[:, :width]
        part = jnp.dot((o * g_cols).astype(BF16), wout_ref[col:col + width, :], preferred_element_type=F32)
        y = part if y is None else y + part
    y_ref[...] = _residual_norm(x_ref[...], y, lng_ref[...], lnb_ref[...], alpha)


def _sample_step(xs2, sq, sk, sv, tq, tk, tv, nq, g, cak, cav, cbk, cbv, cmk, cmv, biasc, biasn,
                 lam_p, subln_g, w_out_b, lng, lnb, *, n_streams, t_new, lambda_init, alpha):
    past_a = cav.shape[3]
    bk_rows, bv_rows = 2 * N_HEADS, N_HEADS
    head_t_spec = lambda r: pl.BlockSpec((1, N_HEADS, HEAD_DIM, r), lambda n: (n, 0, 0, 0))
    past_b = cbv.shape[1] // bv_rows
    row_spec = lambda w: pl.BlockSpec((t_new, w), lambda n: (n, 0))
    nrow_spec = lambda r, w: pl.BlockSpec((t_new * r, w), lambda n: (n, 0))
    cache_spec = lambda r, w: pl.BlockSpec((1, r, w), lambda n: (n, 0, 0))
    const = lambda shape: pl.BlockSpec(shape, lambda n: (0,) * len(shape))
    return pl.pallas_call(
        functools.partial(_sample_kernel, lambda_init=lambda_init, alpha=alpha),
        out_shape=jax.ShapeDtypeStruct((n_streams * t_new, D_MODEL), F32),
        grid=(n_streams,),
        in_specs=[
            row_spec(D_MODEL), row_spec(A_WIDTH), row_spec(A_WIDTH), row_spec(A_WIDTH),
            row_spec(B_WIDTH), nrow_spec(bk_rows, HEAD_DIM), nrow_spec(bv_rows, B_VDIM),
            row_spec(M_WIDTH), row_spec(MIX_WIDTH),
            head_t_spec(past_a), head_t_spec(past_a),
            pl.BlockSpec((1, N_HEADS, 2, HEAD_DIM, past_b), lambda n: (n, 0, 0, 0, 0)),
            cache_spec(past_b * bv_rows, B_VDIM),
            head_t_spec(N_MEM), head_t_spec(N_MEM),
            const((N_HEADS, t_new, past_a)), const((N_HEADS, t_new, t_new)),
            const((4, HEAD_DIM)), const((1, B_VDIM)),
            const((MIX_WIDTH, D_MODEL)), const((1, D_MODEL)), const((1, D_MODEL)),
        ],
        out_specs=row_spec(D_MODEL),
        compiler_params=_cparams(("arbitrary",)),
        name="sample_step",
    )(xs2, sq, sk, sv, tq, tk, tv, nq, g, cak, cav, cbk, cbv, cmk, cmv, biasc, biasn,
      lam_p, subln_g, w_out_b, lng, lnb)


def _rope_tables(pos):
    half = HEAD_DIM // 2
    inv = ROPE_THETA ** (-jnp.arange(half, dtype=F32) / half)
    ang = pos.astype(F32)[:, None] * inv[None, :]
    cos = jnp.cos(ang)
    sin = jnp.sin(ang)
    return jnp.tile(cos, (1, LANES // half)), jnp.concatenate([-sin, sin, -sin, sin], axis=-1)


def _rel_bias(table, dist):
    return table[:, jnp.clip(dist, -REL_CLIP, REL_CLIP) + REL_CLIP].astype(F32)


def _toeplitz_bias(table, n, width, offset):
    period = -(-(n + width) // LANES) * LANES
    u = jnp.arange(period)
    dist = jnp.where(u < width, offset - u, offset + period - u)
    diagonals = _rel_bias(table, dist)
    flat = jnp.tile(diagonals, (1, n))[:, :n * (period - 1)]
    return flat.reshape(table.shape[0], n, period - 1)[:, :, :width]


def _band_bias_prompt(table):
    n, width = MERGE_TILE, 3 * MERGE_TILE
    bias = _toeplitz_bias(table, n, width, BAND_ROWS)
    i = jnp.arange(n)[:, None]
    j = jnp.arange(width)[None, :]
    qc = i // CHUNK
    kc = j // CHUNK
    visible = (kc >= qc) & (kc <= qc + N_BAND_CHUNKS)
    return jnp.where(visible[None], bias, NEG_INF)


def kernel(x_prompt, x_sample, cache_a_k, cache_a_v, cache_b_k, cache_b_v, cache_mem_k, cache_mem_v, mem_prompt,
           w_in, w_mem_kv, a_rel_bias, diff_lambda, diff_subln_g, w_out, ln_g, ln_b):
    depth = w_in.shape[0]
    assert depth == 1, "single-layer step only"
    n_seq, seq, _ = x_prompt.shape
    n_streams, t_new, _ = x_sample.shape
    past_a = cache_a_k.shape[2]
    past_b = cache_b_k.shape[2]
    assert seq % ROW_TILE == 0 and BAND_ROWS == ROW_TILE == 2 * MERGE_TILE and past_a == BAND_ROWS
    assert n_streams * t_new == ROW_TILE
    layer = 0
    lambda_init = 0.8 - 0.6 * math.exp(-0.3 * layer)
    alpha = (2.0 * depth) ** 0.25

    w_in_b = w_in[layer].astype(BF16)
    w_mem_b = w_mem_kv[layer].astype(BF16)
    w_out_b = w_out[layer].astype(BF16)
    table = a_rel_bias[layer]
    lam_p = diff_lambda[layer]
    subln_g = diff_subln_g[layer].reshape(1, B_VDIM)
    lng = ln_g[layer].reshape(1, D_MODEL)
    lnb = ln_b[layer].reshape(1, D_MODEL)

    rows = n_seq * seq
    x2 = x_prompt.reshape(rows, D_MODEL)
    cos_p, sin_p = _rope_tables(jnp.arange(seq))
    (aqT, akb, avT, akf, avf, bqT, bkb, bvT, bkf, bvf, mqT, g) = _project(
        x2, w_in_b, cos_p, sin_p, prompt=True, rows_per_seq=seq)
    mk, mv, mkb, mvT = _mem_kv(mem_prompt.reshape(n_seq * N_MEM, D_MODEL), w_mem_b)
    ob = _diff_attention(bqT, bkb, bvT, lam_p, subln_g, n_seq=n_seq, seq=seq, lambda_init=lambda_init)
    biasT = jnp.swapaxes(_band_bias_prompt(table), 1, 2) * LOG2E
    y_p = _merge_prompt(x2, aqT, akb, avT, mqT, mkb, mvT, ob, g, biasT, w_out_b, lng, lnb,
                        seq=seq, alpha=alpha)

    xs2 = x_sample.reshape(n_streams * t_new, D_MODEL)
    pos_s = past_b + jnp.arange(t_new)
    cos_s, sin_s = _rope_tables(jnp.tile(pos_s, n_streams))
    sq, sk, sv, tq, tk, tv, nq, sg = _project(xs2, w_in_b, cos_s, sin_s, prompt=False, rows_per_seq=t_new)
    bias_s = _toeplitz_bias(table, t_new, past_a + t_new, past_a)
    biasc, biasn = bias_s[:, :, :past_a], bias_s[:, :, past_a:]
    y_s = _sample_step(
        xs2, sq, sk, sv, tq, tk, tv, nq, sg,
        jnp.transpose(cache_a_k[layer], (0, 2, 3, 1)),
        jnp.transpose(cache_a_v[layer], (0, 2, 3, 1)),
        jnp.transpose(cache_b_k[layer], (0, 2, 3, 4, 1)),
        cache_b_v[layer].reshape(n_streams, past_b * N_HEADS, B_VDIM),
        jnp.transpose(cache_mem_k[layer], (0, 2, 3, 1)),
        jnp.transpose(cache_mem_v[layer], (0, 2, 3, 1)),
        biasc, biasn, lam_p, subln_g, w_out_b, lng, lnb,
        n_streams=n_streams, t_new=t_new, lambda_init=lambda_init, alpha=alpha)

    hd = (N_HEADS, HEAD_DIM)
    return (
        y_p.reshape(n_seq, seq, D_MODEL),
        y_s.reshape(n_streams, t_new, D_MODEL),
        akf.reshape(1, n_seq, BAND_ROWS, *hd),
        avf.reshape(1, n_seq, BAND_ROWS, *hd),
        jnp.transpose(bkf, (0, 4, 1, 2, 3))[None],
        bvf.reshape(1, n_seq, seq, N_HEADS, B_VDIM),
        mk.reshape(1, n_seq, N_MEM, *hd),
        mv.reshape(1, n_seq, N_MEM, *hd),
        sk.reshape(1, n_streams, t_new, *hd),
        sv.reshape(1, n_streams, t_new, *hd),
        tk.reshape(1, n_streams, t_new, N_HEADS, 2, HEAD_DIM),
        tv.reshape(1, n_streams, t_new, N_HEADS, B_VDIM),
    )
```

```python
import functools
import math

import jax
import jax.numpy as jnp
from jax import lax
from jax.experimental import pallas as pl
from jax.experimental.pallas import tpu as pltpu

F32 = jnp.float32
BF16 = jnp.bfloat16

D_MODEL = 1024
CHUNK = 64
N_BAND_CHUNKS = 8
BAND_ROWS = N_BAND_CHUNKS * CHUNK
HEAD_DIM = 64
A_WIDTH = 256
B_WIDTH = 512
B_VDIM = 128
M_WIDTH = 256
N_MEM = 256
N_HEADS = 4
MIX_WIDTH = A_WIDTH + B_WIDTH + M_WIDTH
REL_CLIP = 128
ROPE_THETA = 10000.0
LN_EPS = 1e-5
RMS_EPS = 1e-5
NEG_INF = -1e30
QK_SCALE = HEAD_DIM ** -0.5
LOG2E = math.log2(math.e)

_OFF = {}
_o = 0
for _name, _w in (("aq", A_WIDTH), ("ak", A_WIDTH), ("av", A_WIDTH), ("bq", B_WIDTH), ("bk", B_WIDTH),
                  ("bv", B_WIDTH), ("mq", M_WIDTH), ("gate", MIX_WIDTH)):
    _OFF[_name] = (_o, _o + _w)
    _o += _w
PROJ_TOTAL = _o

LANES = 128
ROW_TILE = 512
KV_TILE = ROW_TILE // 2
ONES_ROWS = 16
MERGE_TILE = 256
VMEM_LIMIT = 56 * 1024 * 1024


def _cparams(sem):
    return pltpu.CompilerParams(dimension_semantics=sem, vmem_limit_bytes=VMEM_LIMIT)


def _rope_slab(x, cos, sin_signed, lo_half):
    left = pltpu.roll(x, LANES - 32, 1)
    right = pltpu.roll(x, 32, 1)
    swapped = jnp.where(lo_half, left, right)
    return x * cos + swapped * sin_signed


def _proj_kernel(x_ref, w_ref, cos_ref, sin_ref, *out_refs, prompt):
    if prompt:
        (aq_ref, akb_ref, avb_ref, akf_ref, avf_ref, bqT_ref, bkb_ref, bvT_ref,
         bkf_ref, bvf_ref, mq_ref, g_ref) = out_refs
    else:
        aq_ref, akf_ref, avf_ref, bq_ref, bkf_ref, bvf_ref, mq_ref, g_ref = out_refs
    xb = x_ref[...].astype(BF16)

    def seg(name):
        lo, hi = _OFF[name]
        return jnp.dot(xb, w_ref[:, lo:hi], preferred_element_type=F32)

    aq = seg("aq")
    ak = seg("ak")
    av = seg("av")
    akf_ref[...] = ak
    avf_ref[...] = av
    if prompt:
        aq_ref[0] = (aq * (QK_SCALE * LOG2E)).T.astype(BF16)
        akb_ref[...] = ak.astype(BF16)
        avT = av.T.astype(BF16)
        for half in range(ROW_TILE // MERGE_TILE):
            avb_ref[half] = avT[:, half * MERGE_TILE:(half + 1) * MERGE_TILE]
    else:
        aq_ref[...] = (aq * QK_SCALE).astype(BF16)

    cos = cos_ref[...]
    sin = sin_ref[...]
    lane = lax.broadcasted_iota(jnp.int32, cos.shape, 1)
    lo_half = (lane % HEAD_DIM) < (HEAD_DIM // 2)
    bq = seg("bq")
    bk = seg("bk")
    bv = seg("bv")
    n_rows = bv.shape[0]
    for c in range(B_WIDTH // LANES):
        sl = slice(c * LANES, (c + 1) * LANES)
        q_c = _rope_slab(bq[:, sl], cos, sin, lo_half)
        k_c = _rope_slab(bk[:, sl], cos, sin, lo_half)
        bvf_ref[pl.ds(c, n_rows, stride=N_HEADS), :] = bv[:, sl]
        if prompt:
            kT_c = k_c.T
            bkf_ref[0, c, 0] = kT_c[:HEAD_DIM]
            bkf_ref[0, c, 1] = kT_c[HEAD_DIM:]
            bkb_ref[:, sl] = k_c.astype(BF16)
            bqT_ref[0, sl, :] = (q_c * (QK_SCALE * LOG2E)).T.astype(BF16)
            vT_c = bv[:, sl].T.astype(BF16)
            for half in range(ROW_TILE // KV_TILE):
                bvT_ref[half, sl, :] = vT_c[:, half * KV_TILE:(half + 1) * KV_TILE]
        else:
            bkf_ref[pl.ds(2 * c, n_rows, stride=2 * N_HEADS), :] = k_c[:, :HEAD_DIM]
            bkf_ref[pl.ds(2 * c + 1, n_rows, stride=2 * N_HEADS), :] = k_c[:, HEAD_DIM:]
            bq_ref[:, sl] = (q_c * QK_SCALE).astype(BF16)

    if prompt:
        mq_ref[0] = (seg("mq") * (QK_SCALE * LOG2E)).T.astype(BF16)
    else:
        mq_ref[...] = (seg("mq") * QK_SCALE).astype(BF16)
    gate = seg("gate")
    g_ref[...] = (gate / (1.0 + jnp.exp(-gate))).astype(BF16)


def _project(x2, w_in_b, cos, sin, *, prompt, rows_per_seq):
    rows = x2.shape[0]
    tm = ROW_TILE
    n_tiles = rows // tm
    pos_tiles = cos.shape[0] // tm
    row_spec = lambda w: pl.BlockSpec((tm, w), lambda i: (i, 0))
    bk_rows, bv_rows = 2 * N_HEADS, N_HEADS
    bkf_shape = jax.ShapeDtypeStruct((rows * bk_rows, HEAD_DIM), F32)
    bvf_shape = jax.ShapeDtypeStruct((rows * bv_rows, B_VDIM), F32)
    bkf_spec = pl.BlockSpec((tm * bk_rows, HEAD_DIM), lambda i: (i, 0))
    bvf_spec = pl.BlockSpec((tm * bv_rows, B_VDIM), lambda i: (i, 0))
    in_specs = [
        row_spec(D_MODEL),
        pl.BlockSpec((D_MODEL, PROJ_TOTAL), lambda i: (0, 0)),
        pl.BlockSpec((tm, LANES), lambda i: (i % pos_tiles, 0)),
        pl.BlockSpec((tm, LANES), lambda i: (i % pos_tiles, 0)),
    ]
    sds = jax.ShapeDtypeStruct
    if prompt:
        tiles_per_seq = rows_per_seq // tm
        n_seq = rows // rows_per_seq
        tail_spec = pl.BlockSpec((tm, A_WIDTH), lambda i: (i // tiles_per_seq, 0))
        t_spec = pl.BlockSpec((1, B_WIDTH, tm), lambda i: (i, 0, 0))
        kv_per_row_tile = tm // KV_TILE
        vt_spec = pl.BlockSpec((kv_per_row_tile, B_WIDTH, KV_TILE), lambda i: (i, 0, 0))
        bkf_shape = jax.ShapeDtypeStruct((n_seq, N_HEADS, 2, HEAD_DIM, rows_per_seq), F32)
        bkf_spec = pl.BlockSpec((1, N_HEADS, 2, HEAD_DIM, tm),
                                lambda i: (i // tiles_per_seq, 0, 0, 0, i % tiles_per_seq))
        qt_spec = pl.BlockSpec((1, A_WIDTH, tm), lambda i: (i, 0, 0))
        merge_per_row_tile = tm // MERGE_TILE
        avt_spec = pl.BlockSpec((merge_per_row_tile, A_WIDTH, MERGE_TILE), lambda i: (i, 0, 0))
        out_shape = (
            sds((n_tiles, A_WIDTH, tm), BF16), sds((rows, A_WIDTH), BF16),
            sds((n_tiles * merge_per_row_tile, A_WIDTH, MERGE_TILE), BF16),
            sds((n_seq * tm, A_WIDTH), F32), sds((n_seq * tm, A_WIDTH), F32),
            sds((n_tiles, B_WIDTH, tm), BF16), sds((rows, B_WIDTH), BF16),
            sds((n_tiles * kv_per_row_tile, B_WIDTH, KV_TILE), BF16),
            bkf_shape, bvf_shape,
            sds((n_tiles, M_WIDTH, tm), BF16), sds((rows, MIX_WIDTH), BF16),
        )
        out_specs = (
            qt_spec, row_spec(A_WIDTH), avt_spec, tail_spec, tail_spec,
            t_spec, row_spec(B_WIDTH), vt_spec, bkf_spec, bvf_spec,
            qt_spec, row_spec(MIX_WIDTH),
        )
    else:
        out_shape = (
            sds((rows, A_WIDTH), BF16), sds((rows, A_WIDTH), F32), sds((rows, A_WIDTH), F32),
            sds((rows, B_WIDTH), BF16), bkf_shape, bvf_shape,
            sds((rows, M_WIDTH), BF16), sds((rows, MIX_WIDTH), BF16),
        )
        out_specs = (
            row_spec(A_WIDTH), row_spec(A_WIDTH), row_spec(A_WIDTH),
            row_spec(B_WIDTH), bkf_spec, bvf_spec,
            row_spec(M_WIDTH), row_spec(MIX_WIDTH),
        )
    return pl.pallas_call(
        functools.partial(_proj_kernel, prompt=prompt),
        out_shape=out_shape,
        grid=(n_tiles,),
        in_specs=in_specs,
        out_specs=out_specs,
        compiler_params=_cparams(("arbitrary",)),
        name="proj_prompt" if prompt else "proj_sample",
    )(x2, w_in_b, cos, sin)


def _mem_kv_kernel(mem_ref, w_ref, mk_ref, mv_ref, mkb_ref, mvT_ref):
    kv = jnp.dot(mem_ref[...].astype(BF16), w_ref[...], preferred_element_type=F32)
    mk_ref[...] = kv[:, :M_WIDTH]
    mv_ref[...] = kv[:, M_WIDTH:]
    mkb_ref[...] = kv[:, :M_WIDTH].astype(BF16)
    mvT_ref[...] = kv[:, M_WIDTH:].T.astype(BF16)


def _mem_kv(mem2, w_b):
    rows = mem2.shape[0]
    tm = N_MEM
    spec = pl.BlockSpec((tm, M_WIDTH), lambda i: (i, 0))
    sds = jax.ShapeDtypeStruct
    return pl.pallas_call(
        _mem_kv_kernel,
        out_shape=(sds((rows, M_WIDTH), F32), sds((rows, M_WIDTH), F32),
                   sds((rows, M_WIDTH), BF16), sds((rows, N_MEM), BF16)),
        grid=(rows // tm,),
        in_specs=[pl.BlockSpec((tm, D_MODEL), lambda i: (i, 0)),
                  pl.BlockSpec((D_MODEL, 2 * M_WIDTH), lambda i: (0, 0))],
        out_specs=(spec, spec, spec, spec),
        compiler_params=_cparams(("arbitrary",)),
        name="mem_kv",
    )(mem2, w_b)


def _lambda_value(lp, lambda_init):
    a = jnp.sum(lp[0:1, :] * lp[1:2, :], axis=1, keepdims=True)
    b = jnp.sum(lp[2:3, :] * lp[3:4, :], axis=1, keepdims=True)
    return jnp.exp(a) - jnp.exp(b) + lambda_init


def _diff_post(o, g, lambda_init):
    ms = jnp.mean(o * o, axis=-1, keepdims=True)
    return o * lax.rsqrt(ms + RMS_EPS) * g * (1.0 - lambda_init)


def _diff_attn_kernel(qT_ref, k_ref, vT_ref, lam_ref, g_ref, o_ref,
                      s0_ref, s1_ref, p0_ref, p1_ref, a0_ref, a1_ref, t0_ref, t1_ref, m_ref, acc_ref,
                      *, tq, tk, lambda_init):
    n_q = qT_ref.shape[0]

    def query_maps(t):
        qT = qT_ref[t].astype(F32)
        row = lax.broadcasted_iota(jnp.int32, qT.shape, 0)
        return (jnp.where(row < HEAD_DIM, qT, 0.0).astype(BF16),
                jnp.where(row >= HEAD_DIM, qT, 0.0).astype(BF16))

    s_refs = (s0_ref, s1_ref)
    p_refs = (p0_ref, p1_ref)
    a_refs = (a0_ref, a1_ref)
    t_refs = (t0_ref, t1_ref)
    ones_rows = jnp.ones((ONES_ROWS, tk), BF16)

    every = slice(None)
    upper = slice(tq // 2, tq)

    def qk(q_maps, j, buf, visible=None, cols=every):
        k = k_ref[pl.ds(pl.multiple_of(j * tk, tk), tk), :]
        for mp in range(2):
            s = jnp.dot(k, q_maps[mp][:, cols], preferred_element_type=F32)
            if visible is not None:
                s = jnp.where(visible[:, cols], s, NEG_INF)
            s_refs[buf][mp, :, cols] = s
            t_refs[buf][mp, :, cols] = jnp.max(s, axis=0, keepdims=True)

    def softmax(buf, cols=every):
        for mp in range(2):
            m_old = m_ref[mp, :, cols]
            m_new = jnp.maximum(m_old, t_refs[buf][mp, :, cols])
            a_refs[buf][mp, :, cols] = jnp.exp2(m_old - m_new)
            m_ref[mp, :, cols] = m_new
            p_refs[buf][mp, :, cols] = jnp.exp2(s_refs[buf][mp, :, cols] - m_new).astype(BF16)

    def pv(j, buf, cols=every):
        vT = jnp.concatenate([vT_ref[j], ones_rows], axis=0)
        for mp in range(2):
            upd = jnp.dot(vT, p_refs[buf][mp, :, cols], preferred_element_type=F32)
            acc_ref[mp, :, cols] = a_refs[buf][mp, :, cols] * acc_ref[mp, :, cols] + upd

    def diag_visible(half):
        kc = lax.broadcasted_iota(jnp.int32, (tk, tq), 0) // CHUNK + half * (tk // CHUNK)
        qc = lax.broadcasted_iota(jnp.int32, (tk, tq), 1) // CHUNK
        return kc <= qc

    def tick_pair(q_maps, j, masked):
        pv(j - 2, 0)
        qk(q_maps, j, 0, diag_visible(0) if masked else None)
        softmax(1)
        pv(j - 1, 1)
        if masked:
            qk(q_maps, j + 1, 1, diag_visible(1), cols=upper)
        else:
            qk(q_maps, j + 1, 1)
        softmax(0)

    def start_tile(t, masked):
        q_maps = query_maps(t)
        acc_ref[...] = jnp.zeros_like(acc_ref)
        m_ref[...] = jnp.full(m_ref.shape, NEG_INF, F32)
        qk(q_maps, 0, 0, diag_visible(0) if masked else None)
        qk(q_maps, 1, 1, diag_visible(1) if masked else None)
        softmax(0)

    lam = _lambda_value(lam_ref[...], lambda_init)

    def query_tile(t, carry):
        q_maps = query_maps(t)

        unroll = 4

        def body(i, c):
            for u in range(unroll):
                tick_pair(q_maps, 2 + 2 * u + 2 * unroll * i, False)
            return c

        n_plain = jnp.maximum(t - 1, 0)
        lax.fori_loop(0, n_plain // unroll, body, 0)
        left = n_plain % unroll

        @pl.when((left == 0) & (t > 0))
        def _():
            tick_pair(q_maps, 2 * t, True)

        for n_left in range(1, unroll):
            @pl.when(left == n_left)
            def _(n_left=n_left):
                for u in range(n_left, 0, -1):
                    tick_pair(q_maps, 2 * t - 2 * u, False)
                tick_pair(q_maps, 2 * t, True)

        pv(2 * t, 0)
        softmax(1, cols=upper)
        pv(2 * t + 1, 1, cols=upper)
        inv0 = 1.0 / acc_ref[0, B_VDIM:B_VDIM + 1]
        inv1 = 1.0 / acc_ref[1, B_VDIM:B_VDIM + 1]
        oT = acc_ref[0, :B_VDIM] * inv0 - lam * (acc_ref[1, :B_VDIM] * inv1)
        o_ref[pl.ds(pl.multiple_of(t * tq, tq), tq), :] = _diff_post(oT.T, g_ref[...], lambda_init)
        start_tile(jnp.minimum(t + 1, n_q - 1), False)
        return carry

    start_tile(0, True)
    lax.fori_loop(0, n_q, query_tile, 0)


def _diff_attention(bqT, bkb, bvT, lam_p, subln_g, *, n_seq, seq, lambda_init):
    tq, tk = ROW_TILE, KV_TILE
    nq = seq // tq
    nk = seq // tk
    rows = n_seq * seq
    bvT4 = bvT.reshape(n_seq, nk, B_WIDTH, tk)
    return pl.pallas_call(
        functools.partial(_diff_attn_kernel, tq=tq, tk=tk, lambda_init=lambda_init),
        out_shape=jax.ShapeDtypeStruct((rows, B_WIDTH), F32),
        grid=(n_seq, N_HEADS),
        in_specs=[
            pl.BlockSpec((nq, B_VDIM, tq), lambda b, h: (b, h, 0)),
            pl.BlockSpec((seq, B_VDIM), lambda b, h: (b, h)),
            pl.BlockSpec((None, nk, B_VDIM, tk), lambda b, h: (b, 0, h, 0)),
            pl.BlockSpec((4, HEAD_DIM), lambda b, h: (0, 0)),
            pl.BlockSpec((1, B_VDIM), lambda b, h: (0, 0)),
        ],
        out_specs=pl.BlockSpec((seq, B_VDIM), lambda b, h: (b, h)),
        scratch_shapes=[
            pltpu.VMEM((2, tk, tq), F32), pltpu.VMEM((2, tk, tq), F32),
            pltpu.VMEM((2, tk, tq), BF16), pltpu.VMEM((2, tk, tq), BF16),
            pltpu.VMEM((2, 1, tq), F32), pltpu.VMEM((2, 1, tq), F32),
            pltpu.VMEM((2, 1, tq), F32), pltpu.VMEM((2, 1, tq), F32),
            pltpu.VMEM((2, 1, tq), F32),
            pltpu.VMEM((2, B_VDIM + ONES_ROWS, tq), F32),
        ],
        compiler_params=_cparams(("arbitrary", "arbitrary")),
        name="diff_attn",
    )(bqT, bkb, bvT4, lam_p, subln_g)


def _head_masks(width):
    lane = lax.broadcasted_iota(jnp.int32, (1, width), 1)
    return [(lane >= h * HEAD_DIM) & (lane < (h + 1) * HEAD_DIM) for h in range(N_HEADS)]


def _dot_nt(a, b):
    return lax.dot_general(a, b, (((1,), (1,)), ((), ())), preferred_element_type=F32)


def _softmax_parts(blocks):
    m = functools.reduce(jnp.maximum, [jnp.max(s, axis=-1, keepdims=True) for s in blocks])
    ps = [jnp.exp(s - m) for s in blocks]
    l = functools.reduce(jnp.add, [jnp.sum(p, axis=-1, keepdims=True) for p in ps])
    return ps, l


def _softmax_pv(blocks, vals):
    ps, l = _softmax_parts(blocks)
    o = functools.reduce(jnp.add, [jnp.dot(p.astype(BF16), vv, preferred_element_type=F32)
                                   for p, vv in zip(ps, vals)])
    return o * (1.0 / l)


def _gate_mix(mixed, g):
    return (mixed * g.astype(F32)).astype(BF16)


def _out_norm(x, mg, w_out_ref, lng, lnb, alpha):
    return _residual_norm(x, jnp.dot(mg, w_out_ref[...], preferred_element_type=F32), lng, lnb, alpha)


def _residual_norm(x, y, lng, lnb, alpha):
    z = alpha * x + y
    mu = jnp.mean(z, axis=-1, keepdims=True)
    zc = z - mu
    var = jnp.mean(zc * zc, axis=-1, keepdims=True)
    return zc * lax.rsqrt(var + LN_EPS) * lng + lnb


def _softmax_pv_t(blocks, vT_blocks, ones_rows):
    m = functools.reduce(jnp.maximum, [jnp.max(s, axis=0, keepdims=True) for s in blocks])
    d = vT_blocks[0].shape[0]
    acc = functools.reduce(jnp.add, [
        jnp.dot(jnp.concatenate([vT, ones_rows], axis=0), jnp.exp2(s - m).astype(BF16),
                preferred_element_type=F32)
        for s, vT in zip(blocks, vT_blocks)])
    return acc[:d] * (1.0 / acc[d:d + 1])


def _merge_kernel(xprev_ref, aqT_ref, k0_ref, k1_ref, k2_ref, vT0_ref, vT1_ref, vT2_ref, mqT_ref, mk_ref, mvT_ref,
                  ob_ref, g_ref, biasT_ref, wout_ref, lng_ref, lnb_ref, y_ref, mg_ref,
                  *, tiles_per_seq, n_tiles, alpha):
    step = pl.program_id(0)

    @pl.when(step == 0)
    def _():
        mg_ref[...] = jnp.zeros_like(mg_ref)

    t = jnp.minimum(step, n_tiles - 1) % tiles_per_seq
    tile = MERGE_TILE
    aqT = aqT_ref[0].astype(F32)
    mqT = mqT_ref[0].astype(F32)
    row = lax.broadcasted_iota(jnp.int32, aqT.shape, 0)
    keys = [k0_ref[...], k1_ref[...], k2_ref[...]]
    vTs = [vT0_ref[0], vT1_ref[0], vT2_ref[0]]
    mk = mk_ref[...]
    mvT = mvT_ref[...]
    ones_rows = jnp.ones((ONES_ROWS, tile), BF16)
    pen0 = jnp.where(t >= 2, 0.0, NEG_INF).astype(F32)
    pen1 = jnp.where(t >= 1, 0.0, NEG_INF).astype(F32)
    pens = (pen0, pen1, None)
    a_scores, m_scores = [], []
    for h in range(N_HEADS):
        in_head = (row >= h * HEAD_DIM) & (row < (h + 1) * HEAD_DIM)
        qh = jnp.where(in_head, aqT, 0.0).astype(BF16)
        blocks = []
        for b in range(3):
            s = jnp.dot(keys[b], qh, preferred_element_type=F32) + biasT_ref[h, b * tile:(b + 1) * tile, :]
            blocks.append(s if pens[b] is None else s + pens[b])
        a_scores.append(blocks)
        qmh = jnp.where(in_head, mqT, 0.0).astype(BF16)
        m_scores.append([jnp.dot(mk, qmh, preferred_element_type=F32)])
    y_ref[...] = _out_norm(xprev_ref[...], mg_ref[...], wout_ref, lng_ref[...], lnb_ref[...], alpha)
    o_aT, o_mT = [], []
    for h in range(N_HEADS):
        hs = slice(h * HEAD_DIM, (h + 1) * HEAD_DIM)
        o_aT.append(_softmax_pv_t(a_scores[h], [vT[hs] for vT in vTs], ones_rows))
        o_mT.append(_softmax_pv_t(m_scores[h], [mvT[hs]], ones_rows))
    o_a = jnp.concatenate(o_aT, axis=0).T
    o_m = jnp.concatenate(o_mT, axis=0).T
    mixed = jnp.concatenate([o_a, ob_ref[...], o_m], axis=1)
    mg_ref[...] = _gate_mix(mixed, g_ref[...])


def _merge_prompt(x2, aqT, akb, avT, mqT, mkb, mvT, ob, g, biasT, w_out_b, lng, lnb, *, seq, alpha):
    rows = x2.shape[0]
    tile = MERGE_TILE
    tps = seq // tile
    n_tiles = rows // tile
    per_row_tile = ROW_TILE // tile

    cur = lambda s: jnp.minimum(s, n_tiles - 1)
    done = lambda s: jnp.maximum(s - 1, 0)

    def band(d):
        return lambda s: (cur(s) // tps) * tps + jnp.maximum(cur(s) % tps - d, 0)

    cur_spec = lambda w: pl.BlockSpec((tile, w), lambda s: (cur(s), 0))
    done_spec = pl.BlockSpec((tile, D_MODEL), lambda s: (done(s), 0))
    qt_spec = pl.BlockSpec((1, A_WIDTH, tile), lambda s: (cur(s) // per_row_tile, 0, cur(s) % per_row_tile))
    k_specs = [pl.BlockSpec((tile, A_WIDTH), lambda s, f=band(d): (f(s), 0)) for d in (2, 1, 0)]
    vt_specs = [pl.BlockSpec((1, A_WIDTH, tile), lambda s, f=band(d): (f(s), 0, 0)) for d in (2, 1, 0)]
    mem_spec = pl.BlockSpec((N_MEM, M_WIDTH), lambda s: (cur(s) // tps, 0))
    const = lambda shape: pl.BlockSpec(shape, lambda s: (0,) * len(shape))
    return pl.pallas_call(
        functools.partial(_merge_kernel, tiles_per_seq=tps, n_tiles=n_tiles, alpha=alpha),
        out_shape=jax.ShapeDtypeStruct((rows, D_MODEL), F32),
        grid=(n_tiles + 1,),
        in_specs=[done_spec, qt_spec] + k_specs + vt_specs + [
            qt_spec, mem_spec, mem_spec,
            cur_spec(B_WIDTH), cur_spec(MIX_WIDTH),
            const((N_HEADS, 3 * tile, tile)), const((MIX_WIDTH, D_MODEL)), const((1, D_MODEL)), const((1, D_MODEL)),
        ],
        out_specs=done_spec,
        scratch_shapes=[pltpu.VMEM((tile, MIX_WIDTH), BF16)],
        compiler_params=_cparams(("arbitrary",)),
        name="merge_prompt",
    )(x2, aqT, akb, akb, akb, avT, avT, avT, mqT, mkb, mvT, ob, g, biasT, w_out_b, lng, lnb)


def _sample_kernel(x_ref, sq_ref, sk_ref, sv_ref, tq_ref, tk_ref, tv_ref, nq_ref, g_ref,
                   cak_ref, cav_ref, cbk_ref, cbv_ref, cmk_ref, cmv_ref,
                   biasc_ref, biasn_ref, lam_ref, subg_ref, wout_ref, lng_ref, lnb_ref, y_ref,
                   *, lambda_init, alpha):
    masks = _head_masks(A_WIDTH)
    sq = sq_ref[...].astype(F32)
    nq = nq_ref[...].astype(F32)
    sk_new = sk_ref[...].astype(BF16)
    sv = sv_ref[...]
    rows = sq.shape[0]

    def head_cols(x, h):
        pair = x[:, (h // 2) * LANES:(h // 2 + 1) * LANES]
        if h % 2:
            pair = pltpu.roll(pair, HEAD_DIM, 1)
        return pair[:, :HEAD_DIM].astype(BF16)

    a_scores, m_scores = [], []
    for h in range(N_HEADS):
        qh = jnp.where(masks[h], sq, 0.0).astype(BF16)
        s_cache = jnp.dot(head_cols(sq, h), cak_ref[0, h].astype(BF16), preferred_element_type=F32)
        a_scores.append([s_cache + biasc_ref[h], _dot_nt(qh, sk_new) + biasn_ref[h]])
        m_scores.append([jnp.dot(head_cols(nq, h), cmk_ref[0, h].astype(BF16), preferred_element_type=F32)])
    past_b = cbv_ref.shape[1] // N_HEADS
    b_scores = []
    for h in range(N_HEADS):
        q = tq_ref[:, h * B_VDIM:(h + 1) * B_VDIM].astype(F32)
        q_maps = [q[:, :HEAD_DIM], pltpu.roll(q, HEAD_DIM, 1)[:, :HEAD_DIM]]
        for mp in range(2):
            q_m = q_maps[mp].astype(BF16)
            k_new = tk_ref[pl.ds(2 * h + mp, rows, stride=2 * N_HEADS), :].astype(BF16)
            b_scores.append([jnp.dot(q_m, cbk_ref[0, h, mp].astype(BF16), preferred_element_type=F32),
                             _dot_nt(q_m, k_new)])

    pieces = []
    for h in range(N_HEADS):
        ps, l = _softmax_parts(a_scores[h])
        o = (_dot_nt(ps[0].astype(BF16), cav_ref[0, h].astype(BF16))
             + jnp.dot(ps[1].astype(BF16), head_cols(sv, h), preferred_element_type=F32))
        pieces.append((o * (1.0 / l), h * HEAD_DIM))
        ps, l = _softmax_parts(m_scores[h])
        o = _dot_nt(ps[0].astype(BF16), cmv_ref[0, h].astype(BF16))
        pieces.append((o * (1.0 / l), A_WIDTH + B_WIDTH + h * HEAD_DIM))
    lam = _lambda_value(lam_ref[...], lambda_init)
    subg = subg_ref[...]
    for h in range(N_HEADS):
        vals = [cbv_ref[0, pl.ds(h, past_b, stride=N_HEADS), :].astype(BF16),
                tv_ref[pl.ds(h, rows, stride=N_HEADS), :].astype(BF16)]
        o_maps = [_softmax_pv(b_scores[2 * h + mp], vals) for mp in range(2)]
        pieces.append((_diff_post(o_maps[0] - lam * o_maps[1], subg, lambda_init), A_WIDTH + h * B_VDIM))

    g = g_ref[...].astype(F32)
    y = None
    for o, col in pieces:
        width = o.shape[1]
        if width == LANES:
            g_cols = g[:, col:col + LANES]
        else:
            g_pair = g[:, (col // LANES) * LANES:(col // LANES + 1) * LANES]
            if col % LANES:
                g_pair = pltpu.roll(g_pair, LANES - col % LANES, 1)
            g_cols = g_pair[:, :width]
        part = jnp.dot((o * g_cols).astype(BF16), wout_ref[col:col + width, :], preferred_element_type=F32)
        y = part if y is None else y + part
    y_ref[...] = _residual_norm(x_ref[...], y, lng_ref[...], lnb_ref[...], alpha)


def _sample_step(xs2, sq, sk, sv, tq, tk, tv, nq, g, cak, cav, cbk, cbv, cmk, cmv, biasc, biasn,
                 lam_p, subln_g, w_out_b, lng, lnb, *, n_streams, t_new, lambda_init, alpha):
    past_a = cav.shape[3]
    bk_rows, bv_rows = 2 * N_HEADS, N_HEADS
    head_t_spec = lambda r: pl.BlockSpec((1, N_HEADS, HEAD_DIM, r), lambda n: (n, 0, 0, 0))
    past_b = cbv.shape[1] // bv_rows
    row_spec = lambda w: pl.BlockSpec((t_new, w), lambda n: (n, 0))
    nrow_spec = lambda r, w: pl.BlockSpec((t_new * r, w), lambda n: (n, 0))
    cache_spec = lambda r, w: pl.BlockSpec((1, r, w), lambda n: (n, 0, 0))
    const = lambda shape: pl.BlockSpec(shape, lambda n: (0,) * len(shape))
    return pl.pallas_call(
        functools.partial(_sample_kernel, lambda_init=lambda_init, alpha=alpha),
        out_shape=jax.ShapeDtypeStruct((n_streams * t_new, D_MODEL), F32),
        grid=(n_streams,),
        in_specs=[
            row_spec(D_MODEL), row_spec(A_WIDTH), row_spec(A_WIDTH), row_spec(A_WIDTH),
            row_spec(B_WIDTH), nrow_spec(bk_rows, HEAD_DIM), nrow_spec(bv_rows, B_VDIM),
            row_spec(M_WIDTH), row_spec(MIX_WIDTH),
            head_t_spec(past_a), head_t_spec(past_a),
            pl.BlockSpec((1, N_HEADS, 2, HEAD_DIM, past_b), lambda n: (n, 0, 0, 0, 0)),
            cache_spec(past_b * bv_rows, B_VDIM),
            head_t_spec(N_MEM), head_t_spec(N_MEM),
            const((N_HEADS, t_new, past_a)), const((N_HEADS, t_new, t_new)),
            const((4, HEAD_DIM)), const((1, B_VDIM)),
            const((MIX_WIDTH, D_MODEL)), const((1, D_MODEL)), const((1, D_MODEL)),
        ],
        out_specs=row_spec(D_MODEL),
        compiler_params=_cparams(("arbitrary",)),
        name="sample_step",
    )(xs2, sq, sk, sv, tq, tk, tv, nq, g, cak, cav, cbk, cbv, cmk, cmv, biasc, biasn,
      lam_p, subln_g, w_out_b, lng, lnb)


def _rope_tables(pos):
    half = HEAD_DIM // 2
    inv = ROPE_THETA ** (-jnp.arange(half, dtype=F32) / half)
    ang = pos.astype(F32)[:, None] * inv[None, :]
    cos = jnp.cos(ang)
    sin = jnp.sin(ang)
    return jnp.tile(cos, (1, LANES // half)), jnp.concatenate([-sin, sin, -sin, sin], axis=-1)


def _rel_bias(table, dist):
    return table[:, jnp.clip(dist, -REL_CLIP, REL_CLIP) + REL_CLIP].astype(F32)


def _toeplitz_bias(table, n, width, offset):
    period = -(-(n + width) // LANES) * LANES
    u = jnp.arange(period)
    dist = jnp.where(u < width, offset - u, offset + period - u)
    diagonals = _rel_bias(table, dist)
    flat = jnp.tile(diagonals, (1, n))[:, :n * (period - 1)]
    return flat.reshape(table.shape[0], n, period - 1)[:, :, :width]


def _band_bias_prompt(table):
    n, width = MERGE_TILE, 3 * MERGE_TILE
    bias = _toeplitz_bias(table, n, width, BAND_ROWS)
    i = jnp.arange(n)[:, None]
    j = jnp.arange(width)[None, :]
    qc = i // CHUNK
    kc = j // CHUNK
    visible = (kc >= qc) & (kc <= qc + N_BAND_CHUNKS)
    return jnp.where(visible[None], bias, NEG_INF)


def kernel(x_prompt, x_sample, cache_a_k, cache_a_v, cache_b_k, cache_b_v, cache_mem_k, cache_mem_v, mem_prompt,
           w_in, w_mem_kv, a_rel_bias, diff_lambda, diff_subln_g, w_out, ln_g, ln_b):
    depth = w_in.shape[0]
    assert depth == 1, "single-layer step only"
    n_seq, seq, _ = x_prompt.shape
    n_streams, t_new, _ = x_sample.shape
    past_a = cache_a_k.shape[2]
    past_b = cache_b_k.shape[2]
    assert seq % ROW_TILE == 0 and BAND_ROWS == ROW_TILE == 2 * MERGE_TILE and past_a == BAND_ROWS
    assert n_streams * t_new == ROW_TILE
    layer = 0
    lambda_init = 0.8 - 0.6 * math.exp(-0.3 * layer)
    alpha = (2.0 * depth) ** 0.25

    w_in_b = w_in[layer].astype(BF16)
    w_mem_b = w_mem_kv[layer].astype(BF16)
    w_out_b = w_out[layer].astype(BF16)
    table = a_rel_bias[layer]
    lam_p = diff_lambda[layer]
    subln_g = diff_subln_g[layer].reshape(1, B_VDIM)
    lng = ln_g[layer].reshape(1, D_MODEL)
    lnb = ln_b[layer].reshape(1, D_MODEL)

    rows = n_seq * seq
    x2 = x_prompt.reshape(rows, D_MODEL)
    cos_p, sin_p = _rope_tables(jnp.arange(seq))
    (aqT, akb, avT, akf, avf, bqT, bkb, bvT, bkf, bvf, mqT, g) = _project(
        x2, w_in_b, cos_p, sin_p, prompt=True, rows_per_seq=seq)
    mk, mv, mkb, mvT = _mem_kv(mem_prompt.reshape(n_seq * N_MEM, D_MODEL), w_mem_b)
    ob = _diff_attention(bqT, bkb, bvT, lam_p, subln_g, n_seq=n_seq, seq=seq, lambda_init=lambda_init)
    biasT = jnp.swapaxes(_band_bias_prompt(table), 1, 2) * LOG2E
    y_p = _merge_prompt(x2, aqT, akb, avT, mqT, mkb, mvT, ob, g, biasT, w_out_b, lng, lnb,
                        seq=seq, alpha=alpha)

    xs2 = x_sample.reshape(n_streams * t_new, D_MODEL)
    pos_s = past_b + jnp.arange(t_new)
    cos_s, sin_s = _rope_tables(jnp.tile(pos_s, n_streams))
    sq, sk, sv, tq, tk, tv, nq, sg = _project(xs2, w_in_b, cos_s, sin_s, prompt=False, rows_per_seq=t_new)
    bias_s = _toeplitz_bias(table, t_new, past_a + t_new, past_a)
    biasc, biasn = bias_s[:, :, :past_a], bias_s[:, :, past_a:]
    y_s = _sample_step(
        xs2, sq, sk, sv, tq, tk, tv, nq, sg,
        jnp.transpose(cache_a_k[layer], (0, 2, 3, 1)),
        jnp.transpose(cache_a_v[layer], (0, 2, 3, 1)),
        jnp.transpose(cache_b_k[layer], (0, 2, 3, 4, 1)),
        cache_b_v[layer].reshape(n_streams, past_b * N_HEADS, B_VDIM),
        jnp.transpose(cache_mem_k[layer], (0, 2, 3, 1)),
        jnp.transpose(cache_mem_v[layer], (0, 2, 3, 1)),
        biasc, biasn, lam_p, subln_g, w_out_b, lng, lnb,
        n_streams=n_streams, t_new=t_new, lambda_init=lambda_init, alpha=alpha)

    hd = (N_HEADS, HEAD_DIM)
    return (
        y_p.reshape(n_seq, seq, D_MODEL),
        y_s.reshape(n_streams, t_new, D_MODEL),
        akf.reshape(1, n_seq, BAND_ROWS, *hd),
        avf.reshape(1, n_seq, BAND_ROWS, *hd),
        jnp.transpose(bkf, (0, 4, 1, 2, 3))[None],
        bvf.reshape(1, n_seq, seq, N_HEADS, B_VDIM),
        mk.reshape(1, n_seq, N_MEM, *hd),
        mv.reshape(1, n_seq, N_MEM, *hd),
        sk.reshape(1, n_streams, t_new, *hd),
        sv.reshape(1, n_streams, t_new, *hd),
        tk.reshape(1, n_streams, t_new, N_HEADS, 2, HEAD_DIM),
        tv.reshape(1, n_streams, t_new, N_HEADS, B_VDIM),
    )
```

```python
import functools
import math

import jax
import jax.numpy as jnp
from jax import lax
from jax.experimental import pallas as pl
from jax.experimental.pallas import tpu as pltpu

F32 = jnp.float32
BF16 = jnp.bfloat16

D_MODEL = 1024
CHUNK = 64
N_BAND_CHUNKS = 8
BAND_ROWS = N_BAND_CHUNKS * CHUNK
HEAD_DIM = 64
A_WIDTH = 256
B_WIDTH = 512
B_VDIM = 128
M_WIDTH = 256
N_MEM = 256
N_HEADS = 4
MIX_WIDTH = A_WIDTH + B_WIDTH + M_WIDTH
REL_CLIP = 128
ROPE_THETA = 10000.0
LN_EPS = 1e-5
RMS_EPS = 1e-5
NEG_INF = -1e30
QK_SCALE = HEAD_DIM ** -0.5
LOG2E = math.log2(math.e)

_OFF = {}
_o = 0
for _name, _w in (("aq", A_WIDTH), ("ak", A_WIDTH), ("av", A_WIDTH), ("bq", B_WIDTH), ("bk", B_WIDTH),
                  ("bv", B_WIDTH), ("mq", M_WIDTH), ("gate", MIX_WIDTH)):
    _OFF[_name] = (_o, _o + _w)
    _o += _w
PROJ_TOTAL = _o

LANES = 128
ROW_TILE = 512
KV_TILE = ROW_TILE // 2
ONES_ROWS = 16
MERGE_TILE = 256
VMEM_LIMIT = 56 * 1024 * 1024


def _cparams(sem):
    return pltpu.CompilerParams(dimension_semantics=sem, vmem_limit_bytes=VMEM_LIMIT)


def _rope_slab(x, cos, sin_signed, lo_half):
    left = pltpu.roll(x, LANES - 32, 1)
    right = pltpu.roll(x, 32, 1)
    swapped = jnp.where(lo_half, left, right)
    return x * cos + swapped * sin_signed


def _proj_kernel(x_ref, w_ref, cos_ref, sin_ref, *out_refs, prompt):
    if prompt:
        (aq_ref, akb_ref, avb_ref, akf_ref, avf_ref, bqT_ref, bkb_ref, bvT_ref,
         bkf_ref, bvf_ref, mq_ref, g_ref) = out_refs
    else:
        aq_ref, akf_ref, avf_ref, bq_ref, bkf_ref, bvf_ref, mq_ref, g_ref = out_refs
    xb = x_ref[...].astype(BF16)

    def seg(name):
        lo, hi = _OFF[name]
        return jnp.dot(xb, w_ref[:, lo:hi], preferred_element_type=F32)

    aq = seg("aq")
    ak = seg("ak")
    av = seg("av")
    akf_ref[...] = ak
    avf_ref[...] = av
    if prompt:
        aq_ref[0] = (aq * (QK_SCALE * LOG2E)).T.astype(BF16)
        akb_ref[...] = ak.astype(BF16)
        avT = av.T.astype(BF16)
        for half in range(ROW_TILE // MERGE_TILE):
            avb_ref[half] = avT[:, half * MERGE_TILE:(half + 1) * MERGE_TILE]
    else:
        aq_ref[...] = (aq * QK_SCALE).astype(BF16)

    cos = cos_ref[...]
    sin = sin_ref[...]
    lane = lax.broadcasted_iota(jnp.int32, cos.shape, 1)
    lo_half = (lane % HEAD_DIM) < (HEAD_DIM // 2)
    bq = seg("bq")
    bk = seg("bk")
    bv = seg("bv")
    n_rows = bv.shape[0]
    for c in range(B_WIDTH // LANES):
        sl = slice(c * LANES, (c + 1) * LANES)
        q_c = _rope_slab(bq[:, sl], cos, sin, lo_half)
        k_c = _rope_slab(bk[:, sl], cos, sin, lo_half)
        bvf_ref[pl.ds(c, n_rows, stride=N_HEADS), :] = bv[:, sl]
        if prompt:
            kT_c = k_c.T
            bkf_ref[0, c, 0] = kT_c[:HEAD_DIM]
            bkf_ref[0, c, 1] = kT_c[HEAD_DIM:]
            bkb_ref[:, sl] = k_c.astype(BF16)
            bqT_ref[0, sl, :] = (q_c * (QK_SCALE * LOG2E)).T.astype(BF16)
            vT_c = bv[:, sl].T.astype(BF16)
            for half in range(ROW_TILE // KV_TILE):
                bvT_ref[half, sl, :] = vT_c[:, half * KV_TILE:(half + 1) * KV_TILE]
        else:
            bkf_ref[pl.ds(2 * c, n_rows, stride=2 * N_HEADS), :] = k_c[:, :HEAD_DIM]
            bkf_ref[pl.ds(2 * c + 1, n_rows, stride=2 * N_HEADS), :] = k_c[:, HEAD_DIM:]
            bq_ref[:, sl] = (q_c * QK_SCALE).astype(BF16)

    if prompt:
        mq_ref[0] = (seg("mq") * (QK_SCALE * LOG2E)).T.astype(BF16)
    else:
        mq_ref[...] = (seg("mq") * QK_SCALE).astype(BF16)
    gate = seg("gate")
    g_ref[...] = (gate / (1.0 + jnp.exp(-gate))).astype(BF16)


def _project(x2, w_in_b, cos, sin, *, prompt, rows_per_seq):
    rows = x2.shape[0]
    tm = ROW_TILE
    n_tiles = rows // tm
    pos_tiles = cos.shape[0] // tm
    row_spec = lambda w: pl.BlockSpec((tm, w), lambda i: (i, 0))
    bk_rows, bv_rows = 2 * N_HEADS, N_HEADS
    bkf_shape = jax.ShapeDtypeStruct((rows * bk_rows, HEAD_DIM), F32)
    bvf_shape = jax.ShapeDtypeStruct((rows * bv_rows, B_VDIM), F32)
    bkf_spec = pl.BlockSpec((tm * bk_rows, HEAD_DIM), lambda i: (i, 0))
    bvf_spec = pl.BlockSpec((tm * bv_rows, B_VDIM), lambda i: (i, 0))
    in_specs = [
        row_spec(D_MODEL),
        pl.BlockSpec((D_MODEL, PROJ_TOTAL), lambda i: (0, 0)),
        pl.BlockSpec((tm, LANES), lambda i: (i % pos_tiles, 0)),
        pl.BlockSpec((tm, LANES), lambda i: (i % pos_tiles, 0)),
    ]
    sds = jax.ShapeDtypeStruct
    if prompt:
        tiles_per_seq = rows_per_seq // tm
        n_seq = rows // rows_per_seq
        tail_spec = pl.BlockSpec((tm, A_WIDTH), lambda i: (i // tiles_per_seq, 0))
        t_spec = pl.BlockSpec((1, B_WIDTH, tm), lambda i: (i, 0, 0))
        kv_per_row_tile = tm // KV_TILE
        vt_spec = pl.BlockSpec((kv_per_row_tile, B_WIDTH, KV_TILE), lambda i: (i, 0, 0))
        bkf_shape = jax.ShapeDtypeStruct((n_seq, N_HEADS, 2, HEAD_DIM, rows_per_seq), F32)
        bkf_spec = pl.BlockSpec((1, N_HEADS, 2, HEAD_DIM, tm),
                                lambda i: (i // tiles_per_seq, 0, 0, 0, i % tiles_per_seq))
        qt_spec = pl.BlockSpec((1, A_WIDTH, tm), lambda i: (i, 0, 0))
        merge_per_row_tile = tm // MERGE_TILE
        avt_spec = pl.BlockSpec((merge_per_row_tile, A_WIDTH, MERGE_TILE), lambda i: (i, 0, 0))
        out_shape = (
            sds((n_tiles, A_WIDTH, tm), BF16), sds((rows, A_WIDTH), BF16),
            sds((n_tiles * merge_per_row_tile, A_WIDTH, MERGE_TILE), BF16),
            sds((n_seq * tm, A_WIDTH), F32), sds((n_seq * tm, A_WIDTH), F32),
            sds((n_tiles, B_WIDTH, tm), BF16), sds((rows, B_WIDTH), BF16),
            sds((n_tiles * kv_per_row_tile, B_WIDTH, KV_TILE), BF16),
            bkf_shape, bvf_shape,
            sds((n_tiles, M_WIDTH, tm), BF16), sds((rows, MIX_WIDTH), BF16),
        )
        out_specs = (
            qt_spec, row_spec(A_WIDTH), avt_spec, tail_spec, tail_spec,
            t_spec, row_spec(B_WIDTH), vt_spec, bkf_spec, bvf_spec,
            qt_spec, row_spec(MIX_WIDTH),
        )
    else:
        out_shape = (
            sds((rows, A_WIDTH), BF16), sds((rows, A_WIDTH), F32), sds((rows, A_WIDTH), F32),
            sds((rows, B_WIDTH), BF16), bkf_shape, bvf_shape,
            sds((rows, M_WIDTH), BF16), sds((rows, MIX_WIDTH), BF16),
        )
        out_specs = (
            row_spec(A_WIDTH), row_spec(A_WIDTH), row_spec(A_WIDTH),
            row_spec(B_WIDTH), bkf_spec, bvf_spec,
            row_spec(M_WIDTH), row_spec(MIX_WIDTH),
        )
    return pl.pallas_call(
        functools.partial(_proj_kernel, prompt=prompt),
        out_shape=out_shape,
        grid=(n_tiles,),
        in_specs=in_specs,
        out_specs=out_specs,
        compiler_params=_cparams(("arbitrary",)),
        name="proj_prompt" if prompt else "proj_sample",
    )(x2, w_in_b, cos, sin)


def _mem_kv_kernel(mem_ref, w_ref, mk_ref, mv_ref, mkb_ref, mvT_ref):
    kv = jnp.dot(mem_ref[...].astype(BF16), w_ref[...], preferred_element_type=F32)
    mk_ref[...] = kv[:, :M_WIDTH]
    mv_ref[...] = kv[:, M_WIDTH:]
    mkb_ref[...] = kv[:, :M_WIDTH].astype(BF16)
    mvT_ref[...] = kv[:, M_WIDTH:].T.astype(BF16)


def _mem_kv(mem2, w_b):
    rows = mem2.shape[0]
    tm = N_MEM
    spec = pl.BlockSpec((tm, M_WIDTH), lambda i: (i, 0))
    sds = jax.ShapeDtypeStruct
    return pl.pallas_call(
        _mem_kv_kernel,
        out_shape=(sds((rows, M_WIDTH), F32), sds((rows, M_WIDTH), F32),
                   sds((rows, M_WIDTH), BF16), sds((rows, N_MEM), BF16)),
        grid=(rows // tm,),
        in_specs=[pl.BlockSpec((tm, D_MODEL), lambda i: (i, 0)),
                  pl.BlockSpec((D_MODEL, 2 * M_WIDTH), lambda i: (0, 0))],
        out_specs=(spec, spec, spec, spec),
        compiler_params=_cparams(("arbitrary",)),
        name="mem_kv",
    )(mem2, w_b)


def _lambda_value(lp, lambda_init):
    a = jnp.sum(lp[0:1, :] * lp[1:2, :], axis=1, keepdims=True)
    b = jnp.sum(lp[2:3, :] * lp[3:4, :], axis=1, keepdims=True)
    return jnp.exp(a) - jnp.exp(b) + lambda_init


def _diff_post(o, g, lambda_init):
    ms = jnp.mean(o * o, axis=-1, keepdims=True)
    return o * lax.rsqrt(ms + RMS_EPS) * g * (1.0 - lambda_init)


def _diff_attn_kernel(qT_ref, k_ref, vT_ref, lam_ref, g_ref, o_ref,
                      s0_ref, s1_ref, p0_ref, p1_ref, a0_ref, a1_ref, t0_ref, t1_ref, m_ref, acc_ref,
                      *, tq, tk, lambda_init):
    n_q = qT_ref.shape[0]

    def query_maps(t):
        qT = qT_ref[t].astype(F32)
        row = lax.broadcasted_iota(jnp.int32, qT.shape, 0)
        return (jnp.where(row < HEAD_DIM, qT, 0.0).astype(BF16),
                jnp.where(row >= HEAD_DIM, qT, 0.0).astype(BF16))

    s_refs = (s0_ref, s1_ref)
    p_refs = (p0_ref, p1_ref)
    a_refs = (a0_ref, a1_ref)
    t_refs = (t0_ref, t1_ref)
    ones_rows = jnp.ones((ONES_ROWS, tk), BF16)

    every = slice(None)
    upper = slice(tq // 2, tq)

    def qk(q_maps, j, buf, visible=None, cols=every):
        k = k_ref[pl.ds(pl.multiple_of(j * tk, tk), tk), :]
        for mp in range(2):
            s = jnp.dot(k, q_maps[mp][:, cols], preferred_element_type=F32)
            if visible is not None:
                s = jnp.where(visible[:, cols], s, NEG_INF)
            s_refs[buf][mp, :, cols] = s
            t_refs[buf][mp, :, cols] = jnp.max(s, axis=0, keepdims=True)

    def softmax(buf, cols=every):
        for mp in range(2):
            m_old = m_ref[mp, :, cols]
            m_new = jnp.maximum(m_old, t_refs[buf][mp, :, cols])
            a_refs[buf][mp, :, cols] = jnp.exp2(m_old - m_new)
            m_ref[mp, :, cols] = m_new
            p_refs[buf][mp, :, cols] = jnp.exp2(s_refs[buf][mp, :, cols] - m_new).astype(BF16)

    def pv(j, buf, cols=every):
        vT = jnp.concatenate([vT_ref[j], ones_rows], axis=0)
        for mp in range(2):
            upd = jnp.dot(vT, p_refs[buf][mp, :, cols], preferred_element_type=F32)
            acc_ref[mp, :, cols] = a_refs[buf][mp, :, cols] * acc_ref[mp, :, cols] + upd

    def diag_visible(half):
        kc = lax.broadcasted_iota(jnp.int32, (tk, tq), 0) // CHUNK + half * (tk // CHUNK)
        qc = lax.broadcasted_iota(jnp.int32, (tk, tq), 1) // CHUNK
        return kc <= qc

    def tick_pair(q_maps, j, masked):
        pv(j - 2, 0)
        qk(q_maps, j, 0, diag_visible(0) if masked else None)
        softmax(1)
        pv(j - 1, 1)
        if masked:
            qk(q_maps, j + 1, 1, diag_visible(1), cols=upper)
        else:
            qk(q_maps, j + 1, 1)
        softmax(0)

    def start_tile(t, masked):
        q_maps = query_maps(t)
        acc_ref[...] = jnp.zeros_like(acc_ref)
        m_ref[...] = jnp.full(m_ref.shape, NEG_INF, F32)
        qk(q_maps, 0, 0, diag_visible(0) if masked else None)
        qk(q_maps, 1, 1, diag_visible(1) if masked else None)
        softmax(0)

    lam = _lambda_value(lam_ref[...], lambda_init)

    def query_tile(t, carry):
        q_maps = query_maps(t)

        unroll = 5

        def body(i, c):
            for u in range(unroll):
                tick_pair(q_maps, 2 + 2 * u + 2 * unroll * i, False)
            return c

        n_plain = jnp.maximum(t - 1, 0)
        lax.fori_loop(0, n_plain // unroll, body, 0)
        left = n_plain % unroll

        @pl.when((left == 0) & (t > 0))
        def _():
            tick_pair(q_maps, 2 * t, True)

        for n_left in range(1, unroll):
            @pl.when(left == n_left)
            def _(n_left=n_left):
                for u in range(n_left, 0, -1):
                    tick_pair(q_maps, 2 * t - 2 * u, False)
                tick_pair(q_maps, 2 * t, True)

        pv(2 * t, 0)
        softmax(1, cols=upper)
        pv(2 * t + 1, 1, cols=upper)
        inv0 = 1.0 / acc_ref[0, B_VDIM:B_VDIM + 1]
        inv1 = 1.0 / acc_ref[1, B_VDIM:B_VDIM + 1]
        oT = acc_ref[0, :B_VDIM] * inv0 - lam * (acc_ref[1, :B_VDIM] * inv1)
        o_ref[pl.ds(pl.multiple_of(t * tq, tq), tq), :] = _diff_post(oT.T, g_ref[...], lambda_init)
        start_tile(jnp.minimum(t + 1, n_q - 1), False)
        return carry

    start_tile(0, True)
    lax.fori_loop(0, n_q, query_tile, 0)


def _diff_attention(bqT, bkb, bvT, lam_p, subln_g, *, n_seq, seq, lambda_init):
    tq, tk = ROW_TILE, KV_TILE
    nq = seq // tq
    nk = seq // tk
    rows = n_seq * seq
    bvT4 = bvT.reshape(n_seq, nk, B_WIDTH, tk)
    return pl.pallas_call(
        functools.partial(_diff_attn_kernel, tq=tq, tk=tk, lambda_init=lambda_init),
        out_shape=jax.ShapeDtypeStruct((rows, B_WIDTH), F32),
        grid=(n_seq, N_HEADS),
        in_specs=[
            pl.BlockSpec((nq, B_VDIM, tq), lambda b, h: (b, h, 0)),
            pl.BlockSpec((seq, B_VDIM), lambda b, h: (b, h)),
            pl.BlockSpec((None, nk, B_VDIM, tk), lambda b, h: (b, 0, h, 0)),
            pl.BlockSpec((4, HEAD_DIM), lambda b, h: (0, 0)),
            pl.BlockSpec((1, B_VDIM), lambda b, h: (0, 0)),
        ],
        out_specs=pl.BlockSpec((seq, B_VDIM), lambda b, h: (b, h)),
        scratch_shapes=[
            pltpu.VMEM((2, tk, tq), F32), pltpu.VMEM((2, tk, tq), F32),
            pltpu.VMEM((2, tk, tq), BF16), pltpu.VMEM((2, tk, tq), BF16),
            pltpu.VMEM((2, 1, tq), F32), pltpu.VMEM((2, 1, tq), F32),
            pltpu.VMEM((2, 1, tq), F32), pltpu.VMEM((2, 1, tq), F32),
            pltpu.VMEM((2, 1, tq), F32),
            pltpu.VMEM((2, B_VDIM + ONES_ROWS, tq), F32),
        ],
        compiler_params=_cparams(("arbitrary", "arbitrary")),
        name="diff_attn",
    )(bqT, bkb, bvT4, lam_p, subln_g)


def _head_masks(width):
    lane = lax.broadcasted_iota(jnp.int32, (1, width), 1)
    return [(lane >= h * HEAD_DIM) & (lane < (h + 1) * HEAD_DIM) for h in range(N_HEADS)]


def _dot_nt(a, b):
    return lax.dot_general(a, b, (((1,), (1,)), ((), ())), preferred_element_type=F32)


def _softmax_parts(blocks):
    m = functools.reduce(jnp.maximum, [jnp.max(s, axis=-1, keepdims=True) for s in blocks])
    ps = [jnp.exp(s - m) for s in blocks]
    l = functools.reduce(jnp.add, [jnp.sum(p, axis=-1, keepdims=True) for p in ps])
    return ps, l


def _softmax_pv(blocks, vals):
    ps, l = _softmax_parts(blocks)
    o = functools.reduce(jnp.add, [jnp.dot(p.astype(BF16), vv, preferred_element_type=F32)
                                   for p, vv in zip(ps, vals)])
    return o * (1.0 / l)


def _gate_mix(mixed, g):
    return (mixed * g.astype(F32)).astype(BF16)


def _out_norm(x, mg, w_out_ref, lng, lnb, alpha):
    return _residual_norm(x, jnp.dot(mg, w_out_ref[...], preferred_element_type=F32), lng, lnb, alpha)


def _residual_norm(x, y, lng, lnb, alpha):
    z = alpha * x + y
    mu = jnp.mean(z, axis=-1, keepdims=True)
    zc = z - mu
    var = jnp.mean(zc * zc, axis=-1, keepdims=True)
    return zc * lax.rsqrt(var + LN_EPS) * lng + lnb


def _softmax_pv_t(blocks, vT_blocks, ones_rows):
    m = functools.reduce(jnp.maximum, [jnp.max(s, axis=0, keepdims=True) for s in blocks])
    d = vT_blocks[0].shape[0]
    acc = functools.reduce(jnp.add, [
        jnp.dot(jnp.concatenate([vT, ones_rows], axis=0), jnp.exp2(s - m).astype(BF16),
                preferred_element_type=F32)
        for s, vT in zip(blocks, vT_blocks)])
    return acc[:d] * (1.0 / acc[d:d + 1])


def _merge_kernel(xprev_ref, aqT_ref, k0_ref, k1_ref, k2_ref, vT0_ref, vT1_ref, vT2_ref, mqT_ref, mk_ref, mvT_ref,
                  ob_ref, g_ref, biasT_ref, wout_ref, lng_ref, lnb_ref, y_ref, mg_ref,
                  *, tiles_per_seq, n_tiles, alpha):
    step = pl.program_id(0)

    @pl.when(step == 0)
    def _():
        mg_ref[...] = jnp.zeros_like(mg_ref)

    t = jnp.minimum(step, n_tiles - 1) % tiles_per_seq
    tile = MERGE_TILE
    aqT = aqT_ref[0].astype(F32)
    mqT = mqT_ref[0].astype(F32)
    row = lax.broadcasted_iota(jnp.int32, aqT.shape, 0)
    keys = [k0_ref[...], k1_ref[...], k2_ref[...]]
    vTs = [vT0_ref[0], vT1_ref[0], vT2_ref[0]]
    mk = mk_ref[...]
    mvT = mvT_ref[...]
    ones_rows = jnp.ones((ONES_ROWS, tile), BF16)
    pen0 = jnp.where(t >= 2, 0.0, NEG_INF).astype(F32)
    pen1 = jnp.where(t >= 1, 0.0, NEG_INF).astype(F32)
    pens = (pen0, pen1, None)
    a_scores, m_scores = [], []
    for h in range(N_HEADS):
        in_head = (row >= h * HEAD_DIM) & (row < (h + 1) * HEAD_DIM)
        qh = jnp.where(in_head, aqT, 0.0).astype(BF16)
        blocks = []
        for b in range(3):
            s = jnp.dot(keys[b], qh, preferred_element_type=F32) + biasT_ref[h, b * tile:(b + 1) * tile, :]
            blocks.append(s if pens[b] is None else s + pens[b])
        a_scores.append(blocks)
        qmh = jnp.where(in_head, mqT, 0.0).astype(BF16)
        m_scores.append([jnp.dot(mk, qmh, preferred_element_type=F32)])
    y_ref[...] = _out_norm(xprev_ref[...], mg_ref[...], wout_ref, lng_ref[...], lnb_ref[...], alpha)
    o_aT, o_mT = [], []
    for h in range(N_HEADS):
        hs = slice(h * HEAD_DIM, (h + 1) * HEAD_DIM)
        o_aT.append(_softmax_pv_t(a_scores[h], [vT[hs] for vT in vTs], ones_rows))
        o_mT.append(_softmax_pv_t(m_scores[h], [mvT[hs]], ones_rows))
    o_a = jnp.concatenate(o_aT, axis=0).T
    o_m = jnp.concatenate(o_mT, axis=0).T
    mixed = jnp.concatenate([o_a, ob_ref[...], o_m], axis=1)
    mg_ref[...] = _gate_mix(mixed, g_ref[...])


def _merge_prompt(x2, aqT, akb, avT, mqT, mkb, mvT, ob, g, biasT, w_out_b, lng, lnb, *, seq, alpha):
    rows = x2.shape[0]
    tile = MERGE_TILE
    tps = seq // tile
    n_tiles = rows // tile
    per_row_tile = ROW_TILE // tile

    cur = lambda s: jnp.minimum(s, n_tiles - 1)
    done = lambda s: jnp.maximum(s - 1, 0)

    def band(d):
        return lambda s: (cur(s) // tps) * tps + jnp.maximum(cur(s) % tps - d, 0)

    cur_spec = lambda w: pl.BlockSpec((tile, w), lambda s: (cur(s), 0))
    done_spec = pl.BlockSpec((tile, D_MODEL), lambda s: (done(s), 0))
    qt_spec = pl.BlockSpec((1, A_WIDTH, tile), lambda s: (cur(s) // per_row_tile, 0, cur(s) % per_row_tile))
    k_specs = [pl.BlockSpec((tile, A_WIDTH), lambda s, f=band(d): (f(s), 0)) for d in (2, 1, 0)]
    vt_specs = [pl.BlockSpec((1, A_WIDTH, tile), lambda s, f=band(d): (f(s), 0, 0)) for d in (2, 1, 0)]
    mem_spec = pl.BlockSpec((N_MEM, M_WIDTH), lambda s: (cur(s) // tps, 0))
    const = lambda shape: pl.BlockSpec(shape, lambda s: (0,) * len(shape))
    return pl.pallas_call(
        functools.partial(_merge_kernel, tiles_per_seq=tps, n_tiles=n_tiles, alpha=alpha),
        out_shape=jax.ShapeDtypeStruct((rows, D_MODEL), F32),
        grid=(n_tiles + 1,),
        in_specs=[done_spec, qt_spec] + k_specs + vt_specs + [
            qt_spec, mem_spec, mem_spec,
            cur_spec(B_WIDTH), cur_spec(MIX_WIDTH),
            const((N_HEADS, 3 * tile, tile)), const((MIX_WIDTH, D_MODEL)), const((1, D_MODEL)), const((1, D_MODEL)),
        ],
        out_specs=done_spec,
        scratch_shapes=[pltpu.VMEM((tile, MIX_WIDTH), BF16)],
        compiler_params=_cparams(("arbitrary",)),
        name="merge_prompt",
    )(x2, aqT, akb, akb, akb, avT, avT, avT, mqT, mkb, mvT, ob, g, biasT, w_out_b, lng, lnb)


def _sample_kernel(x_ref, sq_ref, sk_ref, sv_ref, tq_ref, tk_ref, tv_ref, nq_ref, g_ref,
                   cak_ref, cav_ref, cbk_ref, cbv_ref, cmk_ref, cmv_ref,
                   biasc_ref, biasn_ref, lam_ref, subg_ref, wout_ref, lng_ref, lnb_ref, y_ref,
                   *, lambda_init, alpha):
    masks = _head_masks(A_WIDTH)
    sq = sq_ref[...].astype(F32)
    nq = nq_ref[...].astype(F32)
    sk_new = sk_ref[...].astype(BF16)
    sv = sv_ref[...]
    rows = sq.shape[0]

    def head_cols(x, h):
        pair = x[:, (h // 2) * LANES:(h // 2 + 1) * LANES]
        if h % 2:
            pair = pltpu.roll(pair, HEAD_DIM, 1)
        return pair[:, :HEAD_DIM].astype(BF16)

    a_scores, m_scores = [], []
    for h in range(N_HEADS):
        qh = jnp.where(masks[h], sq, 0.0).astype(BF16)
        s_cache = jnp.dot(head_cols(sq, h), cak_ref[0, h].astype(BF16), preferred_element_type=F32)
        a_scores.append([s_cache + biasc_ref[h], _dot_nt(qh, sk_new) + biasn_ref[h]])
        m_scores.append([jnp.dot(head_cols(nq, h), cmk_ref[0, h].astype(BF16), preferred_element_type=F32)])
    past_b = cbv_ref.shape[1] // N_HEADS
    b_scores = []
    for h in range(N_HEADS):
        q = tq_ref[:, h * B_VDIM:(h + 1) * B_VDIM].astype(F32)
        q_maps = [q[:, :HEAD_DIM], pltpu.roll(q, HEAD_DIM, 1)[:, :HEAD_DIM]]
        for mp in range(2):
            q_m = q_maps[mp].astype(BF16)
            k_new = tk_ref[pl.ds(2 * h + mp, rows, stride=2 * N_HEADS), :].astype(BF16)
            b_scores.append([jnp.dot(q_m, cbk_ref[0, h, mp].astype(BF16), preferred_element_type=F32),
                             _dot_nt(q_m, k_new)])

    pieces = []
    for h in range(N_HEADS):
        ps, l = _softmax_parts(a_scores[h])
        o = (_dot_nt(ps[0].astype(BF16), cav_ref[0, h].astype(BF16))
             + jnp.dot(ps[1].astype(BF16), head_cols(sv, h), preferred_element_type=F32))
        pieces.append((o * (1.0 / l), h * HEAD_DIM))
        ps, l = _softmax_parts(m_scores[h])
        o = _dot_nt(ps[0].astype(BF16), cmv_ref[0, h].astype(BF16))
        pieces.append((o * (1.0 / l), A_WIDTH + B_WIDTH + h * HEAD_DIM))
    lam = _lambda_value(lam_ref[...], lambda_init)
    subg = subg_ref[...]
    for h in range(N_HEADS):
        vals = [cbv_ref[0, pl.ds(h, past_b, stride=N_HEADS), :].astype(BF16),
                tv_ref[pl.ds(h, rows, stride=N_HEADS), :].astype(BF16)]
        o_maps = [_softmax_pv(b_scores[2 * h + mp], vals) for mp in range(2)]
        pieces.append((_diff_post(o_maps[0] - lam * o_maps[1], subg, lambda_init), A_WIDTH + h * B_VDIM))

    g = g_ref[...].astype(F32)
    y = None
    for o, col in pieces:
        width = o.shape[1]
        if width == LANES:
            g_cols = g[:, col:col + LANES]
        else:
            g_pair = g[:, (col // LANES) * LANES:(col // LANES + 1) * LANES]
            if col % LANES:
                g_pair = pltpu.roll(g_pair, LANES - col % LANES, 1)
            g_cols = g_pair[:, :width]
        part = jnp.dot((o * g_cols).astype(BF16), wout_ref[col:col + width, :], preferred_element_type=F32)
        y = part if y is None else y + part
    y_ref[...] = _residual_norm(x_ref[...], y, lng_ref[...], lnb_ref[...], alpha)


def _sample_step(xs2, sq, sk, sv, tq, tk, tv, nq, g, cak, cav, cbk, cbv, cmk, cmv, biasc, biasn,
                 lam_p, subln_g, w_out_b, lng, lnb, *, n_streams, t_new, lambda_init, alpha):
    past_a = cav.shape[3]
    bk_rows, bv_rows = 2 * N_HEADS, N_HEADS
    head_t_spec = lambda r: pl.BlockSpec((1, N_HEADS, HEAD_DIM, r), lambda n: (n, 0, 0, 0))
    past_b = cbv.shape[1] // bv_rows
    row_spec = lambda w: pl.BlockSpec((t_new, w), lambda n: (n, 0))
    nrow_spec = lambda r, w: pl.BlockSpec((t_new * r, w), lambda n: (n, 0))
    cache_spec = lambda r, w: pl.BlockSpec((1, r, w), lambda n: (n, 0, 0))
    const = lambda shape: pl.BlockSpec(shape, lambda n: (0,) * len(shape))
    return pl.pallas_call(
        functools.partial(_sample_kernel, lambda_init=lambda_init, alpha=alpha),
        out_shape=jax.ShapeDtypeStruct((n_streams * t_new, D_MODEL), F32),
        grid=(n_streams,),
        in_specs=[
            row_spec(D_MODEL), row_spec(A_WIDTH), row_spec(A_WIDTH), row_spec(A_WIDTH),
            row_spec(B_WIDTH), nrow_spec(bk_rows, HEAD_DIM), nrow_spec(bv_rows, B_VDIM),
            row_spec(M_WIDTH), row_spec(MIX_WIDTH),
            head_t_spec(past_a), head_t_spec(past_a),
            pl.BlockSpec((1, N_HEADS, 2, HEAD_DIM, past_b), lambda n: (n, 0, 0, 0, 0)),
            cache_spec(past_b * bv_rows, B_VDIM),
            head_t_spec(N_MEM), head_t_spec(N_MEM),
            const((N_HEADS, t_new, past_a)), const((N_HEADS, t_new, t_new)),
            const((4, HEAD_DIM)), const((1, B_VDIM)),
            const((MIX_WIDTH, D_MODEL)), const((1, D_MODEL)), const((1, D_MODEL)),
        ],
        out_specs=row_spec(D_MODEL),
        compiler_params=_cparams(("arbitrary",)),
        name="sample_step",
    )(xs2, sq, sk, sv, tq, tk, tv, nq, g, cak, cav, cbk, cbv, cmk, cmv, biasc, biasn,
      lam_p, subln_g, w_out_b, lng, lnb)


def _rope_tables(pos):
    half = HEAD_DIM // 2
    inv = ROPE_THETA ** (-jnp.arange(half, dtype=F32) / half)
    ang = pos.astype(F32)[:, None] * inv[None, :]
    cos = jnp.cos(ang)
    sin = jnp.sin(ang)
    return jnp.tile(cos, (1, LANES // half)), jnp.concatenate([-sin, sin, -sin, sin], axis=-1)


def _rel_bias(table, dist):
    return table[:, jnp.clip(dist, -REL_CLIP, REL_CLIP) + REL_CLIP].astype(F32)


def _toeplitz_bias(table, n, width, offset):
    period = -(-(n + width) // LANES) * LANES
    u = jnp.arange(period)
    dist = jnp.where(u < width, offset - u, offset + period - u)
    diagonals = _rel_bias(table, dist)
    flat = jnp.tile(diagonals, (1, n))[:, :n * (period - 1)]
    return flat.reshape(table.shape[0], n, period - 1)[:, :, :width]


def _band_bias_prompt(table):
    n, width = MERGE_TILE, 3 * MERGE_TILE
    bias = _toeplitz_bias(table, n, width, BAND_ROWS)
    i = jnp.arange(n)[:, None]
    j = jnp.arange(width)[None, :]
    qc = i // CHUNK
    kc = j // CHUNK
    visible = (kc >= qc) & (kc <= qc + N_BAND_CHUNKS)
    return jnp.where(visible[None], bias, NEG_INF)


def kernel(x_prompt, x_sample, cache_a_k, cache_a_v, cache_b_k, cache_b_v, cache_mem_k, cache_mem_v, mem_prompt,
           w_in, w_mem_kv, a_rel_bias, diff_lambda, diff_subln_g, w_out, ln_g, ln_b):
    depth = w_in.shape[0]
    assert depth == 1, "single-layer step only"
    n_seq, seq, _ = x_prompt.shape
    n_streams, t_new, _ = x_sample.shape
    past_a = cache_a_k.shape[2]
    past_b = cache_b_k.shape[2]
    assert seq % ROW_TILE == 0 and BAND_ROWS == ROW_TILE == 2 * MERGE_TILE and past_a == BAND_ROWS
    assert n_streams * t_new == ROW_TILE
    layer = 0
    lambda_init = 0.8 - 0.6 * math.exp(-0.3 * layer)
    alpha = (2.0 * depth) ** 0.25

    w_in_b = w_in[layer].astype(BF16)
    w_mem_b = w_mem_kv[layer].astype(BF16)
    w_out_b = w_out[layer].astype(BF16)
    table = a_rel_bias[layer]
    lam_p = diff_lambda[layer]
    subln_g = diff_subln_g[layer].reshape(1, B_VDIM)
    lng = ln_g[layer].reshape(1, D_MODEL)
    lnb = ln_b[layer].reshape(1, D_MODEL)

    rows = n_seq * seq
    x2 = x_prompt.reshape(rows, D_MODEL)
    cos_p, sin_p = _rope_tables(jnp.arange(seq))
    (aqT, akb, avT, akf, avf, bqT, bkb, bvT, bkf, bvf, mqT, g) = _project(
        x2, w_in_b, cos_p, sin_p, prompt=True, rows_per_seq=seq)
    mk, mv, mkb, mvT = _mem_kv(mem_prompt.reshape(n_seq * N_MEM, D_MODEL), w_mem_b)
    ob = _diff_attention(bqT, bkb, bvT, lam_p, subln_g, n_seq=n_seq, seq=seq, lambda_init=lambda_init)
    biasT = jnp.swapaxes(_band_bias_prompt(table), 1, 2) * LOG2E
    y_p = _merge_prompt(x2, aqT, akb, avT, mqT, mkb, mvT, ob, g, biasT, w_out_b, lng, lnb,
                        seq=seq, alpha=alpha)

    xs2 = x_sample.reshape(n_streams * t_new, D_MODEL)
    pos_s = past_b + jnp.arange(t_new)
    cos_s, sin_s = _rope_tables(jnp.tile(pos_s, n_streams))
    sq, sk, sv, tq, tk, tv, nq, sg = _project(xs2, w_in_b, cos_s, sin_s, prompt=False, rows_per_seq=t_new)
    bias_s = _toeplitz_bias(table, t_new, past_a + t_new, past_a)
    biasc, biasn = bias_s[:, :, :past_a], bias_s[:, :, past_a:]
    y_s = _sample_step(
        xs2, sq, sk, sv, tq, tk, tv, nq, sg,
        jnp.transpose(cache_a_k[layer], (0, 2, 3, 1)),
        jnp.transpose(cache_a_v[layer], (0, 2, 3, 1)),
        jnp.transpose(cache_b_k[layer], (0, 2, 3, 4, 1)),
        cache_b_v[layer].reshape(n_streams, past_b * N_HEADS, B_VDIM),
        jnp.transpose(cache_mem_k[layer], (0, 2, 3, 1)),
        jnp.transpose(cache_mem_v[layer], (0, 2, 3, 1)),
        biasc, biasn, lam_p, subln_g, w_out_b, lng, lnb,
        n_streams=n_streams, t_new=t_new, lambda_init=lambda_init, alpha=alpha)

    hd = (N_HEADS, HEAD_DIM)
    return (
        y_p.reshape(n_seq, seq, D_MODEL),
        y_s.reshape(n_streams, t_new, D_MODEL),
        akf.reshape(1, n_seq, BAND_ROWS, *hd),
        avf.reshape(1, n_seq, BAND_ROWS, *hd),
        jnp.transpose(bkf, (0, 4, 1, 2, 3))[None],
        bvf.reshape(1, n_seq, seq, N_HEADS, B_VDIM),
        mk.reshape(1, n_seq, N_MEM, *hd),
        mv.reshape(1, n_seq, N_MEM, *hd),
        sk.reshape(1, n_streams, t_new, *hd),
        sv.reshape(1, n_streams, t_new, *hd),
        tk.reshape(1, n_streams, t_new, N_HEADS, 2, HEAD_DIM),
        tv.reshape(1, n_streams, t_new, N_HEADS, B_VDIM),
    )
```
